```python
import jax, jax.numpy as jnp
from jax import lax
import numpy as np

D_MODEL = 2048
BATCH = 4
SEQ = 8192
DEPTH = 2
DEC_BATCH = 8
DEC_SEQ = 32
PAST_LEN = 1024

CHUNK = 64
N_A = DEPTH // 2
N_B = DEPTH - N_A
RET_HEADS = 8
RET_DK = D_MODEL // RET_HEADS
RET_DV = 2 * D_MODEL // RET_HEADS
RET_QK = RET_HEADS * RET_DK
RET_V = RET_HEADS * RET_DV
ROPE_BASE = 10000.0
FOX_HEADS = 16
FOX_HD = D_MODEL // FOX_HEADS
FOX_KV_HEADS = 4
FOX_GROUP = FOX_HEADS // FOX_KV_HEADS
Q_BLOCK = 128
D_FF = -(-8 * D_MODEL // (3 * 256)) * 256
NORM_EPS = 1e-6
GN_EPS = 1e-5

kernel_name = 'yoco_retention_fox_stream_step'

F32 = jnp.float32


def _rmsnorm(x, g):
    x32 = x.astype(F32)
    y = x32 * lax.rsqrt(jnp.mean(x32 * x32, axis=-1, keepdims=True) + NORM_EPS)
    return (y * g.astype(F32)).astype(x.dtype)


def _swiglu(xn, wg, wu, wd):
    return (jax.nn.silu(xn @ wg) * (xn @ wu)) @ wd


def _rope(x, pos):
    half = x.shape[-1] // 2
    inv = ROPE_BASE ** (-jnp.arange(half, dtype=F32) / half)
    ang = pos.astype(F32)[:, None] * inv[None, :]
    cos = jnp.cos(ang)[None, :, None, :]
    sin = jnp.sin(ang)[None, :, None, :]
    x1, x2 = x[..., :half], x[..., half:]
    return jnp.concatenate([x1 * cos - x2 * sin, x1 * sin + x2 * cos], axis=-1)


def _log_gamma():
    return jnp.log(1.0 - jnp.exp2(-5.0 - jnp.arange(RET_HEADS, dtype=F32)))


def _ret_chunk(S, q, k, v, lg):
    c = q.shape[1]
    idx = jnp.arange(c, dtype=F32)
    diff = idx[:, None] - idx[None, :]
    dmask = jnp.where(diff >= 0, jnp.exp(lg[:, None, None] * jnp.maximum(diff, 0.0)), 0.0)
    scores = jnp.einsum('blhd,bmhd->bhlm', q, k) * dmask[None]
    inner = jnp.einsum('bhlm,bmhe->blhe', scores, v)
    cross = jnp.einsum('blhd,bhde->blhe', q, S) * jnp.exp(lg[None, :] * (idx[:, None] + 1.0))[None, :, :, None]
    kw = jnp.exp(lg[None, :] * (c - 1.0 - idx)[:, None])
    S_new = jnp.exp(lg * c)[None, :, None, None] * S + jnp.einsum('blhd,blhe->bhde', k * kw[None, :, :, None], v)
    return S_new, inner + cross


def _retention(xn, S0, pos, w_in, w_o):
    B, L, _ = xn.shape
    proj = xn @ w_in
    q, k, v, g = jnp.split(proj, [RET_QK, 2 * RET_QK, 2 * RET_QK + RET_V], axis=-1)
    q = _rope(q.reshape(B, L, RET_HEADS, RET_DK).astype(F32), pos)
    k = _rope(k.reshape(B, L, RET_HEADS, RET_DK).astype(F32), pos) * (RET_DK ** -0.5)
    v = v.reshape(B, L, RET_HEADS, RET_DV).astype(F32)
    c = min(CHUNK, L)
    n = L // c
    def to_chunks(t):
        return t.reshape(B, n, c, *t.shape[2:]).swapaxes(0, 1)
    lg = _log_gamma()
    S_fin, o = lax.scan(lambda S, xs: _ret_chunk(S, xs[0], xs[1], xs[2], lg), S0.astype(F32),
                        (to_chunks(q), to_chunks(k), to_chunks(v)))
    o = o.swapaxes(0, 1).reshape(B, L, RET_HEADS, RET_DV)
    mu = jnp.mean(o, axis=-1, keepdims=True)
    var = jnp.mean(jnp.square(o - mu), axis=-1, keepdims=True)
    o = (o - mu) * lax.rsqrt(var + GN_EPS)
    o = o.reshape(B, L, RET_V) * jax.nn.silu(g.astype(F32))
    return o.astype(xn.dtype) @ w_o, S_fin


def _shared_kv(h, norm_kv, w_k, w_v, w_f, b_f):
    B, L, _ = h.shape
    hn = _rmsnorm(h, norm_kv)
    k = (hn @ w_k).reshape(B, L, FOX_KV_HEADS, FOX_HD)
    v = (hn @ w_v).reshape(B, L, FOX_KV_HEADS, FOX_HD)
    logf = jax.nn.log_sigmoid((hn @ w_f).astype(F32) + b_f.astype(F32))
    return k, v, logf


def _fox_block(q, fq, qpos, k, v, fk, kpos):
    s = jnp.einsum('bqkgd,bskd->bkgqs', q, k).astype(F32) * (FOX_HD ** -0.5)
    s = s + jnp.transpose(fq, (0, 2, 3, 1))[..., None] - jnp.transpose(fk, (0, 2, 3, 1))[..., None, :]
    mask = kpos[None, :] <= qpos[:, None]
    p = jax.nn.softmax(jnp.where(mask, s, -jnp.inf), axis=-1)
    return jnp.einsum('bkgqs,bskd->bqkgd', p.astype(v.dtype), v)


def _fox(xn, k, v, F, q_start, w_q, w_o):
    B, Lq, _ = xn.shape
    Lk = k.shape[1]
    q = (xn @ w_q).reshape(B, Lq, FOX_KV_HEADS, FOX_GROUP, FOX_HD)
    Fk = F.reshape(B, Lk, FOX_KV_HEADS, FOX_GROUP)
    Fq = Fk[:, q_start:]
    kpos = jnp.arange(Lk)
    qpos = q_start + jnp.arange(Lq)
    qblk = min(Q_BLOCK, Lq)
    nb = Lq // qblk
    def blocks(t):
        return t.reshape(B, nb, qblk, *t.shape[2:]).swapaxes(0, 1)
    o = lax.map(lambda a: _fox_block(a[0], a[1], a[2], k, v, Fk, kpos),
                (blocks(q), blocks(Fq), qpos.reshape(nb, qblk)))
    o = o.swapaxes(0, 1).reshape(B, Lq, FOX_HEADS * FOX_HD)
    return o @ w_o


def setup_inputs(seed: int = 0) -> dict:
    key = jax.random.key(seed)
    ks = jax.random.split(key, 24)
    def w(k, shape, fan_in):
        return jax.random.normal(k, shape, F32) * (fan_in ** -0.5)
    def gain(k, shape):
        return 1.0 + 0.02 * jax.random.normal(k, shape, F32)
    return {
        'x_prompt': jax.random.normal(ks[0], (BATCH, SEQ, D_MODEL), F32),
        'x_sample': jax.random.normal(ks[1], (DEC_BATCH, DEC_SEQ, D_MODEL), F32),
        'state_ret': 0.5 * jax.random.normal(ks[2], (N_A, DEC_BATCH, RET_HEADS, RET_DK, RET_DV), F32),
        'cache_k': jax.random.normal(ks[3], (DEC_BATCH, PAST_LEN, FOX_KV_HEADS, FOX_HD), F32),
        'cache_v': jax.random.normal(ks[4], (DEC_BATCH, PAST_LEN, FOX_KV_HEADS, FOX_HD), F32),
        'cache_logf': jax.nn.log_sigmoid(4.0 + jax.random.normal(ks[5], (DEC_BATCH, PAST_LEN, FOX_HEADS), F32)),
        'norm_mix': gain(ks[6], (DEPTH, D_MODEL)),
        'norm_ffn': gain(ks[7], (DEPTH, D_MODEL)),
        'norm_kv': gain(ks[8], (D_MODEL,)),
        'norm_final': gain(ks[9], (D_MODEL,)),
        'w_ret_in': w(ks[10], (N_A, D_MODEL, 2 * RET_QK + 2 * RET_V), D_MODEL),
        'w_ret_o': w(ks[11], (N_A, RET_V, D_MODEL), RET_V),
        'w_kv_k': w(ks[12], (D_MODEL, FOX_KV_HEADS * FOX_HD), D_MODEL),
        'w_kv_v': w(ks[13], (D_MODEL, FOX_KV_HEADS * FOX_HD), D_MODEL),
        'w_kv_f': w(ks[14], (D_MODEL, FOX_HEADS), D_MODEL),
        'b_kv_f': jnp.linspace(1.0, 6.0, FOX_HEADS, dtype=F32) + 0.1 * jax.random.normal(ks[15], (FOX_HEADS,), F32),
        'w_fox_q': w(ks[16], (N_B, D_MODEL, FOX_HEADS * FOX_HD), D_MODEL),
        'w_fox_o': w(ks[17], (N_B, FOX_HEADS * FOX_HD, D_MODEL), FOX_HEADS * FOX_HD),
        'w_ffn_gate': w(ks[18], (DEPTH, D_MODEL, D_FF), D_MODEL),
        'w_ffn_up': w(ks[19], (DEPTH, D_MODEL, D_FF), D_MODEL),
        'w_ffn_down': w(ks[20], (DEPTH, D_FF, D_MODEL), D_FF),
    }


def reference(x_prompt, x_sample, state_ret, cache_k, cache_v, cache_logf, norm_mix, norm_ffn, norm_kv,
              norm_final, w_ret_in, w_ret_o, w_kv_k, w_kv_v, w_kv_f, b_kv_f, w_fox_q, w_fox_o,
              w_ffn_gate, w_ffn_up, w_ffn_down):
    Bp, Lp, _ = x_prompt.shape
    Ls = x_sample.shape[1]
    past = cache_k.shape[1]
    pos_p = jnp.arange(Lp)
    pos_s = past + jnp.arange(Ls)
    hp, hs = x_prompt, x_sample
    S_p_list, S_s_list = [], []
    for layer in range(DEPTH):
        if layer < N_A:
            S0p = jnp.zeros((Bp, RET_HEADS, RET_DK, RET_DV), F32)
            yp, Sp = _retention(_rmsnorm(hp, norm_mix[layer]), S0p, pos_p, w_ret_in[layer], w_ret_o[layer])
            ys, Ss = _retention(_rmsnorm(hs, norm_mix[layer]), state_ret[layer], pos_s, w_ret_in[layer], w_ret_o[layer])
            S_p_list.append(Sp)
            S_s_list.append(Ss)
        else:
            if layer == N_A:
                k_p, v_p, lf_p = _shared_kv(hp, norm_kv, w_kv_k, w_kv_v, w_kv_f, b_kv_f)
                k_s, v_s, lf_s = _shared_kv(hs, norm_kv, w_kv_k, w_kv_v, w_kv_f, b_kv_f)
                F_p = jnp.cumsum(lf_p, axis=1)
                k_all = jnp.concatenate([cache_k.astype(k_s.dtype), k_s], axis=1)
                v_all = jnp.concatenate([cache_v.astype(v_s.dtype), v_s], axis=1)
                F_s = jnp.cumsum(jnp.concatenate([cache_logf.astype(F32), lf_s], axis=1), axis=1)
            j = layer - N_A
            yp = _fox(_rmsnorm(hp, norm_mix[layer]), k_p, v_p, F_p, 0, w_fox_q[j], w_fox_o[j])
            ys = _fox(_rmsnorm(hs, norm_mix[layer]), k_all, v_all, F_s, past, w_fox_q[j], w_fox_o[j])
        hp = hp + yp
        hs = hs + ys
        hp = hp + _swiglu(_rmsnorm(hp, norm_ffn[layer]), w_ffn_gate[layer], w_ffn_up[layer], w_ffn_down[layer])
        hs = hs + _swiglu(_rmsnorm(hs, norm_ffn[layer]), w_ffn_gate[layer], w_ffn_up[layer], w_ffn_down[layer])
    y_prompt = _rmsnorm(hp, norm_final)
    y_sample = _rmsnorm(hs, norm_final)
    ret_state_prompt = jnp.stack(S_p_list, axis=0)
    ret_state_sample = jnp.stack(S_s_list, axis=0)
    return (y_prompt, y_sample, ret_state_prompt, k_p, v_p, lf_p, ret_state_sample, k_s, v_s, lf_s)
```

```python
import functools

import jax
import jax.numpy as jnp
from jax import lax
from jax.experimental import pallas as pl
from jax.experimental.pallas import tpu as pltpu

F32 = jnp.float32
BF16 = jnp.bfloat16

D_MODEL = 2048
RET_HEADS = 8
RET_DK = D_MODEL // RET_HEADS
RET_DV = 2 * D_MODEL // RET_HEADS
RET_QK = RET_HEADS * RET_DK
RET_V = RET_HEADS * RET_DV
ROPE_BASE = 10000.0
FOX_HEADS = 16
FOX_HD = D_MODEL // FOX_HEADS
FOX_KV_HEADS = 4
FOX_GROUP = FOX_HEADS // FOX_KV_HEADS
FOX_KV = FOX_KV_HEADS * FOX_HD
NORM_EPS = 1e-6
GN_EPS = 1e-5

LANES = 128
VMEM_LIMIT_BYTES = 56 * 1024 * 1024

ROW_TILE = 1024
COL_TILE = 1024
FFN_ROW_TILE = 512
FFN_COL_TILE = 512
RET_CHUNK = 256
FOX_Q_TILE = 256
FOX_K_TILE = 512
CUMSUM_TILE = 512
MASK_VALUE = -1e30


def _params(*semantics):
    return pltpu.CompilerParams(dimension_semantics=semantics, vmem_limit_bytes=VMEM_LIMIT_BYTES)


def _rmsnorm(x, gain):
    y = x * lax.rsqrt(jnp.mean(x * x, axis=-1, keepdims=True) + NORM_EPS)
    return y * gain


def _silu(x):
    return x * (1.0 / (1.0 + jnp.exp(-x)))


def _ret_in_kernel(x_ref, gain_ref, cos_ref, sin_ref, w_ref, o_ref, xn_ref, *, n_q_tiles, n_rope_tiles):
    j = pl.program_id(1)

    @pl.when(j == 0)
    def _():
        xn_ref[...] = _rmsnorm(x_ref[...], gain_ref[...]).astype(BF16)

    y = jnp.dot(xn_ref[...], w_ref[...], preferred_element_type=F32)

    @pl.when(j < n_rope_tiles)
    def _():
        cos = cos_ref[...]
        sin = sin_ref[...]
        scale = jnp.where(j >= n_q_tiles, RET_DK ** -0.5, 1.0).astype(F32)
        half = RET_DK // 2
        for h in range(y.shape[1] // RET_DK):
            x1 = y[:, h * RET_DK:h * RET_DK + half]
            x2 = y[:, h * RET_DK + half:(h + 1) * RET_DK]
            o_ref[:, h * RET_DK:h * RET_DK + half] = ((x1 * cos - x2 * sin) * scale).astype(BF16)
            o_ref[:, h * RET_DK + half:(h + 1) * RET_DK] = ((x1 * sin + x2 * cos) * scale).astype(BF16)

    @pl.when(j >= n_rope_tiles)
    def _():
        o_ref[...] = y.astype(BF16)


def _ret_in(x, gain, cos, sin, w, *, tm):
    m, d = x.shape
    n = w.shape[1]
    tn = COL_TILE
    pos_tiles = cos.shape[0] // tm
    kernel = functools.partial(_ret_in_kernel, n_q_tiles=RET_QK // tn, n_rope_tiles=2 * RET_QK // tn)
    return pl.pallas_call(
        kernel,
        grid=(m // tm, n // tn),
        in_specs=[
            pl.BlockSpec((tm, d), lambda i, j: (i, 0)),
            pl.BlockSpec((1, d), lambda i, j: (0, 0)),
            pl.BlockSpec((tm, RET_DK // 2), lambda i, j: (i % pos_tiles, 0)),
            pl.BlockSpec((tm, RET_DK // 2), lambda i, j: (i % pos_tiles, 0)),
            pl.BlockSpec((d, tn), lambda i, j: (0, j)),
        ],
        out_specs=pl.BlockSpec((tm, tn), lambda i, j: (i, j)),
        out_shape=jax.ShapeDtypeStruct((m, n), BF16),
        scratch_shapes=[pltpu.VMEM((tm, d), BF16)],
        compiler_params=_params("parallel", "arbitrary"),
        name="ret_in_proj",
    )(x, gain, cos, sin, w)


def _retention_kernel(*refs, chunk, has_init):
    if has_init:
        q_ref, k_ref, v_ref, g_ref, s0_ref, o_ref, s_ref, dm_ref, rd_ref, kw_ref = refs
    else:
        q_ref, k_ref, v_ref, g_ref, o_ref, s_ref, dm_ref, rd_ref, kw_ref = refs
        s0_ref = None
    head = pl.program_id(1)
    c = pl.program_id(2)
    lg = jnp.log(1.0 - jnp.exp2(-5.0 - jnp.full((1, 1), head, jnp.int32).astype(F32)))

    @pl.when(c == 0)
    def _():
        li = lax.broadcasted_iota(jnp.int32, (chunk, chunk), 0)
        mi = lax.broadcasted_iota(jnp.int32, (chunk, chunk), 1)
        diff = (li - mi).astype(F32)
        dm_ref[...] = jnp.where(diff >= 0, jnp.exp(lg * jnp.maximum(diff, 0.0)), 0.0)
        idx = lax.broadcasted_iota(jnp.int32, (chunk, 1), 0).astype(F32)
        rd_ref[...] = jnp.exp(lg * (idx + 1.0))
        kw_ref[...] = jnp.exp(lg * (chunk - 1.0 - idx))
        if has_init:
            s_ref[...] = s0_ref[...]
        else:
            s_ref[...] = jnp.zeros_like(s_ref)

    q = q_ref[...]
    k = k_ref[...]
    v = v_ref[...]
    s_old = s_ref[0, 0]
    scores = lax.dot_general(q, k, (((1,), (1,)), ((), ())), preferred_element_type=F32) * dm_ref[...]
    inner = jnp.dot(scores.astype(BF16), v, preferred_element_type=F32)
    cross = jnp.dot(q, s_old.astype(BF16), preferred_element_type=F32) * rd_ref[...]
    kd = (k.astype(F32) * kw_ref[...]).astype(BF16)
    s_ref[0, 0] = jnp.exp(lg * float(chunk)) * s_old + lax.dot_general(
        kd, v, (((0,), (0,)), ((), ())), preferred_element_type=F32)

    o = inner + cross
    mu = jnp.mean(o, axis=-1, keepdims=True)
    cen = o - mu
    var = jnp.mean(cen * cen, axis=-1, keepdims=True)
    on = cen * lax.rsqrt(var + GN_EPS)
    o_ref[...] = (on * _silu(g_ref[...].astype(F32))).astype(BF16)


def _retention(proj, s0, *, batch, seq, chunk):
    nc = seq // chunk
    has_init = s0 is not None
    k_off = RET_QK // RET_DK
    v_off = 2 * RET_QK // RET_DV
    g_off = (2 * RET_QK + RET_V) // RET_DV
    in_specs = [
        pl.BlockSpec((chunk, RET_DK), lambda b, h, c: (b * nc + c, h)),
        pl.BlockSpec((chunk, RET_DK), lambda b, h, c: (b * nc + c, k_off + h)),
        pl.BlockSpec((chunk, RET_DV), lambda b, h, c: (b * nc + c, v_off + h)),
        pl.BlockSpec((chunk, RET_DV), lambda b, h, c: (b * nc + c, g_off + h)),
    ]
    args = [proj, proj, proj, proj]
    if has_init:
        in_specs.append(pl.BlockSpec((1, 1, RET_DK, RET_DV), lambda b, h, c: (b, h, 0, 0)))
        args.append(s0)
    return pl.pallas_call(
        functools.partial(_retention_kernel, chunk=chunk, has_init=has_init),
        grid=(batch, RET_HEADS, nc),
        in_specs=in_specs,
        out_specs=[
            pl.BlockSpec((chunk, RET_DV), lambda b, h, c: (b * nc + c, h)),
            pl.BlockSpec((1, 1, RET_DK, RET_DV), lambda b, h, c: (b, h, 0, 0)),
        ],
        out_shape=[
            jax.ShapeDtypeStruct((batch * seq, RET_V), BF16),
            jax.ShapeDtypeStruct((batch, RET_HEADS, RET_DK, RET_DV), F32),
        ],
        scratch_shapes=[
            pltpu.VMEM((chunk, chunk), F32),
            pltpu.VMEM((chunk, 1), F32),
            pltpu.VMEM((chunk, 1), F32),
        ],
        compiler_params=_params("parallel", "parallel", "arbitrary"),
        name="retention",
    )(*args)


def _mm_res_kernel(a_ref, w_ref, r_ref, o_ref):
    o_ref[...] = r_ref[...] + jnp.dot(a_ref[...], w_ref[...], preferred_element_type=F32)


def _mm_res(a, w, res, *, tm, tn):
    m, kdim = a.shape
    n = w.shape[1]
    return pl.pallas_call(
        _mm_res_kernel,
        grid=(m // tm, n // tn),
        in_specs=[
            pl.BlockSpec((tm, kdim), lambda i, j: (i, 0)),
            pl.BlockSpec((kdim, tn), lambda i, j: (0, j)),
            pl.BlockSpec((tm, tn), lambda i, j: (i, j)),
        ],
        out_specs=pl.BlockSpec((tm, tn), lambda i, j: (i, j)),
        out_shape=jax.ShapeDtypeStruct((m, n), F32),
        compiler_params=_params("parallel", "arbitrary"),
        name="matmul_residual",
    )(a, w, res)


def _ffn_kernel(*refs, final_norm):
    if final_norm:
        x_ref, gain_ref, wg_ref, wu_ref, wd_ref, fgain_ref, o_ref, xn_ref = refs
    else:
        x_ref, gain_ref, wg_ref, wu_ref, wd_ref, o_ref, xn_ref = refs
    j = pl.program_id(1)

    @pl.when(j == 0)
    def _():
        x = x_ref[...]
        xn_ref[...] = _rmsnorm(x, gain_ref[...]).astype(BF16)
        o_ref[...] = x

    xn = xn_ref[...]
    gate = jnp.dot(xn, wg_ref[...], preferred_element_type=F32)
    up = jnp.dot(xn, wu_ref[...], preferred_element_type=F32)
    hidden = (_silu(gate) * up).astype(BF16)
    o_ref[...] += jnp.dot(hidden, wd_ref[...], preferred_element_type=F32)

    if final_norm:
        @pl.when(j == pl.num_programs(1) - 1)
        def _():
            o_ref[...] = _rmsnorm(o_ref[...], fgain_ref[...])


def _ffn(x, gain, wg, wu, wd, final_gain=None, *, tm):
    m, d = x.shape
    f = wg.shape[1]
    tf = FFN_COL_TILE
    final_norm = final_gain is not None
    in_specs = [
        pl.BlockSpec((tm, d), lambda i, j: (i, 0)),
        pl.BlockSpec((1, d), lambda i, j: (0, 0)),
        pl.BlockSpec((d, tf), lambda i, j: (0, j)),
        pl.BlockSpec((d, tf), lambda i, j: (0, j)),
        pl.BlockSpec((tf, d), lambda i, j: (j, 0)),
    ]
    args = [x, gain, wg, wu, wd]
    if final_norm:
        in_specs.append(pl.BlockSpec((1, d), lambda i, j: (0, 0)))
        args.append(final_gain)
    return pl.pallas_call(
        functools.partial(_ffn_kernel, final_norm=final_norm),
        grid=(m // tm, f // tf),
        in_specs=in_specs,
        out_specs=pl.BlockSpec((tm, d), lambda i, j: (i, 0)),
        out_shape=jax.ShapeDtypeStruct((m, d), F32),
        scratch_shapes=[pltpu.VMEM((tm, d), BF16)],
        compiler_params=_params("parallel", "arbitrary"),
        name="ffn",
    )(*args)


def _kv_kernel(x_ref, gain_ref, w_ref, bias_ref, k32_ref, v32_ref, k16_ref, v16_ref, lf_ref):
    xn = _rmsnorm(x_ref[...], gain_ref[...]).astype(BF16)
    y = jnp.dot(xn, w_ref[...], preferred_element_type=F32)
    k = y[:, :FOX_KV]
    v = y[:, FOX_KV:2 * FOX_KV]
    k32_ref[...] = k
    v32_ref[...] = v
    k16_ref[...] = k.astype(BF16)
    v16_ref[...] = v.astype(BF16)
    z = y[:, 2 * FOX_KV:] + bias_ref[...]
    lf_ref[...] = jnp.minimum(z, 0.0) - jnp.log1p(jnp.exp(-jnp.abs(z)))


def _kv_proj(x, gain, w, bias, *, tm):
    m, d = x.shape
    n = w.shape[1]
    return pl.pallas_call(
        _kv_kernel,
        grid=(m // tm,),
        in_specs=[
            pl.BlockSpec((tm, d), lambda i: (i, 0)),
            pl.BlockSpec((1, d), lambda i: (0, 0)),
            pl.BlockSpec((d, n), lambda i: (0, 0)),
            pl.BlockSpec((1, LANES), lambda i: (0, 0)),
        ],
        out_specs=[
            pl.BlockSpec((tm, FOX_KV), lambda i: (i, 0)),
            pl.BlockSpec((tm, FOX_KV), lambda i: (i, 0)),
            pl.BlockSpec((tm, FOX_KV), lambda i: (i, 0)),
            pl.BlockSpec((tm, FOX_KV), lambda i: (i, 0)),
            pl.BlockSpec((tm, LANES), lambda i: (i, 0)),
        ],
        out_shape=[
            jax.ShapeDtypeStruct((m, FOX_KV), F32),
            jax.ShapeDtypeStruct((m, FOX_KV), F32),
            jax.ShapeDtypeStruct((m, FOX_KV), BF16),
            jax.ShapeDtypeStruct((m, FOX_KV), BF16),
            jax.ShapeDtypeStruct((m, LANES), F32),
        ],
        compiler_params=_params("parallel"),
        name="kv_proj",
    )(x, gain, w, bias)


def _q_kernel(x_ref, gain_ref, w_ref, o_ref, xn_ref):
    @pl.when(pl.program_id(1) == 0)
    def _():
        xn_ref[...] = _rmsnorm(x_ref[...], gain_ref[...]).astype(BF16)

    y = jnp.dot(xn_ref[...], w_ref[...], preferred_element_type=F32)
    o_ref[...] = (y * (FOX_HD ** -0.5)).astype(BF16)


def _q_proj(x, gain, w, *, tm):
    m, d = x.shape
    n = w.shape[1]
    tn = COL_TILE
    return pl.pallas_call(
        _q_kernel,
        grid=(m // tm, n // tn),
        in_specs=[
            pl.BlockSpec((tm, d), lambda i, j: (i, 0)),
            pl.BlockSpec((1, d), lambda i, j: (0, 0)),
            pl.BlockSpec((d, tn), lambda i, j: (0, j)),
        ],
        out_specs=pl.BlockSpec((tm, tn), lambda i, j: (i, j)),
        out_shape=jax.ShapeDtypeStruct((m, n), BF16),
        scratch_shapes=[pltpu.VMEM((tm, d), BF16)],
        compiler_params=_params("parallel", "arbitrary"),
        name="q_proj",
    )(x, gain, w)


def _cumsum_kernel(x_ref, o_ref, carry_ref):
    @pl.when(pl.program_id(1) == 0)
    def _():
        carry_ref[...] = jnp.zeros_like(carry_ref)

    x = x_ref[0]
    tl = x.shape[1]
    r = lax.broadcasted_iota(jnp.int32, (tl, tl), 0)
    c = lax.broadcasted_iota(jnp.int32, (tl, tl), 1)
    tri = jnp.where(r <= c, 1.0, 0.0).astype(F32)
    y = jnp.dot(x, tri, precision=lax.Precision.HIGHEST, preferred_element_type=F32) + carry_ref[...]
    o_ref[0] = y
    carry_ref[...] = y[:, tl - 1:tl]


def _cumsum_lanes(x, *, tl):
    b, h, length = x.shape
    return pl.pallas_call(
        _cumsum_kernel,
        grid=(b, length // tl),
        in_specs=[pl.BlockSpec((1, h, tl), lambda i, t: (i, 0, t))],
        out_specs=pl.BlockSpec((1, h, tl), lambda i, t: (i, 0, t)),
        out_shape=jax.ShapeDtypeStruct((b, h, length), F32),
        scratch_shapes=[pltpu.VMEM((h, 1), F32)],
        compiler_params=_params("parallel", "arbitrary"),
        name="logf_cumsum",
    )(x)


def _fox_kernel(q_ref, k_ref, v_ref, fk_ref, o_ref, m_ref, l_ref, acc_ref, *, tq, tk, q_start):
    qb = pl.program_id(2)
    q = q_ref[0]
    qs = jnp.concatenate([q[:, g * FOX_HD:(g + 1) * FOX_HD] for g in range(FOX_GROUP)], axis=0)
    q_first = q_start + qb * tq
    n_full = q_first // tk

    m_ref[...] = jnp.full_like(m_ref, MASK_VALUE)
    l_ref[...] = jnp.zeros_like(l_ref)
    acc_ref[...] = jnp.zeros_like(acc_ref)

    def attend(kb, mask):
        start = pl.multiple_of(kb * tk, tk)
        k = k_ref[0, pl.ds(start, tk), :]
        v = v_ref[0, pl.ds(start, tk), :]
        s = lax.dot_general(qs, k, (((1,), (1,)), ((), ())), preferred_element_type=F32)
        fk = fk_ref[0, 0, kb]
        parts = []
        for g in range(FOX_GROUP):
            zg = s[g * tq:(g + 1) * tq, :] - fk[g:g + 1, :]
            if mask is not None:
                zg = jnp.where(mask, zg, MASK_VALUE)
            parts.append(zg)
        z = jnp.concatenate(parts, axis=0)
        m_old = m_ref[...]
        m_new = jnp.maximum(m_old, jnp.max(z, axis=-1, keepdims=True))
        alpha = jnp.exp(m_old - m_new)
        p = jnp.exp(z - m_new)
        l_ref[...] = alpha * l_ref[...] + jnp.sum(p, axis=-1, keepdims=True)
        acc_ref[...] = alpha * acc_ref[...] + jnp.dot(p.astype(BF16), v, preferred_element_type=F32)
        m_ref[...] = m_new

    def body(kb, carry):
        attend(kb, None)
        return carry

    lax.fori_loop(0, n_full, body, 0)

    qpos = q_first + lax.broadcasted_iota(jnp.int32, (tq, tk), 0)
    kpos = n_full * tk + lax.broadcasted_iota(jnp.int32, (tq, tk), 1)
    attend(n_full, kpos <= qpos)

    out = acc_ref[...] / l_ref[...]
    for g in range(FOX_GROUP):
        o_ref[0, :, g * FOX_HD:(g + 1) * FOX_HD] = out[g * tq:(g + 1) * tq, :].astype(BF16)


def _fox_attention(q, k, v, fk, *, tq, tk, q_start):
    b, lq, _ = q.shape
    lk = k.shape[1]
    nk = lk // tk
    rows = FOX_GROUP * tq
    return pl.pallas_call(
        functools.partial(_fox_kernel, tq=tq, tk=tk, q_start=q_start),
        grid=(b, FOX_KV_HEADS, lq // tq),
        in_specs=[
            pl.BlockSpec((1, tq, FOX_GROUP * FOX_HD), lambda i, h, t: (i, t, h)),
            pl.BlockSpec((1, lk, FOX_HD), lambda i, h, t: (i, 0, h)),
            pl.BlockSpec((1, lk, FOX_HD), lambda i, h, t: (i, 0, h)),
            pl.BlockSpec((1, 1, nk, FOX_GROUP, tk), lambda i, h, t: (i, h, 0, 0, 0)),
        ],
        out_specs=pl.BlockSpec((1, tq, FOX_GROUP * FOX_HD), lambda i, h, t: (i, t, h)),
        out_shape=jax.ShapeDtypeStruct(q.shape, BF16),
        scratch_shapes=[
            pltpu.VMEM((rows, 1), F32),
            pltpu.VMEM((rows, 1), F32),
            pltpu.VMEM((rows, FOX_HD), F32),
        ],
        compiler_params=_params("parallel", "parallel", "arbitrary"),
        name="fox_attention",
    )(q, k, v, fk)


def _rope_tables(pos):
    half = RET_DK // 2
    inv = ROPE_BASE ** (-jnp.arange(half, dtype=F32) / half)
    ang = pos.astype(F32)[:, None] * inv[None, :]
    return jnp.cos(ang), jnp.sin(ang)


def _fk_blocks(f_t, tk):
    b, _, lk = f_t.shape
    return f_t.reshape(b, FOX_KV_HEADS, FOX_GROUP, lk // tk, tk).transpose(0, 1, 3, 2, 4)


def _stream(x, *, pos, s0, cache, weights, tm, ffn_tm, chunk, tq):
    (norm_mix, norm_ffn, norm_kv, norm_final, w_ret_in, w_ret_o, w_kv, b_kv, w_fox_q, w_fox_o,
     w_ffn_gate, w_ffn_up, w_ffn_down) = weights
    b, length, d = x.shape
    m = b * length
    h = x.reshape(m, d)

    cos, sin = _rope_tables(pos)
    if cos.shape[0] < tm:
        cos = jnp.tile(cos, (tm // cos.shape[0], 1))
        sin = jnp.tile(sin, (tm // sin.shape[0], 1))
    proj = _ret_in(h, norm_mix[0:1], cos, sin, w_ret_in, tm=tm)
    o, s_fin = _retention(proj, s0, batch=b, seq=length, chunk=chunk)
    h = _mm_res(o, w_ret_o, h, tm=tm, tn=COL_TILE // 2)
    h = _ffn(h, norm_ffn[0:1], w_ffn_gate[0], w_ffn_up[0], w_ffn_down[0], tm=ffn_tm)

    k32, v32, k16, v16, lf_pad = _kv_proj(h, norm_kv[None, :], w_kv, b_kv, tm=tm)
    lf = lf_pad[:, :FOX_HEADS].reshape(b, length, FOX_HEADS)
    k16 = k16.reshape(b, length, FOX_KV)
    v16 = v16.reshape(b, length, FOX_KV)
    if cache is None:
        q_start = 0
        k_all, v_all, lf_all = k16, v16, lf
    else:
        cache_k, cache_v, cache_logf = cache
        q_start = cache_k.shape[1]
        k_all = jnp.concatenate([cache_k.reshape(b, q_start, FOX_KV).astype(BF16), k16], axis=1)
        v_all = jnp.concatenate([cache_v.reshape(b, q_start, FOX_KV).astype(BF16), v16], axis=1)
        lf_all = jnp.concatenate([cache_logf.astype(F32), lf], axis=1)
    lk = k_all.shape[1]
    lk_pad = -(-lk // FOX_K_TILE) * FOX_K_TILE
    if lk_pad != lk:
        k_all = jnp.pad(k_all, ((0, 0), (0, lk_pad - lk), (0, 0)))
        v_all = jnp.pad(v_all, ((0, 0), (0, lk_pad - lk), (0, 0)))
        lf_all = jnp.pad(lf_all, ((0, 0), (0, lk_pad - lk), (0, 0)))
    f_t = _cumsum_lanes(lf_all.transpose(0, 2, 1), tl=CUMSUM_TILE)
    fk = _fk_blocks(f_t, FOX_K_TILE)

    q = _q_proj(h, norm_mix[1:2], w_fox_q, tm=tm).reshape(b, length, d)
    att = _fox_attention(q, k_all, v_all, fk, tq=tq, tk=FOX_K_TILE, q_start=q_start)
    h = _mm_res(att.reshape(m, d), w_fox_o, h, tm=tm, tn=COL_TILE // 2)
    y = _ffn(h, norm_ffn[1:2], w_ffn_gate[1], w_ffn_up[1], w_ffn_down[1], norm_final[None, :], tm=ffn_tm)

    return (y.reshape(b, length, d), s_fin[None],
            k32.reshape(b, length, FOX_KV_HEADS, FOX_HD), v32.reshape(b, length, FOX_KV_HEADS, FOX_HD), lf)


def kernel(x_prompt, x_sample, state_ret, cache_k, cache_v, cache_logf, norm_mix, norm_ffn, norm_kv, norm_final, w_ret_in, w_ret_o, w_kv_k, w_kv_v, w_kv_f, b_kv_f, w_fox_q, w_fox_o, w_ffn_gate, w_ffn_up, w_ffn_down):
    lp = x_prompt.shape[1]
    bs, ls, _ = x_sample.shape
    past = cache_k.shape[1]

    w_kv = jnp.concatenate(
        [w_kv_k, w_kv_v, jnp.pad(w_kv_f, ((0, 0), (0, LANES - FOX_HEADS)))], axis=1).astype(BF16)
    b_kv = jnp.pad(b_kv_f.astype(F32), (0, LANES - FOX_HEADS))[None, :]
    weights = (norm_mix.astype(F32), norm_ffn.astype(F32), norm_kv.astype(F32), norm_final.astype(F32),
               w_ret_in[0].astype(BF16), w_ret_o[0].astype(BF16), w_kv, b_kv,
               w_fox_q[0].astype(BF16), w_fox_o[0].astype(BF16),
               w_ffn_gate.astype(BF16), w_ffn_up.astype(BF16), w_ffn_down.astype(BF16))

    y_p, s_p, k_p, v_p, lf_p = _stream(
        x_prompt, pos=jnp.arange(lp), s0=None, cache=None, weights=weights,
        tm=ROW_TILE, ffn_tm=FFN_ROW_TILE, chunk=RET_CHUNK, tq=FOX_Q_TILE)
    y_s, s_s, k_s, v_s, lf_s = _stream(
        x_sample, pos=past + jnp.arange(ls), s0=state_ret[0], cache=(cache_k, cache_v, cache_logf),
        weights=weights, tm=bs * ls, ffn_tm=bs * ls, chunk=ls, tq=ls)
    return (y_p, y_s, s_p, k_p, v_p, lf_p, s_s, k_s, v_s, lf_s)
```

```python
import functools

import jax
import jax.numpy as jnp
from jax import lax
from jax.experimental import pallas as pl
from jax.experimental.pallas import tpu as pltpu

F32 = jnp.float32
BF16 = jnp.bfloat16

D_MODEL = 2048
RET_HEADS = 8
RET_DK = D_MODEL // RET_HEADS
RET_DV = 2 * D_MODEL // RET_HEADS
RET_QK = RET_HEADS * RET_DK
RET_V = RET_HEADS * RET_DV
ROPE_BASE = 10000.0
FOX_HEADS = 16
FOX_HD = D_MODEL // FOX_HEADS
FOX_KV_HEADS = 4
FOX_GROUP = FOX_HEADS // FOX_KV_HEADS
FOX_KV = FOX_KV_HEADS * FOX_HD
FOX_F_PIECES = 3
FOX_V_ROWS = FOX_HD + 16
LOG2_E = 1.4426950408889634
NORM_EPS = 1e-6
GN_EPS = 1e-5

LANES = 128
VMEM_LIMIT_BYTES = 56 * 1024 * 1024

ROW_TILE = 1024
COL_TILE = 1024
FFN_ROW_TILE = 512
FFN_COL_TILE = 512
RET_CHUNK = 256
FOX_Q_TILE = 256
FOX_K_TILE = 512
FOX_COL_CHUNK = 512
CUMSUM_TILE = 512
MASK_VALUE = -1e30


def _params(*semantics):
    return pltpu.CompilerParams(dimension_semantics=semantics, vmem_limit_bytes=VMEM_LIMIT_BYTES)


def _rmsnorm(x, gain):
    y = x * lax.rsqrt(jnp.mean(x * x, axis=-1, keepdims=True) + NORM_EPS)
    return y * gain


def _silu(x):
    return x * (1.0 / (1.0 + jnp.exp(-x)))


def _ret_in_kernel(x_ref, gain_ref, cos_ref, sin_ref, w_ref, o_ref, xn_ref, *, n_q_tiles, n_rope_tiles):
    j = pl.program_id(1)

    @pl.when(j == 0)
    def _():
        xn_ref[...] = _rmsnorm(x_ref[...], gain_ref[...]).astype(BF16)

    y = jnp.dot(xn_ref[...], w_ref[...], preferred_element_type=F32)

    @pl.when(j < n_rope_tiles)
    def _():
        cos = cos_ref[...]
        sin = sin_ref[...]
        scale = jnp.where(j >= n_q_tiles, RET_DK ** -0.5, 1.0).astype(F32)
        half = RET_DK // 2
        for h in range(y.shape[1] // RET_DK):
            x1 = y[:, h * RET_DK:h * RET_DK + half]
            x2 = y[:, h * RET_DK + half:(h + 1) * RET_DK]
            o_ref[:, h * RET_DK:h * RET_DK + half] = ((x1 * cos - x2 * sin) * scale).astype(BF16)
            o_ref[:, h * RET_DK + half:(h + 1) * RET_DK] = ((x1 * sin + x2 * cos) * scale).astype(BF16)

    @pl.when(j >= n_rope_tiles)
    def _():
        o_ref[...] = y.astype(BF16)


def _ret_in(x, gain, cos, sin, w, *, tm):
    m, d = x.shape
    n = w.shape[1]
    tn = COL_TILE
    pos_tiles = cos.shape[0] // tm
    kernel = functools.partial(_ret_in_kernel, n_q_tiles=RET_QK // tn, n_rope_tiles=2 * RET_QK // tn)
    return pl.pallas_call(
        kernel,
        grid=(m // tm, n // tn),
        in_specs=[
            pl.BlockSpec((tm, d), lambda i, j: (i, 0)),
            pl.BlockSpec((1, d), lambda i, j: (0, 0)),
            pl.BlockSpec((tm, RET_DK // 2), lambda i, j: (i % pos_tiles, 0)),
            pl.BlockSpec((tm, RET_DK // 2), lambda i, j: (i % pos_tiles, 0)),
            pl.BlockSpec((d, tn), lambda i, j: (0, j)),
        ],
        out_specs=pl.BlockSpec((tm, tn), lambda i, j: (i, j)),
        out_shape=jax.ShapeDtypeStruct((m, n), BF16),
        scratch_shapes=[pltpu.VMEM((tm, d), BF16)],
        compiler_params=_params("parallel", "arbitrary"),
        name="ret_in_proj",
    )(x, gain, cos, sin, w)


def _retention_kernel(*refs, chunk, has_init):
    if has_init:
        q_ref, k_ref, v_ref, g_ref, s0_ref, o_ref, s_ref, dm_ref, rd_ref, kw_ref = refs
    else:
        q_ref, k_ref, v_ref, g_ref, o_ref, s_ref, dm_ref, rd_ref, kw_ref = refs
        s0_ref = None
    head = pl.program_id(1)
    c = pl.program_id(2)
    lg = jnp.log(1.0 - jnp.exp2(-5.0 - jnp.full((1, 1), head, jnp.int32).astype(F32)))

    @pl.when(c == 0)
    def _():
        li = lax.broadcasted_iota(jnp.int32, (chunk, chunk), 0)
        mi = lax.broadcasted_iota(jnp.int32, (chunk, chunk), 1)
        diff = (li - mi).astype(F32)
        dm_ref[...] = jnp.where(diff >= 0, jnp.exp(lg * jnp.maximum(diff, 0.0)), 0.0)
        idx = lax.broadcasted_iota(jnp.int32, (chunk, 1), 0).astype(F32)
        rd_ref[...] = jnp.exp(lg * (idx + 1.0))
        kw_ref[...] = jnp.exp(lg * (chunk - 1.0 - idx))
        if has_init:
            s_ref[...] = s0_ref[...]
        else:
            s_ref[...] = jnp.zeros_like(s_ref)

    q = q_ref[...]
    k = k_ref[...]
    v = v_ref[...]
    s_old = s_ref[0, 0]
    scores = lax.dot_general(q, k, (((1,), (1,)), ((), ())), preferred_element_type=F32) * dm_ref[...]
    inner = jnp.dot(scores.astype(BF16), v, preferred_element_type=F32)
    cross = jnp.dot(q, s_old.astype(BF16), preferred_element_type=F32) * rd_ref[...]
    kd = (k.astype(F32) * kw_ref[...]).astype(BF16)
    s_ref[0, 0] = jnp.exp(lg * float(chunk)) * s_old + lax.dot_general(
        kd, v, (((0,), (0,)), ((), ())), preferred_element_type=F32)

    o = inner + cross
    mu = jnp.mean(o, axis=-1, keepdims=True)
    cen = o - mu
    var = jnp.mean(cen * cen, axis=-1, keepdims=True)
    on = cen * lax.rsqrt(var + GN_EPS)
    o_ref[...] = (on * _silu(g_ref[...].astype(F32))).astype(BF16)


def _retention(proj, s0, *, batch, seq, chunk):
    nc = seq // chunk
    has_init = s0 is not None
    k_off = RET_QK // RET_DK
    v_off = 2 * RET_QK // RET_DV
    g_off = (2 * RET_QK + RET_V) // RET_DV
    in_specs = [
        pl.BlockSpec((chunk, RET_DK), lambda b, h, c: (b * nc + c, h)),
        pl.BlockSpec((chunk, RET_DK), lambda b, h, c: (b * nc + c, k_off + h)),
        pl.BlockSpec((chunk, RET_DV), lambda b, h, c: (b * nc + c, v_off + h)),
        pl.BlockSpec((chunk, RET_DV), lambda b, h, c: (b * nc + c, g_off + h)),
    ]
    args = [proj, proj, proj, proj]
    if has_init:
        in_specs.append(pl.BlockSpec((1, 1, RET_DK, RET_DV), lambda b, h, c: (b, h, 0, 0)))
        args.append(s0)
    return pl.pallas_call(
        functools.partial(_retention_kernel, chunk=chunk, has_init=has_init),
        grid=(batch, RET_HEADS, nc),
        in_specs=in_specs,
        out_specs=[
            pl.BlockSpec((chunk, RET_DV), lambda b, h, c: (b * nc + c, h)),
            pl.BlockSpec((1, 1, RET_DK, RET_DV), lambda b, h, c: (b, h, 0, 0)),
        ],
        out_shape=[
            jax.ShapeDtypeStruct((batch * seq, RET_V), BF16),
            jax.ShapeDtypeStruct((batch, RET_HEADS, RET_DK, RET_DV), F32),
        ],
        scratch_shapes=[
            pltpu.VMEM((chunk, chunk), F32),
            pltpu.VMEM((chunk, 1), F32),
            pltpu.VMEM((chunk, 1), F32),
        ],
        compiler_params=_params("parallel", "parallel", "arbitrary"),
        name="retention",
    )(*args)


def _mm_res_kernel(a_ref, w_ref, r_ref, o_ref):
    o_ref[...] = r_ref[...] + jnp.dot(a_ref[...], w_ref[...], preferred_element_type=F32)


def _mm_res(a, w, res, *, tm, tn):
    m, kdim = a.shape
    n = w.shape[1]
    return pl.pallas_call(
        _mm_res_kernel,
        grid=(m // tm, n // tn),
        in_specs=[
            pl.BlockSpec((tm, kdim), lambda i, j: (i, 0)),
            pl.BlockSpec((kdim, tn), lambda i, j: (0, j)),
            pl.BlockSpec((tm, tn), lambda i, j: (i, j)),
        ],
        out_specs=pl.BlockSpec((tm, tn), lambda i, j: (i, j)),
        out_shape=jax.ShapeDtypeStruct((m, n), F32),
        compiler_params=_params("parallel", "arbitrary"),
        name="matmul_residual",
    )(a, w, res)


def _ffn_kernel(*refs, final_norm):
    if final_norm:
        x_ref, gain_ref, wg_ref, wu_ref, wd_ref, fgain_ref, o_ref, xn_ref = refs
    else:
        x_ref, gain_ref, wg_ref, wu_ref, wd_ref, o_ref, xn_ref = refs
    j = pl.program_id(1)

    @pl.when(j == 0)
    def _():
        x = x_ref[...]
        xn_ref[...] = _rmsnorm(x, gain_ref[...]).astype(BF16)
        o_ref[...] = x

    xn = xn_ref[...]
    gate = jnp.dot(xn, wg_ref[...], preferred_element_type=F32)
    up = jnp.dot(xn, wu_ref[...], preferred_element_type=F32)
    hidden = (_silu(gate) * up).astype(BF16)
    o_ref[...] += jnp.dot(hidden, wd_ref[...], preferred_element_type=F32)

    if final_norm:
        @pl.when(j == pl.num_programs(1) - 1)
        def _():
            o_ref[...] = _rmsnorm(o_ref[...], fgain_ref[...])


def _ffn(x, gain, wg, wu, wd, final_gain=None, *, tm):
    m, d = x.shape
    f = wg.shape[1]
    tf = FFN_COL_TILE
    final_norm = final_gain is not None
    in_specs = [
        pl.BlockSpec((tm, d), lambda i, j: (i, 0)),
        pl.BlockSpec((1, d), lambda i, j: (0, 0)),
        pl.BlockSpec((d, tf), lambda i, j: (0, j)),
        pl.BlockSpec((d, tf), lambda i, j: (0, j)),
        pl.BlockSpec((tf, d), lambda i, j: (j, 0)),
    ]
    args = [x, gain, wg, wu, wd]
    if final_norm:
        in_specs.append(pl.BlockSpec((1, d), lambda i, j: (0, 0)))
        args.append(final_gain)
    return pl.pallas_call(
        functools.partial(_ffn_kernel, final_norm=final_norm),
        grid=(m // tm, f // tf),
        in_specs=in_specs,
        out_specs=pl.BlockSpec((tm, d), lambda i, j: (i, 0)),
        out_shape=jax.ShapeDtypeStruct((m, d), F32),
        scratch_shapes=[pltpu.VMEM((tm, d), BF16)],
        compiler_params=_params("parallel", "arbitrary"),
        name="ffn",
    )(*args)


def _kv_kernel(x_ref, gain_ref, w_ref, bias_ref, k32_ref, v32_ref, k16_ref, v16_ref, lf_ref):
    xn = _rmsnorm(x_ref[...], gain_ref[...]).astype(BF16)
    y = jnp.dot(xn, w_ref[...], preferred_element_type=F32)
    k = y[:, :FOX_KV]
    v = y[:, FOX_KV:2 * FOX_KV]
    k32_ref[...] = k
    v32_ref[...] = v
    k16_ref[...] = k.astype(BF16)
    v16_ref[...] = v.astype(BF16)
    z = y[:, 2 * FOX_KV:] + bias_ref[...]
    lf_ref[...] = jnp.minimum(z, 0.0) - jnp.log1p(jnp.exp(-jnp.abs(z)))


def _kv_proj(x, gain, w, bias, *, tm):
    m, d = x.shape
    n = w.shape[1]
    return pl.pallas_call(
        _kv_kernel,
        grid=(m // tm,),
        in_specs=[
            pl.BlockSpec((tm, d), lambda i: (i, 0)),
            pl.BlockSpec((1, d), lambda i: (0, 0)),
            pl.BlockSpec((d, n), lambda i: (0, 0)),
            pl.BlockSpec((1, LANES), lambda i: (0, 0)),
        ],
        out_specs=[
            pl.BlockSpec((tm, FOX_KV), lambda i: (i, 0)),
            pl.BlockSpec((tm, FOX_KV), lambda i: (i, 0)),
            pl.BlockSpec((tm, FOX_KV), lambda i: (i, 0)),
            pl.BlockSpec((tm, FOX_KV), lambda i: (i, 0)),
            pl.BlockSpec((tm, LANES), lambda i: (i, 0)),
        ],
        out_shape=[
            jax.ShapeDtypeStruct((m, FOX_KV), F32),
            jax.ShapeDtypeStruct((m, FOX_KV), F32),
            jax.ShapeDtypeStruct((m, FOX_KV), BF16),
            jax.ShapeDtypeStruct((m, FOX_KV), BF16),
            jax.ShapeDtypeStruct((m, LANES), F32),
        ],
        compiler_params=_params("parallel"),
        name="kv_proj",
    )(x, gain, w, bias)


def _q_kernel(x_ref, gain_ref, w_ref, o_ref, xn_ref):
    @pl.when(pl.program_id(1) == 0)
    def _():
        xn_ref[...] = _rmsnorm(x_ref[...], gain_ref[...]).astype(BF16)

    y = jnp.dot(xn_ref[...], w_ref[...], preferred_element_type=F32)
    o_ref[...] = (y * (FOX_HD ** -0.5 * LOG2_E)).astype(BF16)


def _q_proj(x, gain, w, *, tm):
    m, d = x.shape
    n = w.shape[1]
    tn = COL_TILE
    return pl.pallas_call(
        _q_kernel,
        grid=(m // tm, n // tn),
        in_specs=[
            pl.BlockSpec((tm, d), lambda i, j: (i, 0)),
            pl.BlockSpec((1, d), lambda i, j: (0, 0)),
            pl.BlockSpec((d, tn), lambda i, j: (0, j)),
        ],
        out_specs=pl.BlockSpec((tm, tn), lambda i, j: (i, j)),
        out_shape=jax.ShapeDtypeStruct((m, n), BF16),
        scratch_shapes=[pltpu.VMEM((tm, d), BF16)],
        compiler_params=_params("parallel", "arbitrary"),
        name="q_proj",
    )(x, gain, w)


def _cumsum_kernel(x_ref, hi_ref, mid_ref, lo_ref, carry_ref):
    @pl.when(pl.program_id(1) == 0)
    def _():
        carry_ref[...] = jnp.zeros_like(carry_ref)

    x = x_ref[0]
    tl = x.shape[1]
    r = lax.broadcasted_iota(jnp.int32, (tl, tl), 0)
    c = lax.broadcasted_iota(jnp.int32, (tl, tl), 1)
    tri = jnp.where(r <= c, 1.0, 0.0).astype(F32)
    y = jnp.dot(x, tri, precision=lax.Precision.HIGHEST, preferred_element_type=F32) + carry_ref[...]
    carry_ref[...] = y[:, tl - 1:tl]
    y2 = y * LOG2_E
    hi = y2.astype(BF16)
    rest = y2 - hi.astype(F32)
    mid = rest.astype(BF16)
    hi_ref[0] = hi
    mid_ref[0] = mid
    lo_ref[0] = (rest - mid.astype(F32)).astype(BF16)


def _cumsum_lanes(x, *, tl):
    b, h, length = x.shape
    piece = jax.ShapeDtypeStruct((b, h, length), BF16)
    return pl.pallas_call(
        _cumsum_kernel,
        grid=(b, length // tl),
        in_specs=[pl.BlockSpec((1, h, tl), lambda i, t: (i, 0, t))],
        out_specs=[pl.BlockSpec((1, h, tl), lambda i, t: (i, 0, t))] * 3,
        out_shape=[piece] * 3,
        scratch_shapes=[pltpu.VMEM((h, 1), F32)],
        compiler_params=_params("parallel", "arbitrary"),
        name="logf_cumsum",
    )(x)


def _fox_kernel(q_ref, k_ref, v_ref, o_ref, qa_ref, z0_ref, z1_ref, mb0_ref, mb1_ref, m_ref, acc_ref,
                *, tq, tk, q_start):
    qb = pl.program_id(2)
    rows = FOX_GROUP * tq
    q = q_ref[0]
    lane = lax.broadcasted_iota(jnp.int32, (tq, FOX_HD), 1)
    for g in range(FOX_GROUP):
        pick = jnp.where(lane >= FOX_F_PIECES * g, jnp.where(lane < FOX_F_PIECES * (g + 1), -1.0, 0.0), 0.0)
        qa_ref[g * tq:(g + 1) * tq, :FOX_HD] = q[:, g * FOX_HD:(g + 1) * FOX_HD]
        qa_ref[g * tq:(g + 1) * tq, FOX_HD:] = pick.astype(BF16)
    q_first = q_start + qb * tq
    n_full = q_first // tk

    m_ref[...] = jnp.full_like(m_ref, MASK_VALUE)
    acc_ref[...] = jnp.zeros_like(acc_ref)

    cw = min(rows, FOX_COL_CHUNK)
    chunks = [slice(c * cw, (c + 1) * cw) for c in range(rows // cw)]

    def score(kb, masked, z_ref, mb_ref):
        start = pl.multiple_of(kb * tk, tk)
        ka = k_ref[0, 0, pl.ds(start, tk), :]
        for c, cs in enumerate(chunks):
            z = lax.dot_general(ka, qa_ref[cs, :], (((1,), (1,)), ((), ())), preferred_element_type=F32)
            if masked:
                kpos = kb * tk + lax.broadcasted_iota(jnp.int32, (tk, cw), 0)
                qpos = q_first + ((c * cw + lax.broadcasted_iota(jnp.int32, (tk, cw), 1)) & (tq - 1))
                z = jnp.where(kpos <= qpos, z, MASK_VALUE)
            z_ref[:, cs] = z
            mb_ref[:, cs] = jnp.max(z, axis=0, keepdims=True)

    def accumulate(kb, z_ref, mb_ref):
        vt = v_ref[0, 0, kb]
        for cs in chunks:
            m_old = m_ref[:, cs]
            m_new = jnp.maximum(m_old, mb_ref[:, cs])
            alpha = jnp.exp2(m_old - m_new)
            p = jnp.exp2(z_ref[:, cs] - m_new)
            acc_ref[:, cs] = alpha * acc_ref[:, cs] + jnp.dot(vt, p.astype(BF16), preferred_element_type=F32)
            m_ref[:, cs] = m_new

    score(n_full, True, z0_ref, mb0_ref)

    def pair(i, pending):
        score(2 * i, False, z1_ref, mb1_ref)
        accumulate(pending, z0_ref, mb0_ref)
        score(2 * i + 1, False, z0_ref, mb0_ref)
        accumulate(2 * i, z1_ref, mb1_ref)
        return 2 * i + 1

    pending = lax.fori_loop(0, n_full // 2, pair, n_full)

    @pl.when(n_full % 2 == 1)
    def _():
        score(n_full - 1, False, z1_ref, mb1_ref)
        accumulate(pending, z0_ref, mb0_ref)
        accumulate(n_full - 1, z1_ref, mb1_ref)

    @pl.when(n_full % 2 == 0)
    def _():
        accumulate(pending, z0_ref, mb0_ref)

    acc = acc_ref[...]
    out = (acc[:FOX_HD] / acc[FOX_HD:FOX_HD + 1]).T
    for g in range(FOX_GROUP):
        o_ref[0, :, g * FOX_HD:(g + 1) * FOX_HD] = out[g * tq:(g + 1) * tq, :].astype(BF16)


def _fox_attention(q, k_aug, v_t, *, tq, tk, q_start):
    b, lq, _ = q.shape
    lk = k_aug.shape[2]
    nk = lk // tk
    rows = FOX_GROUP * tq
    assert tq & (tq - 1) == 0 and tk % tq == 0 and q_start % tk == 0
    return pl.pallas_call(
        functools.partial(_fox_kernel, tq=tq, tk=tk, q_start=q_start),
        grid=(b, FOX_KV_HEADS, lq // tq),
        in_specs=[
            pl.BlockSpec((1, tq, FOX_GROUP * FOX_HD), lambda i, h, t: (i, t, h)),
            pl.BlockSpec((1, 1, lk, 2 * FOX_HD), lambda i, h, t: (i, h, 0, 0)),
            pl.BlockSpec((1, 1, nk, FOX_V_ROWS, tk), lambda i, h, t: (i, h, 0, 0, 0)),
        ],
        out_specs=pl.BlockSpec((1, tq, FOX_GROUP * FOX_HD), lambda i, h, t: (i, t, h)),
        out_shape=jax.ShapeDtypeStruct(q.shape, BF16),
        scratch_shapes=[
            pltpu.VMEM((rows, 2 * FOX_HD), BF16),
            pltpu.VMEM((tk, rows), F32),
            pltpu.VMEM((tk, rows), F32),
            pltpu.VMEM((1, rows), F32),
            pltpu.VMEM((1, rows), F32),
            pltpu.VMEM((1, rows), F32),
            pltpu.VMEM((FOX_V_ROWS, rows), F32),
        ],
        compiler_params=_params("parallel", "parallel", "arbitrary"),
        name="fox_attention",
    )(q, k_aug, v_t)


def _rope_tables(pos):
    half = RET_DK // 2
    inv = ROPE_BASE ** (-jnp.arange(half, dtype=F32) / half)
    ang = pos.astype(F32)[:, None] * inv[None, :]
    return jnp.cos(ang), jnp.sin(ang)


def _augment_keys(k, f_pieces):
    b, lk, _ = k.shape
    k = k.reshape(b, lk, FOX_KV_HEADS, FOX_HD).transpose(0, 2, 1, 3)
    f = jnp.stack(f_pieces, axis=2).reshape(b, FOX_KV_HEADS, FOX_GROUP * FOX_F_PIECES, lk).transpose(0, 1, 3, 2)
    pad = jnp.zeros((b, FOX_KV_HEADS, lk, FOX_HD - FOX_GROUP * FOX_F_PIECES), BF16)
    return jnp.concatenate([k, f, pad], axis=-1)


def _transpose_values(v, tk):
    b, lk, _ = v.shape
    v_t = v.reshape(b, lk // tk, tk, FOX_KV_HEADS, FOX_HD).transpose(0, 3, 1, 4, 2)
    ones = jnp.ones((b, FOX_KV_HEADS, lk // tk, FOX_V_ROWS - FOX_HD, tk), BF16)
    return jnp.concatenate([v_t, ones], axis=3)


def _stream(x, *, pos, s0, cache, weights, tm, ffn_tm, chunk, tq):
    (norm_mix, norm_ffn, norm_kv, norm_final, w_ret_in, w_ret_o, w_kv, b_kv, w_fox_q, w_fox_o,
     w_ffn_gate, w_ffn_up, w_ffn_down) = weights
    b, length, d = x.shape
    m = b * length
    h = x.reshape(m, d)

    cos, sin = _rope_tables(pos)
    if cos.shape[0] < tm:
        cos = jnp.tile(cos, (tm // cos.shape[0], 1))
        sin = jnp.tile(sin, (tm // sin.shape[0], 1))
    proj = _ret_in(h, norm_mix[0:1], cos, sin, w_ret_in, tm=tm)
    o, s_fin = _retention(proj, s0, batch=b, seq=length, chunk=chunk)
    h = _mm_res(o, w_ret_o, h, tm=tm, tn=COL_TILE // 2)
    h = _ffn(h, norm_ffn[0:1], w_ffn_gate[0], w_ffn_up[0], w_ffn_down[0], tm=ffn_tm)

    k32, v32, k16, v16, lf_pad = _kv_proj(h, norm_kv[None, :], w_kv, b_kv, tm=tm)
    lf = lf_pad[:, :FOX_HEADS].reshape(b, length, FOX_HEADS)
    k16 = k16.reshape(b, length, FOX_KV)
    v16 = v16.reshape(b, length, FOX_KV)
    if cache is None:
        q_start = 0
        k_all, v_all, lf_all = k16, v16, lf
    else:
        cache_k, cache_v, cache_logf = cache
        q_start = cache_k.shape[1]
        k_all = jnp.concatenate([cache_k.reshape(b, q_start, FOX_KV).astype(BF16), k16], axis=1)
        v_all = jnp.concatenate([cache_v.reshape(b, q_start, FOX_KV).astype(BF16), v16], axis=1)
        lf_all = jnp.concatenate([cache_logf.astype(F32), lf], axis=1)
    lk = k_all.shape[1]
    lk_pad = -(-lk // FOX_K_TILE) * FOX_K_TILE
    if lk_pad != lk:
        k_all = jnp.pad(k_all, ((0, 0), (0, lk_pad - lk), (0, 0)))
        v_all = jnp.pad(v_all, ((0, 0), (0, lk_pad - lk), (0, 0)))
        lf_all = jnp.pad(lf_all, ((0, 0), (0, lk_pad - lk), (0, 0)))
    f_pieces = _cumsum_lanes(lf_all.transpose(0, 2, 1), tl=CUMSUM_TILE)
    k_aug = _augment_keys(k_all, f_pieces)
    v_t = _transpose_values(v_all, FOX_K_TILE)

    q = _q_proj(h, norm_mix[1:2], w_fox_q, tm=tm).reshape(b, length, d)
    att = _fox_attention(q, k_aug, v_t, tq=tq, tk=FOX_K_TILE, q_start=q_start)
    h = _mm_res(att.reshape(m, d), w_fox_o, h, tm=tm, tn=COL_TILE // 2)
    y = _ffn(h, norm_ffn[1:2], w_ffn_gate[1], w_ffn_up[1], w_ffn_down[1], norm_final[None, :], tm=ffn_tm)

    return (y.reshape(b, length, d), s_fin[None],
            k32.reshape(b, length, FOX_KV_HEADS, FOX_HD), v32.reshape(b, length, FOX_KV_HEADS, FOX_HD), lf)


def kernel(x_prompt, x_sample, state_ret, cache_k, cache_v, cache_logf, norm_mix, norm_ffn, norm_kv, norm_final, w_ret_in, w_ret_o, w_kv_k, w_kv_v, w_kv_f, b_kv_f, w_fox_q, w_fox_o, w_ffn_gate, w_ffn_up, w_ffn_down):
    lp = x_prompt.shape[1]
    bs, ls, _ = x_sample.shape
    past = cache_k.shape[1]

    w_kv = jnp.concatenate(
        [w_kv_k, w_kv_v, jnp.pad(w_kv_f, ((0, 0), (0, LANES - FOX_HEADS)))], axis=1).astype(BF16)
    b_kv = jnp.pad(b_kv_f.astype(F32), (0, LANES - FOX_HEADS))[None, :]
    weights = (norm_mix.astype(F32), norm_ffn.astype(F32), norm_kv.astype(F32), norm_final.astype(F32),
               w_ret_in[0].astype(BF16), w_ret_o[0].astype(BF16), w_kv, b_kv,
               w_fox_q[0].astype(BF16), w_fox_o[0].astype(BF16),
               w_ffn_gate.astype(BF16), w_ffn_up.astype(BF16), w_ffn_down.astype(BF16))

    y_p, s_p, k_p, v_p, lf_p = _stream(
        x_prompt, pos=jnp.arange(lp), s0=None, cache=None, weights=weights,
        tm=ROW_TILE, ffn_tm=FFN_ROW_TILE, chunk=RET_CHUNK, tq=FOX_Q_TILE)
    y_s, s_s, k_s, v_s, lf_s = _stream(
        x_sample, pos=past + jnp.arange(ls), s0=state_ret[0], cache=(cache_k, cache_v, cache_logf),
        weights=weights, tm=bs * ls, ffn_tm=bs * ls, chunk=ls, tq=ls)
    return (y_p, y_s, s_p, k_p, v_p, lf_p, s_s, k_s, v_s, lf_s)
```

```python
import functools

import jax
import jax.numpy as jnp
from jax import lax
from jax.experimental import pallas as pl
from jax.experimental.pallas import tpu as pltpu

F32 = jnp.float32
BF16 = jnp.bfloat16

D_MODEL = 2048
RET_HEADS = 8
RET_DK = D_MODEL // RET_HEADS
RET_DV = 2 * D_MODEL // RET_HEADS
RET_QK = RET_HEADS * RET_DK
RET_V = RET_HEADS * RET_DV
ROPE_BASE = 10000.0
FOX_HEADS = 16
FOX_HD = D_MODEL // FOX_HEADS
FOX_KV_HEADS = 4
FOX_GROUP = FOX_HEADS // FOX_KV_HEADS
FOX_KV = FOX_KV_HEADS * FOX_HD
FOX_F_PIECES = 3
FOX_V_ROWS = FOX_HD + 16
LOG2_E = 1.4426950408889634
NORM_EPS = 1e-6
GN_EPS = 1e-5

LANES = 128
VMEM_LIMIT_BYTES = 56 * 1024 * 1024

ROW_TILE = 1024
COL_TILE = 1024
FFN_ROW_TILE = 512
FFN_COL_TILE = 512
RET_CHUNK = 256
RET_HEADS_PER_STEP = 4
FOX_Q_TILE = 256
FOX_K_TILE = 512
FOX_COL_CHUNK = 512
CUMSUM_TILE = 512
MASK_VALUE = -1e30


def _params(*semantics):
    return pltpu.CompilerParams(dimension_semantics=semantics, vmem_limit_bytes=VMEM_LIMIT_BYTES)


def _rmsnorm(x, gain):
    y = x * lax.rsqrt(jnp.mean(x * x, axis=-1, keepdims=True) + NORM_EPS)
    return y * gain


def _silu(x):
    return x * (1.0 / (1.0 + jnp.exp(-x)))


def _ret_in_kernel(x_ref, gain_ref, cos_ref, sin_ref, w_ref, o_ref, xn_ref, *, n_q_tiles, n_rope_tiles, n_gate_first):
    j = pl.program_id(1)

    @pl.when(j == 0)
    def _():
        xn_ref[...] = _rmsnorm(x_ref[...], gain_ref[...]).astype(BF16)

    y = jnp.dot(xn_ref[...], w_ref[...], preferred_element_type=F32)

    @pl.when(j < n_rope_tiles)
    def _():
        cos = cos_ref[...]
        sin = sin_ref[...]
        scale = jnp.where(j >= n_q_tiles, RET_DK ** -0.5, 1.0).astype(F32)
        half = RET_DK // 2
        for h in range(y.shape[1] // RET_DK):
            x1 = y[:, h * RET_DK:h * RET_DK + half]
            x2 = y[:, h * RET_DK + half:(h + 1) * RET_DK]
            o_ref[:, h * RET_DK:h * RET_DK + half] = ((x1 * cos - x2 * sin) * scale).astype(BF16)
            o_ref[:, h * RET_DK + half:(h + 1) * RET_DK] = ((x1 * sin + x2 * cos) * scale).astype(BF16)

    @pl.when(jnp.logical_and(j >= n_rope_tiles, j < n_gate_first))
    def _():
        o_ref[...] = y.astype(BF16)

    @pl.when(j >= n_gate_first)
    def _():
        o_ref[...] = _silu(y).astype(BF16)


def _ret_in(x, gain, cos, sin, w, *, tm):
    m, d = x.shape
    n = w.shape[1]
    tn = COL_TILE
    pos_tiles = cos.shape[0] // tm
    kernel = functools.partial(_ret_in_kernel, n_q_tiles=RET_QK // tn, n_rope_tiles=2 * RET_QK // tn,
                               n_gate_first=(2 * RET_QK + RET_V) // tn)
    return pl.pallas_call(
        kernel,
        grid=(m // tm, n // tn),
        in_specs=[
            pl.BlockSpec((tm, d), lambda i, j: (i, 0)),
            pl.BlockSpec((1, d), lambda i, j: (0, 0)),
            pl.BlockSpec((tm, RET_DK // 2), lambda i, j: (i % pos_tiles, 0)),
            pl.BlockSpec((tm, RET_DK // 2), lambda i, j: (i % pos_tiles, 0)),
            pl.BlockSpec((d, tn), lambda i, j: (0, j)),
        ],
        out_specs=pl.BlockSpec((tm, tn), lambda i, j: (i, j)),
        out_shape=jax.ShapeDtypeStruct((m, n), BF16),
        scratch_shapes=[pltpu.VMEM((tm, d), BF16)],
        compiler_params=_params("parallel", "arbitrary"),
        name="ret_in_proj",
    )(x, gain, cos, sin, w)


def _retention_kernel(*refs, chunk, has_init):
    if has_init:
        q_ref, k_ref, v_ref, g_ref, s0_ref, o_ref, s_ref, dm_ref, rd_ref, kw_ref, raw_ref = refs
    else:
        q_ref, k_ref, v_ref, g_ref, o_ref, s_ref, dm_ref, rd_ref, kw_ref, raw_ref = refs
        s0_ref = None
    c = pl.program_id(2)
    heads = RET_HEADS_PER_STEP

    def log_gamma(hh):
        head = pl.program_id(1) * heads + hh
        return jnp.log(1.0 - jnp.exp2(-5.0 - jnp.full((1, 1), head, jnp.int32).astype(F32)))

    @pl.when(c == 0)
    def _():
        li = lax.broadcasted_iota(jnp.int32, (chunk, chunk), 0)
        mi = lax.broadcasted_iota(jnp.int32, (chunk, chunk), 1)
        diff = (li - mi).astype(F32)
        idx = lax.broadcasted_iota(jnp.int32, (chunk, 1), 0).astype(F32)
        for hh in range(heads):
            lg = log_gamma(hh)
            dm_ref[hh] = jnp.where(diff >= 0, jnp.exp(lg * jnp.maximum(diff, 0.0)), 0.0)
            rd_ref[hh] = jnp.exp(lg * (idx + 1.0))
            kw_ref[hh] = jnp.exp(lg * (chunk - 1.0 - idx))
        if has_init:
            s_ref[...] = s0_ref[...]
        else:
            s_ref[...] = jnp.zeros_like(s_ref)

    for hh in range(heads):
        q = q_ref[:, hh * RET_DK:(hh + 1) * RET_DK]
        k = k_ref[:, hh * RET_DK:(hh + 1) * RET_DK]
        v = v_ref[:, hh * RET_DV:(hh + 1) * RET_DV]
        s_old = s_ref[0, hh]
        scores = lax.dot_general(q, k, (((1,), (1,)), ((), ())), preferred_element_type=F32) * dm_ref[hh]
        inner = jnp.dot(scores.astype(BF16), v, preferred_element_type=F32)
        cross = jnp.dot(q, s_old.astype(BF16), preferred_element_type=F32) * rd_ref[hh]
        raw_ref[hh] = inner + cross
        kd = (k.astype(F32) * kw_ref[hh]).astype(BF16)
        s_ref[0, hh] = jnp.exp(log_gamma(hh) * float(chunk)) * s_old + lax.dot_general(
            kd, v, (((0,), (0,)), ((), ())), preferred_element_type=F32)

    for hh in range(heads):
        o = raw_ref[hh]
        mu = jnp.mean(o, axis=-1, keepdims=True)
        cen = o - mu
        var = jnp.mean(cen * cen, axis=-1, keepdims=True)
        on = cen * lax.rsqrt(var + GN_EPS)
        gate = g_ref[:, hh * RET_DV:(hh + 1) * RET_DV].astype(F32)
        o_ref[:, hh * RET_DV:(hh + 1) * RET_DV] = (on * gate).astype(BF16)


def _retention(proj, s0, *, batch, seq, chunk):
    nc = seq // chunk
    has_init = s0 is not None
    heads = RET_HEADS_PER_STEP
    qk_w = heads * RET_DK
    v_w = heads * RET_DV
    k_off = RET_QK // qk_w
    v_off = 2 * RET_QK // v_w
    g_off = (2 * RET_QK + RET_V) // v_w
    in_specs = [
        pl.BlockSpec((chunk, qk_w), lambda b, h, c: (b * nc + c, h)),
        pl.BlockSpec((chunk, qk_w), lambda b, h, c: (b * nc + c, k_off + h)),
        pl.BlockSpec((chunk, v_w), lambda b, h, c: (b * nc + c, v_off + h)),
        pl.BlockSpec((chunk, v_w), lambda b, h, c: (b * nc + c, g_off + h)),
    ]
    args = [proj, proj, proj, proj]
    if has_init:
        in_specs.append(pl.BlockSpec((1, heads, RET_DK, RET_DV), lambda b, h, c: (b, h, 0, 0)))
        args.append(s0)
    return pl.pallas_call(
        functools.partial(_retention_kernel, chunk=chunk, has_init=has_init),
        grid=(batch, RET_HEADS // heads, nc),
        in_specs=in_specs,
        out_specs=[
            pl.BlockSpec((chunk, v_w), lambda b, h, c: (b * nc + c, h)),
            pl.BlockSpec((1, heads, RET_DK, RET_DV), lambda b, h, c: (b, h, 0, 0)),
        ],
        out_shape=[
            jax.ShapeDtypeStruct((batch * seq, RET_V), BF16),
            jax.ShapeDtypeStruct((batch, RET_HEADS, RET_DK, RET_DV), F32),
        ],
        scratch_shapes=[
            pltpu.VMEM((heads, chunk, chunk), F32),
            pltpu.VMEM((heads, chunk, 1), F32),
            pltpu.VMEM((heads, chunk, 1), F32),
            pltpu.VMEM((heads, chunk, RET_DV), F32),
        ],
        compiler_params=_params("parallel", "parallel", "arbitrary"),
        name="retention",
    )(*args)


def _mm_res_kernel(a_ref, w_ref, r_ref, o_ref):
    o_ref[...] = r_ref[...] + jnp.dot(a_ref[...], w_ref[...], preferred_element_type=F32)


def _mm_res(a, w, res, *, tm, tn):
    m, kdim = a.shape
    n = w.shape[1]
    return pl.pallas_call(
        _mm_res_kernel,
        grid=(m // tm, n // tn),
        in_specs=[
            pl.BlockSpec((tm, kdim), lambda i, j: (i, 0)),
            pl.BlockSpec((kdim, tn), lambda i, j: (0, j)),
            pl.BlockSpec((tm, tn), lambda i, j: (i, j)),
        ],
        out_specs=pl.BlockSpec((tm, tn), lambda i, j: (i, j)),
        out_shape=jax.ShapeDtypeStruct((m, n), F32),
        compiler_params=_params("parallel", "arbitrary"),
        name="matmul_residual",
    )(a, w, res)


def _ffn_kernel(*refs, final_norm):
    if final_norm:
        x_ref, gain_ref, wg_ref, wu_ref, wd_ref, fgain_ref, o_ref, xn_ref = refs
    else:
        x_ref, gain_ref, wg_ref, wu_ref, wd_ref, o_ref, xn_ref = refs
    j = pl.program_id(1)

    @pl.when(j == 0)
    def _():
        x = x_ref[...]
        xn_ref[...] = _rmsnorm(x, gain_ref[...]).astype(BF16)
        o_ref[...] = x

    xn = xn_ref[...]
    gate = jnp.dot(xn, wg_ref[...], preferred_element_type=F32)
    up = jnp.dot(xn, wu_ref[...], preferred_element_type=F32)
    hidden = (_silu(gate) * up).astype(BF16)
    o_ref[...] += jnp.dot(hidden, wd_ref[...], preferred_element_type=F32)

    if final_norm:
        @pl.when(j == pl.num_programs(1) - 1)
        def _():
            o_ref[...] = _rmsnorm(o_ref[...], fgain_ref[...])


def _ffn(x, gain, wg, wu, wd, final_gain=None, *, tm):
    m, d = x.shape
    f = wg.shape[1]
    tf = FFN_COL_TILE
    final_norm = final_gain is not None
    in_specs = [
        pl.BlockSpec((tm, d), lambda i, j: (i, 0)),
        pl.BlockSpec((1, d), lambda i, j: (0, 0)),
        pl.BlockSpec((d, tf), lambda i, j: (0, j)),
        pl.BlockSpec((d, tf), lambda i, j: (0, j)),
        pl.BlockSpec((tf, d), lambda i, j: (j, 0)),
    ]
    args = [x, gain, wg, wu, wd]
    if final_norm:
        in_specs.append(pl.BlockSpec((1, d), lambda i, j: (0, 0)))
        args.append(final_gain)
    return pl.pallas_call(
        functools.partial(_ffn_kernel, final_norm=final_norm),
        grid=(m // tm, f // tf),
        in_specs=in_specs,
        out_specs=pl.BlockSpec((tm, d), lambda i, j: (i, 0)),
        out_shape=jax.ShapeDtypeStruct((m, d), F32),
        scratch_shapes=[pltpu.VMEM((tm, d), BF16)],
        compiler_params=_params("parallel", "arbitrary"),
        name="ffn",
    )(*args)


def _kv_kernel(x_ref, gain_ref, w_ref, bias_ref, k32_ref, v32_ref, k16_ref, v16_ref, lf_ref):
    xn = _rmsnorm(x_ref[...], gain_ref[...]).astype(BF16)
    y = jnp.dot(xn, w_ref[...], preferred_element_type=F32)
    k = y[:, :FOX_KV]
    v = y[:, FOX_KV:2 * FOX_KV]
    k32_ref[...] = k
    v32_ref[...] = v
    k16_ref[...] = k.astype(BF16)
    v16_ref[...] = v.astype(BF16)
    z = y[:, 2 * FOX_KV:] + bias_ref[...]
    lf_ref[...] = jnp.minimum(z, 0.0) - jnp.log1p(jnp.exp(-jnp.abs(z)))


def _kv_proj(x, gain, w, bias, *, tm):
    m, d = x.shape
    n = w.shape[1]
    return pl.pallas_call(
        _kv_kernel,
        grid=(m // tm,),
        in_specs=[
            pl.BlockSpec((tm, d), lambda i: (i, 0)),
            pl.BlockSpec((1, d), lambda i: (0, 0)),
            pl.BlockSpec((d, n), lambda i: (0, 0)),
            pl.BlockSpec((1, LANES), lambda i: (0, 0)),
        ],
        out_specs=[
            pl.BlockSpec((tm, FOX_KV), lambda i: (i, 0)),
            pl.BlockSpec((tm, FOX_KV), lambda i: (i, 0)),
            pl.BlockSpec((tm, FOX_KV), lambda i: (i, 0)),
            pl.BlockSpec((tm, FOX_KV), lambda i: (i, 0)),
            pl.BlockSpec((tm, LANES), lambda i: (i, 0)),
        ],
        out_shape=[
            jax.ShapeDtypeStruct((m, FOX_KV), F32),
            jax.ShapeDtypeStruct((m, FOX_KV), F32),
            jax.ShapeDtypeStruct((m, FOX_KV), BF16),
            jax.ShapeDtypeStruct((m, FOX_KV), BF16),
            jax.ShapeDtypeStruct((m, LANES), F32),
        ],
        compiler_params=_params("parallel"),
        name="kv_proj",
    )(x, gain, w, bias)


def _q_kernel(x_ref, gain_ref, w_ref, o_ref, xn_ref):
    @pl.when(pl.program_id(1) == 0)
    def _():
        xn_ref[...] = _rmsnorm(x_ref[...], gain_ref[...]).astype(BF16)

    y = jnp.dot(xn_ref[...], w_ref[...], preferred_element_type=F32)
    o_ref[...] = (y * (FOX_HD ** -0.5 * LOG2_E)).astype(BF16)


def _q_proj(x, gain, w, *, tm):
    m, d = x.shape
    n = w.shape[1]
    tn = COL_TILE
    return pl.pallas_call(
        _q_kernel,
        grid=(m // tm, n // tn),
        in_specs=[
            pl.BlockSpec((tm, d), lambda i, j: (i, 0)),
            pl.BlockSpec((1, d), lambda i, j: (0, 0)),
            pl.BlockSpec((d, tn), lambda i, j: (0, j)),
        ],
        out_specs=pl.BlockSpec((tm, tn), lambda i, j: (i, j)),
        out_shape=jax.ShapeDtypeStruct((m, n), BF16),
        scratch_shapes=[pltpu.VMEM((tm, d), BF16)],
        compiler_params=_params("parallel", "arbitrary"),
        name="q_proj",
    )(x, gain, w)


def _cumsum_kernel(x_ref, hi_ref, mid_ref, lo_ref, carry_ref):
    @pl.when(pl.program_id(1) == 0)
    def _():
        carry_ref[...] = jnp.zeros_like(carry_ref)

    x = x_ref[0]
    tl = x.shape[1]
    r = lax.broadcasted_iota(jnp.int32, (tl, tl), 0)
    c = lax.broadcasted_iota(jnp.int32, (tl, tl), 1)
    tri = jnp.where(r <= c, 1.0, 0.0).astype(F32)
    y = jnp.dot(x, tri, precision=lax.Precision.HIGHEST, preferred_element_type=F32) + carry_ref[...]
    carry_ref[...] = y[:, tl - 1:tl]
    y2 = y * LOG2_E
    hi = y2.astype(BF16)
    rest = y2 - hi.astype(F32)
    mid = rest.astype(BF16)
    hi_ref[0] = hi
    mid_ref[0] = mid
    lo_ref[0] = (rest - mid.astype(F32)).astype(BF16)


def _cumsum_lanes(x, *, tl):
    b, h, length = x.shape
    piece = jax.ShapeDtypeStruct((b, h, length), BF16)
    return pl.pallas_call(
        _cumsum_kernel,
        grid=(b, length // tl),
        in_specs=[pl.BlockSpec((1, h, tl), lambda i, t: (i, 0, t))],
        out_specs=[pl.BlockSpec((1, h, tl), lambda i, t: (i, 0, t))] * 3,
        out_shape=[piece] * 3,
        scratch_shapes=[pltpu.VMEM((h, 1), F32)],
        compiler_params=_params("parallel", "arbitrary"),
        name="logf_cumsum",
    )(x)


def _fox_kernel(q_ref, k_ref, v_ref, o_ref, qa_ref, z0_ref, z1_ref, mb0_ref, mb1_ref, m_ref, acc_ref,
                *, tq, tk, q_start):
    qb = pl.program_id(2)
    rows = FOX_GROUP * tq
    q = q_ref[0]
    lane = lax.broadcasted_iota(jnp.int32, (tq, FOX_HD), 1)
    for g in range(FOX_GROUP):
        pick = jnp.where(lane >= FOX_F_PIECES * g, jnp.where(lane < FOX_F_PIECES * (g + 1), -1.0, 0.0), 0.0)
        qa_ref[g * tq:(g + 1) * tq, :FOX_HD] = q[:, g * FOX_HD:(g + 1) * FOX_HD]
        qa_ref[g * tq:(g + 1) * tq, FOX_HD:] = pick.astype(BF16)
    q_first = q_start + qb * tq
    n_full = q_first // tk

    m_ref[...] = jnp.full_like(m_ref, MASK_VALUE)
    acc_ref[...] = jnp.zeros_like(acc_ref)

    cw = min(rows, FOX_COL_CHUNK)
    chunks = [slice(c * cw, (c + 1) * cw) for c in range(rows // cw)]

    def score(kb, masked, z_ref, mb_ref):
        start = pl.multiple_of(kb * tk, tk)
        ka = k_ref[0, 0, pl.ds(start, tk), :]
        for c, cs in enumerate(chunks):
            z = lax.dot_general(ka, qa_ref[cs, :], (((1,), (1,)), ((), ())), preferred_element_type=F32)
            if masked:
                kpos = kb * tk + lax.broadcasted_iota(jnp.int32, (tk, cw), 0)
                qpos = q_first + ((c * cw + lax.broadcasted_iota(jnp.int32, (tk, cw), 1)) & (tq - 1))
                z = jnp.where(kpos <= qpos, z, MASK_VALUE)
            z_ref[:, cs] = z
            mb_ref[:, cs] = jnp.max(z, axis=0, keepdims=True)

    def accumulate(kb, z_ref, mb_ref):
        vt = v_ref[0, 0, kb]
        for cs in chunks:
            m_old = m_ref[:, cs]
            m_new = jnp.maximum(m_old, mb_ref[:, cs])
            alpha = jnp.exp2(m_old - m_new)
            p = jnp.exp2(z_ref[:, cs] - m_new)
            acc_ref[:, cs] = alpha * acc_ref[:, cs] + jnp.dot(vt, p.astype(BF16), preferred_element_type=F32)
            m_ref[:, cs] = m_new

    score(n_full, True, z0_ref, mb0_ref)

    def pair(i, pending):
        score(2 * i, False, z1_ref, mb1_ref)
        accumulate(pending, z0_ref, mb0_ref)
        score(2 * i + 1, False, z0_ref, mb0_ref)
        accumulate(2 * i, z1_ref, mb1_ref)
        return 2 * i + 1

    pending = lax.fori_loop(0, n_full // 2, pair, n_full)

    @pl.when(n_full % 2 == 1)
    def _():
        score(n_full - 1, False, z1_ref, mb1_ref)
        accumulate(pending, z0_ref, mb0_ref)
        accumulate(n_full - 1, z1_ref, mb1_ref)

    @pl.when(n_full % 2 == 0)
    def _():
        accumulate(pending, z0_ref, mb0_ref)

    acc = acc_ref[...]
    out = (acc[:FOX_HD] / acc[FOX_HD:FOX_HD + 1]).T
    for g in range(FOX_GROUP):
        o_ref[0, :, g * FOX_HD:(g + 1) * FOX_HD] = out[g * tq:(g + 1) * tq, :].astype(BF16)


def _fox_attention(q, k_aug, v_t, *, tq, tk, q_start):
    b, lq, _ = q.shape
    lk = k_aug.shape[2]
    nk = lk // tk
    rows = FOX_GROUP * tq
    assert tq & (tq - 1) == 0 and tk % tq == 0 and q_start % tk == 0
    return pl.pallas_call(
        functools.partial(_fox_kernel, tq=tq, tk=tk, q_start=q_start),
        grid=(b, FOX_KV_HEADS, lq // tq),
        in_specs=[
            pl.BlockSpec((1, tq, FOX_GROUP * FOX_HD), lambda i, h, t: (i, t, h)),
            pl.BlockSpec((1, 1, lk, 2 * FOX_HD), lambda i, h, t: (i, h, 0, 0)),
            pl.BlockSpec((1, 1, nk, FOX_V_ROWS, tk), lambda i, h, t: (i, h, 0, 0, 0)),
        ],
        out_specs=pl.BlockSpec((1, tq, FOX_GROUP * FOX_HD), lambda i, h, t: (i, t, h)),
        out_shape=jax.ShapeDtypeStruct(q.shape, BF16),
        scratch_shapes=[
            pltpu.VMEM((rows, 2 * FOX_HD), BF16),
            pltpu.VMEM((tk, rows), F32),
            pltpu.VMEM((tk, rows), F32),
            pltpu.VMEM((1, rows), F32),
            pltpu.VMEM((1, rows), F32),
            pltpu.VMEM((1, rows), F32),
            pltpu.VMEM((FOX_V_ROWS, rows), F32),
        ],
        compiler_params=_params("parallel", "parallel", "arbitrary"),
        name="fox_attention",
    )(q, k_aug, v_t)


def _rope_tables(pos):
    half = RET_DK // 2
    inv = ROPE_BASE ** (-jnp.arange(half, dtype=F32) / half)
    ang = pos.astype(F32)[:, None] * inv[None, :]
    return jnp.cos(ang), jnp.sin(ang)


def _augment_keys(k, f_pieces):
    b, lk, _ = k.shape
    k = k.reshape(b, lk, FOX_KV_HEADS, FOX_HD).transpose(0, 2, 1, 3)
    f = jnp.stack(f_pieces, axis=2).reshape(b, FOX_KV_HEADS, FOX_GROUP * FOX_F_PIECES, lk).transpose(0, 1, 3, 2)
    pad = jnp.zeros((b, FOX_KV_HEADS, lk, FOX_HD - FOX_GROUP * FOX_F_PIECES), BF16)
    return jnp.concatenate([k, f, pad], axis=-1)


def _transpose_values(v, tk):
    b, lk, _ = v.shape
    v_t = v.reshape(b, lk // tk, tk, FOX_KV_HEADS, FOX_HD).transpose(0, 3, 1, 4, 2)
    ones = jnp.ones((b, FOX_KV_HEADS, lk // tk, FOX_V_ROWS - FOX_HD, tk), BF16)
    return jnp.concatenate([v_t, ones], axis=3)


def _stream(x, *, pos, s0, cache, weights, tm, ffn_tm, chunk, tq):
    (norm_mix, norm_ffn, norm_kv, norm_final, w_ret_in, w_ret_o, w_kv, b_kv, w_fox_q, w_fox_o,
     w_ffn_gate, w_ffn_up, w_ffn_down) = weights
    b, length, d = x.shape
    m = b * length
    h = x.reshape(m, d)

    cos, sin = _rope_tables(pos)
    if cos.shape[0] < tm:
        cos = jnp.tile(cos, (tm // cos.shape[0], 1))
        sin = jnp.tile(sin, (tm // sin.shape[0], 1))
    proj = _ret_in(h, norm_mix[0:1], cos, sin, w_ret_in, tm=tm)
    o, s_fin = _retention(proj, s0, batch=b, seq=length, chunk=chunk)
    h = _mm_res(o, w_ret_o, h, tm=tm, tn=COL_TILE // 2)
    h = _ffn(h, norm_ffn[0:1], w_ffn_gate[0], w_ffn_up[0], w_ffn_down[0], tm=ffn_tm)

    k32, v32, k16, v16, lf_pad = _kv_proj(h, norm_kv[None, :], w_kv, b_kv, tm=tm)
    lf = lf_pad[:, :FOX_HEADS].reshape(b, length, FOX_HEADS)
    k16 = k16.reshape(b, length, FOX_KV)
    v16 = v16.reshape(b, length, FOX_KV)
    if cache is None:
        q_start = 0
        k_all, v_all, lf_all = k16, v16, lf
    else:
        cache_k, cache_v, cache_logf = cache
        q_start = cache_k.shape[1]
        k_all = jnp.concatenate([cache_k.reshape(b, q_start, FOX_KV).astype(BF16), k16], axis=1)
        v_all = jnp.concatenate([cache_v.reshape(b, q_start, FOX_KV).astype(BF16), v16], axis=1)
        lf_all = jnp.concatenate([cache_logf.astype(F32), lf], axis=1)
    lk = k_all.shape[1]
    lk_pad = -(-lk // FOX_K_TILE) * FOX_K_TILE
    if lk_pad != lk:
        k_all = jnp.pad(k_all, ((0, 0), (0, lk_pad - lk), (0, 0)))
        v_all = jnp.pad(v_all, ((0, 0), (0, lk_pad - lk), (0, 0)))
        lf_all = jnp.pad(lf_all, ((0, 0), (0, lk_pad - lk), (0, 0)))
    f_pieces = _cumsum_lanes(lf_all.transpose(0, 2, 1), tl=CUMSUM_TILE)
    k_aug = _augment_keys(k_all, f_pieces)
    v_t = _transpose_values(v_all, FOX_K_TILE)

    q = _q_proj(h, norm_mix[1:2], w_fox_q, tm=tm).reshape(b, length, d)
    att = _fox_attention(q, k_aug, v_t, tq=tq, tk=FOX_K_TILE, q_start=q_start)
    h = _mm_res(att.reshape(m, d), w_fox_o, h, tm=tm, tn=COL_TILE // 2)
    y = _ffn(h, norm_ffn[1:2], w_ffn_gate[1], w_ffn_up[1], w_ffn_down[1], norm_final[None, :], tm=ffn_tm)

    return (y.reshape(b, length, d), s_fin[None],
            k32.reshape(b, length, FOX_KV_HEADS, FOX_HD), v32.reshape(b, length, FOX_KV_HEADS, FOX_HD), lf)


def kernel(x_prompt, x_sample, state_ret, cache_k, cache_v, cache_logf, norm_mix, norm_ffn, norm_kv, norm_final, w_ret_in, w_ret_o, w_kv_k, w_kv_v, w_kv_f, b_kv_f, w_fox_q, w_fox_o, w_ffn_gate, w_ffn_up, w_ffn_down):
    lp = x_prompt.shape[1]
    bs, ls, _ = x_sample.shape
    past = cache_k.shape[1]

    w_kv = jnp.concatenate(
        [w_kv_k, w_kv_v, jnp.pad(w_kv_f, ((0, 0), (0, LANES - FOX_HEADS)))], axis=1).astype(BF16)
    b_kv = jnp.pad(b_kv_f.astype(F32), (0, LANES - FOX_HEADS))[None, :]
    weights = (norm_mix.astype(F32), norm_ffn.astype(F32), norm_kv.astype(F32), norm_final.astype(F32),
               w_ret_in[0].astype(BF16), w_ret_o[0].astype(BF16), w_kv, b_kv,
               w_fox_q[0].astype(BF16), w_fox_o[0].astype(BF16),
               w_ffn_gate.astype(BF16), w_ffn_up.astype(BF16), w_ffn_down.astype(BF16))

    y_p, s_p, k_p, v_p, lf_p = _stream(
        x_prompt, pos=jnp.arange(lp), s0=None, cache=None, weights=weights,
        tm=ROW_TILE, ffn_tm=FFN_ROW_TILE, chunk=RET_CHUNK, tq=FOX_Q_TILE)
    y_s, s_s, k_s, v_s, lf_s = _stream(
        x_sample, pos=past + jnp.arange(ls), s0=state_ret[0], cache=(cache_k, cache_v, cache_logf),
        weights=weights, tm=bs * ls, ffn_tm=bs * ls, chunk=ls, tq=ls)
    return (y_p, y_s, s_p, k_p, v_p, lf_p, s_s, k_s, v_s, lf_s)
```

```python
import functools

import jax
import jax.numpy as jnp
from jax import lax
from jax.experimental import pallas as pl
from jax.experimental.pallas import tpu as pltpu

F32 = jnp.float32
BF16 = jnp.bfloat16

D_MODEL = 2048
RET_HEADS = 8
RET_DK = D_MODEL // RET_HEADS
RET_DV = 2 * D_MODEL // RET_HEADS
RET_QK = RET_HEADS * RET_DK
RET_V = RET_HEADS * RET_DV
ROPE_BASE = 10000.0
FOX_HEADS = 16
FOX_HD = D_MODEL // FOX_HEADS
FOX_KV_HEADS = 4
FOX_GROUP = FOX_HEADS // FOX_KV_HEADS
FOX_KV = FOX_KV_HEADS * FOX_HD
FOX_F_PIECES = 3
FOX_V_ROWS = FOX_HD + 16
LOG2_E = 1.4426950408889634
NORM_EPS = 1e-6
GN_EPS = 1e-5

LANES = 128
VMEM_LIMIT_BYTES = 56 * 1024 * 1024

ROW_TILE = 1024
COL_TILE = 1024
FFN_ROW_TILE = 512
FFN_COL_TILE = 512
RET_CHUNK = 256
RET_HEADS_PER_STEP = 4
FOX_Q_TILE = 256
FOX_K_TILE = 512
FOX_COL_CHUNK = 512
CUMSUM_TILE = 512
MASK_VALUE = -1e30


def _params(*semantics):
    return pltpu.CompilerParams(dimension_semantics=semantics, vmem_limit_bytes=VMEM_LIMIT_BYTES)


def _rmsnorm(x, gain):
    y = x * lax.rsqrt(jnp.mean(x * x, axis=-1, keepdims=True) + NORM_EPS)
    return y * gain


def _silu(x):
    return x * (1.0 / (1.0 + jnp.exp(-x)))


def _ret_in_kernel(x_ref, gain_ref, cos_ref, sin_ref, w_ref, o_ref, xn_ref, *, n_q_tiles, n_rope_tiles, n_gate_first):
    j = pl.program_id(1)

    @pl.when(j == 0)
    def _():
        xn_ref[...] = _rmsnorm(x_ref[...], gain_ref[...]).astype(BF16)

    y = jnp.dot(xn_ref[...], w_ref[...], preferred_element_type=F32)

    @pl.when(j < n_rope_tiles)
    def _():
        cos = cos_ref[...]
        sin = sin_ref[...]
        scale = jnp.where(j >= n_q_tiles, RET_DK ** -0.5, 1.0).astype(F32)
        half = RET_DK // 2
        for h in range(y.shape[1] // RET_DK):
            x1 = y[:, h * RET_DK:h * RET_DK + half]
            x2 = y[:, h * RET_DK + half:(h + 1) * RET_DK]
            o_ref[:, h * RET_DK:h * RET_DK + half] = ((x1 * cos - x2 * sin) * scale).astype(BF16)
            o_ref[:, h * RET_DK + half:(h + 1) * RET_DK] = ((x1 * sin + x2 * cos) * scale).astype(BF16)

    @pl.when(jnp.logical_and(j >= n_rope_tiles, j < n_gate_first))
    def _():
        o_ref[...] = y.astype(BF16)

    @pl.when(j >= n_gate_first)
    def _():
        o_ref[...] = _silu(y).astype(BF16)


def _ret_in(x, gain, cos, sin, w, *, tm):
    m, d = x.shape
    n = w.shape[1]
    tn = COL_TILE
    pos_tiles = cos.shape[0] // tm
    kernel = functools.partial(_ret_in_kernel, n_q_tiles=RET_QK // tn, n_rope_tiles=2 * RET_QK // tn,
                               n_gate_first=(2 * RET_QK + RET_V) // tn)
    return pl.pallas_call(
        kernel,
        grid=(m // tm, n // tn),
        in_specs=[
            pl.BlockSpec((tm, d), lambda i, j: (i, 0)),
            pl.BlockSpec((1, d), lambda i, j: (0, 0)),
            pl.BlockSpec((tm, RET_DK // 2), lambda i, j: (i % pos_tiles, 0)),
            pl.BlockSpec((tm, RET_DK // 2), lambda i, j: (i % pos_tiles, 0)),
            pl.BlockSpec((d, tn), lambda i, j: (0, j)),
        ],
        out_specs=pl.BlockSpec((tm, tn), lambda i, j: (i, j)),
        out_shape=jax.ShapeDtypeStruct((m, n), BF16),
        scratch_shapes=[pltpu.VMEM((tm, d), BF16)],
        compiler_params=_params("parallel", "arbitrary"),
        name="ret_in_proj",
    )(x, gain, cos, sin, w)


def _retention_kernel(*refs, chunk, has_init):
    if has_init:
        q_ref, k_ref, v_ref, g_ref, s0_ref, o_ref, s_ref, dm_ref, rd_ref, kw_ref, raw_ref = refs
    else:
        q_ref, k_ref, v_ref, g_ref, o_ref, s_ref, dm_ref, rd_ref, kw_ref, raw_ref = refs
        s0_ref = None
    c = pl.program_id(2)
    heads = RET_HEADS_PER_STEP

    def log_gamma(hh):
        head = pl.program_id(1) * heads + hh
        return jnp.log(1.0 - jnp.exp2(-5.0 - jnp.full((1, 1), head, jnp.int32).astype(F32)))

    @pl.when(c == 0)
    def _():
        li = lax.broadcasted_iota(jnp.int32, (chunk, chunk), 0)
        mi = lax.broadcasted_iota(jnp.int32, (chunk, chunk), 1)
        diff = (li - mi).astype(F32)
        idx = lax.broadcasted_iota(jnp.int32, (chunk, 1), 0).astype(F32)
        for hh in range(heads):
            lg = log_gamma(hh)
            dm_ref[hh] = jnp.where(diff >= 0, jnp.exp(lg * jnp.maximum(diff, 0.0)), 0.0)
            rd_ref[hh] = jnp.exp(lg * (idx + 1.0))
            kw_ref[hh] = jnp.exp(lg * (chunk - 1.0 - idx))
        if has_init:
            s_ref[...] = s0_ref[...]
        else:
            s_ref[...] = jnp.zeros_like(s_ref)

    for hh in range(heads):
        q = q_ref[:, hh * RET_DK:(hh + 1) * RET_DK]
        k = k_ref[:, hh * RET_DK:(hh + 1) * RET_DK]
        v = v_ref[:, hh * RET_DV:(hh + 1) * RET_DV]
        s_old = s_ref[0, hh]
        scores = lax.dot_general(q, k, (((1,), (1,)), ((), ())), preferred_element_type=F32) * dm_ref[hh]
        inner = jnp.dot(scores.astype(BF16), v, preferred_element_type=F32)
        cross = jnp.dot(q, s_old.astype(BF16), preferred_element_type=F32) * rd_ref[hh]
        raw_ref[hh] = inner + cross
        kd = (k.astype(F32) * kw_ref[hh]).astype(BF16)
        s_ref[0, hh] = jnp.exp(log_gamma(hh) * float(chunk)) * s_old + lax.dot_general(
            kd, v, (((0,), (0,)), ((), ())), preferred_element_type=F32)

    for hh in range(heads):
        o = raw_ref[hh]
        mu = jnp.mean(o, axis=-1, keepdims=True)
        cen = o - mu
        var = jnp.mean(cen * cen, axis=-1, keepdims=True)
        on = cen * lax.rsqrt(var + GN_EPS)
        gate = g_ref[:, hh * RET_DV:(hh + 1) * RET_DV].astype(F32)
        o_ref[:, hh * RET_DV:(hh + 1) * RET_DV] = (on * gate).astype(BF16)


def _retention(proj, s0, *, batch, seq, chunk):
    nc = seq // chunk
    has_init = s0 is not None
    heads = RET_HEADS_PER_STEP
    qk_w = heads * RET_DK
    v_w = heads * RET_DV
    k_off = RET_QK // qk_w
    v_off = 2 * RET_QK // v_w
    g_off = (2 * RET_QK + RET_V) // v_w
    in_specs = [
        pl.BlockSpec((chunk, qk_w), lambda b, h, c: (b * nc + c, h)),
        pl.BlockSpec((chunk, qk_w), lambda b, h, c: (b * nc + c, k_off + h)),
        pl.BlockSpec((chunk, v_w), lambda b, h, c: (b * nc + c, v_off + h)),
        pl.BlockSpec((chunk, v_w), lambda b, h, c: (b * nc + c, g_off + h)),
    ]
    args = [proj, proj, proj, proj]
    if has_init:
        in_specs.append(pl.BlockSpec((1, heads, RET_DK, RET_DV), lambda b, h, c: (b, h, 0, 0)))
        args.append(s0)
    return pl.pallas_call(
        functools.partial(_retention_kernel, chunk=chunk, has_init=has_init),
        grid=(batch, RET_HEADS // heads, nc),
        in_specs=in_specs,
        out_specs=[
            pl.BlockSpec((chunk, v_w), lambda b, h, c: (b * nc + c, h)),
            pl.BlockSpec((1, heads, RET_DK, RET_DV), lambda b, h, c: (b, h, 0, 0)),
        ],
        out_shape=[
            jax.ShapeDtypeStruct((batch * seq, RET_V), BF16),
            jax.ShapeDtypeStruct((batch, RET_HEADS, RET_DK, RET_DV), F32),
        ],
        scratch_shapes=[
            pltpu.VMEM((heads, chunk, chunk), F32),
            pltpu.VMEM((heads, chunk, 1), F32),
            pltpu.VMEM((heads, chunk, 1), F32),
            pltpu.VMEM((heads, chunk, RET_DV), F32),
        ],
        compiler_params=_params("parallel", "parallel", "arbitrary"),
        name="retention",
    )(*args)


def _mm_res_kernel(a_ref, w_ref, r_ref, o_ref):
    o_ref[...] = r_ref[...] + jnp.dot(a_ref[...], w_ref[...], preferred_element_type=F32)


def _mm_res(a, w, res, *, tm, tn):
    m, kdim = a.shape
    n = w.shape[1]
    return pl.pallas_call(
        _mm_res_kernel,
        grid=(m // tm, n // tn),
        in_specs=[
            pl.BlockSpec((tm, kdim), lambda i, j: (i, 0)),
            pl.BlockSpec((kdim, tn), lambda i, j: (0, j)),
            pl.BlockSpec((tm, tn), lambda i, j: (i, j)),
        ],
        out_specs=pl.BlockSpec((tm, tn), lambda i, j: (i, j)),
        out_shape=jax.ShapeDtypeStruct((m, n), F32),
        compiler_params=_params("parallel", "arbitrary"),
        name="matmul_residual",
    )(a, w, res)


def _ffn_kernel(*refs, final_norm):
    if final_norm:
        x_ref, gain_ref, wg_ref, wu_ref, wd_ref, fgain_ref, o_ref, xn_ref = refs
    else:
        x_ref, gain_ref, wg_ref, wu_ref, wd_ref, o_ref, xn_ref = refs
    j = pl.program_id(1)

    @pl.when(j == 0)
    def _():
        x = x_ref[...]
        xn_ref[...] = _rmsnorm(x, gain_ref[...]).astype(BF16)
        o_ref[...] = x

    xn = xn_ref[...]
    gate = jnp.dot(xn, wg_ref[...], preferred_element_type=F32)
    up = jnp.dot(xn, wu_ref[...], preferred_element_type=F32)
    hidden = (_silu(gate) * up).astype(BF16)
    o_ref[...] += jnp.dot(hidden, wd_ref[...], preferred_element_type=F32)

    if final_norm:
        @pl.when(j == pl.num_programs(1) - 1)
        def _():
            o_ref[...] = _rmsnorm(o_ref[...], fgain_ref[...])


def _ffn(x, gain, wg, wu, wd, final_gain=None, *, tm):
    m, d = x.shape
    f = wg.shape[1]
    tf = FFN_COL_TILE
    final_norm = final_gain is not None
    in_specs = [
        pl.BlockSpec((tm, d), lambda i, j: (i, 0)),
        pl.BlockSpec((1, d), lambda i, j: (0, 0)),
        pl.BlockSpec((d, tf), lambda i, j: (0, j)),
        pl.BlockSpec((d, tf), lambda i, j: (0, j)),
        pl.BlockSpec((tf, d), lambda i, j: (j, 0)),
    ]
    args = [x, gain, wg, wu, wd]
    if final_norm:
        in_specs.append(pl.BlockSpec((1, d), lambda i, j: (0, 0)))
        args.append(final_gain)
    return pl.pallas_call(
        functools.partial(_ffn_kernel, final_norm=final_norm),
        grid=(m // tm, f // tf),
        in_specs=in_specs,
        out_specs=pl.BlockSpec((tm, d), lambda i, j: (i, 0)),
        out_shape=jax.ShapeDtypeStruct((m, d), F32),
        scratch_shapes=[pltpu.VMEM((tm, d), BF16)],
        compiler_params=_params("parallel", "arbitrary"),
        name="ffn",
    )(*args)


def _kv_kernel(x_ref, gain_ref, w_ref, bias_ref, k32_ref, v32_ref, kh_ref, vh_ref, lf_ref, *, key_tile):
    xn = _rmsnorm(x_ref[...], gain_ref[...]).astype(BF16)
    y = jnp.dot(xn, w_ref[...], preferred_element_type=F32)
    k = y[:, :FOX_KV]
    v = y[:, FOX_KV:2 * FOX_KV]
    k32_ref[...] = k
    v32_ref[...] = v
    tm = k.shape[0]
    for h in range(FOX_KV_HEADS):
        hs = slice(h * FOX_HD, (h + 1) * FOX_HD)
        kh_ref[h] = k[:, hs].astype(BF16)
        if key_tile is None:
            vh_ref[h] = v[:, hs].astype(BF16)
        else:
            v_t = v[:, hs].T.astype(BF16)
            for t in range(tm // key_tile):
                vh_ref[h, t, :FOX_HD, :] = v_t[:, t * key_tile:(t + 1) * key_tile]
                vh_ref[h, t, FOX_HD:, :] = jnp.ones((FOX_V_ROWS - FOX_HD, key_tile), BF16)
    z = y[:, 2 * FOX_KV:] + bias_ref[...]
    lf_ref[...] = jnp.minimum(z, 0.0) - jnp.log1p(jnp.exp(-jnp.abs(z)))


def _kv_proj(x, gain, w, bias, *, tm, key_tile):
    m, d = x.shape
    n = w.shape[1]
    if key_tile is None:
        v_spec = pl.BlockSpec((FOX_KV_HEADS, tm, FOX_HD), lambda i: (0, i, 0))
        v_shape = jax.ShapeDtypeStruct((FOX_KV_HEADS, m, FOX_HD), BF16)
    else:
        v_spec = pl.BlockSpec((FOX_KV_HEADS, tm // key_tile, FOX_V_ROWS, key_tile), lambda i: (0, i, 0, 0))
        v_shape = jax.ShapeDtypeStruct((FOX_KV_HEADS, m // key_tile, FOX_V_ROWS, key_tile), BF16)
    return pl.pallas_call(
        functools.partial(_kv_kernel, key_tile=key_tile),
        grid=(m // tm,),
        in_specs=[
            pl.BlockSpec((tm, d), lambda i: (i, 0)),
            pl.BlockSpec((1, d), lambda i: (0, 0)),
            pl.BlockSpec((d, n), lambda i: (0, 0)),
            pl.BlockSpec((1, LANES), lambda i: (0, 0)),
        ],
        out_specs=[
            pl.BlockSpec((tm, FOX_KV), lambda i: (i, 0)),
            pl.BlockSpec((tm, FOX_KV), lambda i: (i, 0)),
            pl.BlockSpec((FOX_KV_HEADS, tm, FOX_HD), lambda i: (0, i, 0)),
            v_spec,
            pl.BlockSpec((tm, LANES), lambda i: (i, 0)),
        ],
        out_shape=[
            jax.ShapeDtypeStruct((m, FOX_KV), F32),
            jax.ShapeDtypeStruct((m, FOX_KV), F32),
            jax.ShapeDtypeStruct((FOX_KV_HEADS, m, FOX_HD), BF16),
            v_shape,
            jax.ShapeDtypeStruct((m, LANES), F32),
        ],
        compiler_params=_params("parallel"),
        name="kv_proj",
    )(x, gain, w, bias)


def _q_kernel(x_ref, gain_ref, w_ref, o_ref, xn_ref):
    @pl.when(pl.program_id(1) == 0)
    def _():
        xn_ref[...] = _rmsnorm(x_ref[...], gain_ref[...]).astype(BF16)

    y = jnp.dot(xn_ref[...], w_ref[...], preferred_element_type=F32)
    o_ref[...] = (y * (FOX_HD ** -0.5 * LOG2_E)).astype(BF16)


def _q_proj(x, gain, w, *, tm):
    m, d = x.shape
    n = w.shape[1]
    tn = COL_TILE
    return pl.pallas_call(
        _q_kernel,
        grid=(m // tm, n // tn),
        in_specs=[
            pl.BlockSpec((tm, d), lambda i, j: (i, 0)),
            pl.BlockSpec((1, d), lambda i, j: (0, 0)),
            pl.BlockSpec((d, tn), lambda i, j: (0, j)),
        ],
        out_specs=pl.BlockSpec((tm, tn), lambda i, j: (i, j)),
        out_shape=jax.ShapeDtypeStruct((m, n), BF16),
        scratch_shapes=[pltpu.VMEM((tm, d), BF16)],
        compiler_params=_params("parallel", "arbitrary"),
        name="q_proj",
    )(x, gain, w)


def _cumsum_kernel(x_ref, o_ref, carry_ref):
    @pl.when(pl.program_id(1) == 0)
    def _():
        carry_ref[...] = jnp.zeros_like(carry_ref)

    x = x_ref[0]
    tl = x.shape[0]
    r = lax.broadcasted_iota(jnp.int32, (tl, tl), 0)
    c = lax.broadcasted_iota(jnp.int32, (tl, tl), 1)
    tri = jnp.where(c <= r, 1.0, 0.0).astype(F32)
    y = jnp.dot(tri, x, precision=lax.Precision.HIGHEST, preferred_element_type=F32) + carry_ref[...]
    carry_ref[...] = y[tl - 1:tl, :]
    y2 = y * LOG2_E
    hi = y2.astype(BF16).astype(F32)
    rest = y2 - hi
    mid = rest.astype(BF16).astype(F32)
    lo = rest - mid
    lane = lax.broadcasted_iota(jnp.int32, y.shape, 1)
    packed = jnp.where(
        lane < FOX_HEADS, hi,
        jnp.where(lane < 2 * FOX_HEADS, pltpu.roll(mid, FOX_HEADS, 1),
                  jnp.where(lane < 3 * FOX_HEADS, pltpu.roll(lo, 2 * FOX_HEADS, 1), 0.0)))
    o_ref[0] = packed.astype(BF16)


def _cumsum_pieces(x, *, tl):
    b, length, lanes = x.shape
    return pl.pallas_call(
        _cumsum_kernel,
        grid=(b, length // tl),
        in_specs=[pl.BlockSpec((1, tl, lanes), lambda i, t: (i, t, 0))],
        out_specs=pl.BlockSpec((1, tl, lanes), lambda i, t: (i, t, 0)),
        out_shape=jax.ShapeDtypeStruct((b, length, lanes), BF16),
        scratch_shapes=[pltpu.VMEM((1, lanes), F32)],
        compiler_params=_params("parallel", "arbitrary"),
        name="logf_cumsum",
    )(x)


def _fox_kernel(q_ref, k_ref, f_ref, v_ref, o_ref, qa_ref, z0_ref, z1_ref, mb0_ref, mb1_ref, m_ref, acc_ref,
                *, tq, tk, q_start, blocks):
    step = pl.program_id(2)
    rows = FOX_GROUP * tq
    lane = lax.broadcasted_iota(jnp.int32, (tq, FOX_HD), 1)
    for x in range(blocks):
        for g in range(FOX_GROUP):
            head = pl.program_id(1) * FOX_GROUP + g
            pick = jnp.where(lane < FOX_F_PIECES * FOX_HEADS,
                             jnp.where((lane & (FOX_HEADS - 1)) == head, -1.0, 0.0), 0.0)
            qa_ref[x, g * tq:(g + 1) * tq, :FOX_HD] = q_ref[0, x * tq:(x + 1) * tq, g * FOX_HD:(g + 1) * FOX_HD]
            qa_ref[x, g * tq:(g + 1) * tq, FOX_HD:] = pick.astype(BF16)
    n_full = (q_start + step * blocks * tq) // tk

    m_ref[...] = jnp.full_like(m_ref, MASK_VALUE)
    acc_ref[...] = jnp.zeros_like(acc_ref)

    cw = min(rows, FOX_COL_CHUNK)
    chunks = [slice(c * cw, (c + 1) * cw) for c in range(rows // cw)]
    buffers = ((z0_ref, mb0_ref), (z1_ref, mb1_ref))

    def score(x, kb, masked, buf):
        z_ref, mb_ref = buffers[buf]
        start = pl.multiple_of(kb * tk, tk)
        ka = jnp.concatenate([k_ref[0, 0, pl.ds(start, tk), :], f_ref[0, pl.ds(start, tk), :]], axis=1)
        for c, cs in enumerate(chunks):
            z = lax.dot_general(ka, qa_ref[x, cs, :], (((1,), (1,)), ((), ())), preferred_element_type=F32)
            if masked:
                kpos = kb * tk + lax.broadcasted_iota(jnp.int32, (tk, cw), 0)
                col = c * cw + lax.broadcasted_iota(jnp.int32, (tk, cw), 1)
                qpos = q_start + (step * blocks + x) * tq + (col & (tq - 1))
                z = jnp.where(kpos <= qpos, z, MASK_VALUE)
            z_ref[:, cs] = z
            mb_ref[:, cs] = jnp.max(z, axis=0, keepdims=True)

    def accumulate(x, kb, buf):
        z_ref, mb_ref = buffers[buf]
        vt = v_ref[0, 0, kb]
        for cs in chunks:
            m_old = m_ref[x, :, cs]
            m_new = jnp.maximum(m_old, mb_ref[:, cs])
            alpha = jnp.exp2(m_old - m_new)
            p = jnp.exp2(z_ref[:, cs] - m_new)
            acc_ref[x, :, cs] = alpha * acc_ref[x, :, cs] + jnp.dot(vt, p.astype(BF16), preferred_element_type=F32)
            m_ref[x, :, cs] = m_new

    if blocks == 2:
        score(0, n_full, True, 0)
        score(1, n_full, True, 1)
        accumulate(0, n_full, 0)

        def tile(kb, pending):
            score(0, kb, False, 0)
            accumulate(1, pending, 1)
            score(1, kb, False, 1)
            accumulate(0, kb, 0)
            return kb

        pending = lax.fori_loop(0, n_full, tile, n_full)
        accumulate(1, pending, 1)
    else:
        score(0, n_full, True, 0)

        def pair(i, pending):
            score(0, 2 * i, False, 1)
            accumulate(0, pending, 0)
            score(0, 2 * i + 1, False, 0)
            accumulate(0, 2 * i, 1)
            return 2 * i + 1

        pending = lax.fori_loop(0, n_full // 2, pair, n_full)

        @pl.when(n_full % 2 == 1)
        def _():
            score(0, n_full - 1, False, 1)
            accumulate(0, pending, 0)
            accumulate(0, n_full - 1, 1)

        @pl.when(n_full % 2 == 0)
        def _():
            accumulate(0, pending, 0)

    for x in range(blocks):
        acc = acc_ref[x]
        out = (acc[:FOX_HD] / acc[FOX_HD:FOX_HD + 1]).T
        for g in range(FOX_GROUP):
            o_ref[0, x * tq:(x + 1) * tq, g * FOX_HD:(g + 1) * FOX_HD] = out[g * tq:(g + 1) * tq, :].astype(BF16)


def _fox_attention(q, k_heads, f_pieces, v_t, *, tq, tk, q_start):
    b, lq, _ = q.shape
    lk = k_heads.shape[2]
    nk = lk // tk
    rows = FOX_GROUP * tq
    blocks = 2 if (2 * tq == tk and lq % tk == 0) else 1
    assert tq & (tq - 1) == 0 and tk % tq == 0 and q_start % tk == 0
    return pl.pallas_call(
        functools.partial(_fox_kernel, tq=tq, tk=tk, q_start=q_start, blocks=blocks),
        grid=(b, FOX_KV_HEADS, lq // (blocks * tq)),
        in_specs=[
            pl.BlockSpec((1, blocks * tq, FOX_GROUP * FOX_HD), lambda i, h, t: (i, t, h)),
            pl.BlockSpec((1, 1, lk, FOX_HD), lambda i, h, t: (h, i, 0, 0)),
            pl.BlockSpec((1, lk, LANES), lambda i, h, t: (i, 0, 0)),
            pl.BlockSpec((1, 1, nk, FOX_V_ROWS, tk), lambda i, h, t: (h, i, 0, 0, 0)),
        ],
        out_specs=pl.BlockSpec((1, blocks * tq, FOX_GROUP * FOX_HD), lambda i, h, t: (i, t, h)),
        out_shape=jax.ShapeDtypeStruct(q.shape, BF16),
        scratch_shapes=[
            pltpu.VMEM((blocks, rows, 2 * FOX_HD), BF16),
            pltpu.VMEM((tk, rows), F32),
            pltpu.VMEM((tk, rows), F32),
            pltpu.VMEM((1, rows), F32),
            pltpu.VMEM((1, rows), F32),
            pltpu.VMEM((blocks, 1, rows), F32),
            pltpu.VMEM((blocks, FOX_V_ROWS, rows), F32),
        ],
        compiler_params=_params("parallel", "parallel", "arbitrary"),
        name="fox_attention",
    )(q, k_heads, f_pieces, v_t)


def _rope_tables(pos):
    half = RET_DK // 2
    inv = ROPE_BASE ** (-jnp.arange(half, dtype=F32) / half)
    ang = pos.astype(F32)[:, None] * inv[None, :]
    return jnp.cos(ang), jnp.sin(ang)


def _with_cache(new_heads, cache, lk_pad):
    past = cache.transpose(2, 0, 1, 3).astype(BF16)
    both = jnp.concatenate([past, new_heads], axis=2)
    return jnp.pad(both, ((0, 0), (0, 0), (0, lk_pad - both.shape[2]), (0, 0)))


def _transpose_values(v_heads, tk):
    kvh, b, lk, _ = v_heads.shape
    v_t = v_heads.reshape(kvh, b, lk // tk, tk, FOX_HD).transpose(0, 1, 2, 4, 3)
    ones = jnp.ones((kvh, b, lk // tk, FOX_V_ROWS - FOX_HD, tk), BF16)
    return jnp.concatenate([v_t, ones], axis=3)


def _stream(x, *, pos, s0, cache, weights, tm, ffn_tm, chunk, tq):
    (norm_mix, norm_ffn, norm_kv, norm_final, w_ret_in, w_ret_o, w_kv, b_kv, w_fox_q, w_fox_o,
     w_ffn_gate, w_ffn_up, w_ffn_down) = weights
    b, length, d = x.shape
    m = b * length
    h = x.reshape(m, d)

    cos, sin = _rope_tables(pos)
    if cos.shape[0] < tm:
        cos = jnp.tile(cos, (tm // cos.shape[0], 1))
        sin = jnp.tile(sin, (tm // sin.shape[0], 1))
    proj = _ret_in(h, norm_mix[0:1], cos, sin, w_ret_in, tm=tm)
    o, s_fin = _retention(proj, s0, batch=b, seq=length, chunk=chunk)
    h = _mm_res(o, w_ret_o, h, tm=tm, tn=COL_TILE // 2)
    h = _ffn(h, norm_ffn[0:1], w_ffn_gate[0], w_ffn_up[0], w_ffn_down[0], tm=ffn_tm)

    tk = FOX_K_TILE
    if cache is None:
        q_start = 0
        k32, v32, k_heads, v_t, lf_pad = _kv_proj(h, norm_kv[None, :], w_kv, b_kv, tm=tm, key_tile=tk)
        k_heads = k_heads.reshape(FOX_KV_HEADS, b, length, FOX_HD)
        v_t = v_t.reshape(FOX_KV_HEADS, b, length // tk, FOX_V_ROWS, tk)
        lf_all = lf_pad.reshape(b, length, LANES)
    else:
        cache_k, cache_v, cache_logf = cache
        q_start = cache_k.shape[1]
        lk_pad = -(-(q_start + length) // tk) * tk
        k32, v32, k_heads, v_heads, lf_pad = _kv_proj(h, norm_kv[None, :], w_kv, b_kv, tm=tm, key_tile=None)
        k_heads = _with_cache(k_heads.reshape(FOX_KV_HEADS, b, length, FOX_HD), cache_k, lk_pad)
        v_t = _transpose_values(_with_cache(v_heads.reshape(FOX_KV_HEADS, b, length, FOX_HD), cache_v, lk_pad), tk)
        lf_all = jnp.concatenate([
            jnp.pad(cache_logf.astype(F32), ((0, 0), (0, 0), (0, LANES - FOX_HEADS))),
            lf_pad.reshape(b, length, LANES)], axis=1)
        lf_all = jnp.pad(lf_all, ((0, 0), (0, lk_pad - lf_all.shape[1]), (0, 0)))
    lf = lf_pad[:, :FOX_HEADS].reshape(b, length, FOX_HEADS)
    f_pieces = _cumsum_pieces(lf_all, tl=CUMSUM_TILE)

    q = _q_proj(h, norm_mix[1:2], w_fox_q, tm=tm).reshape(b, length, d)
    att = _fox_attention(q, k_heads, f_pieces, v_t, tq=tq, tk=tk, q_start=q_start)
    h = _mm_res(att.reshape(m, d), w_fox_o, h, tm=tm, tn=COL_TILE // 2)
    y = _ffn(h, norm_ffn[1:2], w_ffn_gate[1], w_ffn_up[1], w_ffn_down[1], norm_final[None, :], tm=ffn_tm)

    return (y.reshape(b, length, d), s_fin[None],
            k32.reshape(b, length, FOX_KV_HEADS, FOX_HD), v32.reshape(b, length, FOX_KV_HEADS, FOX_HD), lf)


def kernel(x_prompt, x_sample, state_ret, cache_k, cache_v, cache_logf, norm_mix, norm_ffn, norm_kv, norm_final, w_ret_in, w_ret_o, w_kv_k, w_kv_v, w_kv_f, b_kv_f, w_fox_q, w_fox_o, w_ffn_gate, w_ffn_up, w_ffn_down):
    lp = x_prompt.shape[1]
    bs, ls, _ = x_sample.shape
    past = cache_k.shape[1]

    w_kv = jnp.concatenate(
        [w_kv_k, w_kv_v, jnp.pad(w_kv_f, ((0, 0), (0, LANES - FOX_HEADS)))], axis=1).astype(BF16)
    b_kv = jnp.pad(b_kv_f.astype(F32), (0, LANES - FOX_HEADS))[None, :]
    weights = (norm_mix.astype(F32), norm_ffn.astype(F32), norm_kv.astype(F32), norm_final.astype(F32),
               w_ret_in[0].astype(BF16), w_ret_o[0].astype(BF16), w_kv, b_kv,
               w_fox_q[0].astype(BF16), w_fox_o[0].astype(BF16),
               w_ffn_gate.astype(BF16), w_ffn_up.astype(BF16), w_ffn_down.astype(BF16))

    y_p, s_p, k_p, v_p, lf_p = _stream(
        x_prompt, pos=jnp.arange(lp), s0=None, cache=None, weights=weights,
        tm=ROW_TILE, ffn_tm=FFN_ROW_TILE, chunk=RET_CHUNK, tq=FOX_Q_TILE)
    y_s, s_s, k_s, v_s, lf_s = _stream(
        x_sample, pos=past + jnp.arange(ls), s0=state_ret[0], cache=(cache_k, cache_v, cache_logf),
        weights=weights, tm=bs * ls, ffn_tm=bs * ls, chunk=ls, tq=ls)
    return (y_p, y_s, s_p, k_p, v_p, lf_p, s_s, k_s, v_s, lf_s)
```

```python
import functools

import jax
import jax.numpy as jnp
from jax import lax
from jax.experimental import pallas as pl
from jax.experimental.pallas import tpu as pltpu

F32 = jnp.float32
BF16 = jnp.bfloat16

D_MODEL = 2048
RET_HEADS = 8
RET_DK = D_MODEL // RET_HEADS
RET_DV = 2 * D_MODEL // RET_HEADS
RET_QK = RET_HEADS * RET_DK
RET_V = RET_HEADS * RET_DV
ROPE_BASE = 10000.0
FOX_HEADS = 16
FOX_HD = D_MODEL // FOX_HEADS
FOX_KV_HEADS = 4
FOX_GROUP = FOX_HEADS // FOX_KV_HEADS
FOX_KV = FOX_KV_HEADS * FOX_HD
FOX_F_PIECES = 3
FOX_V_ROWS = FOX_HD + 16
LOG2_E = 1.4426950408889634
NORM_EPS = 1e-6
GN_EPS = 1e-5

LANES = 128
VMEM_LIMIT_BYTES = 56 * 1024 * 1024

ROW_TILE = 1024
COL_TILE = 1024
FFN_ROW_TILE = 1024
FFN_COL_TILE = 256
RET_CHUNK = 256
RET_HEADS_PER_STEP = 4
FOX_Q_TILE = 256
FOX_K_TILE = 512
FOX_COL_CHUNK = 512
CUMSUM_TILE = 512
MASK_VALUE = -1e30


def _params(*semantics):
    return pltpu.CompilerParams(dimension_semantics=semantics, vmem_limit_bytes=VMEM_LIMIT_BYTES)


def _rmsnorm(x, gain):
    y = x * lax.rsqrt(jnp.mean(x * x, axis=-1, keepdims=True) + NORM_EPS)
    return y * gain


def _silu(x):
    return x * (1.0 / (1.0 + jnp.exp(-x)))


def _ret_in_kernel(x_ref, gain_ref, cos_ref, sin_ref, w_ref, o_ref, xn_ref, *, n_q_tiles, n_rope_tiles, n_gate_first):
    j = pl.program_id(1)

    @pl.when(j == 0)
    def _():
        xn_ref[...] = _rmsnorm(x_ref[...], gain_ref[...]).astype(BF16)

    y = jnp.dot(xn_ref[...], w_ref[...], preferred_element_type=F32)

    @pl.when(j < n_rope_tiles)
    def _():
        cos = cos_ref[...]
        sin = sin_ref[...]
        scale = jnp.where(j >= n_q_tiles, RET_DK ** -0.5, 1.0).astype(F32)
        half = RET_DK // 2
        for h in range(y.shape[1] // RET_DK):
            x1 = y[:, h * RET_DK:h * RET_DK + half]
            x2 = y[:, h * RET_DK + half:(h + 1) * RET_DK]
            o_ref[:, h * RET_DK:h * RET_DK + half] = ((x1 * cos - x2 * sin) * scale).astype(BF16)
            o_ref[:, h * RET_DK + half:(h + 1) * RET_DK] = ((x1 * sin + x2 * cos) * scale).astype(BF16)

    @pl.when(jnp.logical_and(j >= n_rope_tiles, j < n_gate_first))
    def _():
        o_ref[...] = y.astype(BF16)

    @pl.when(j >= n_gate_first)
    def _():
        o_ref[...] = _silu(y).astype(BF16)


def _ret_in(x, gain, cos, sin, w, *, tm):
    m, d = x.shape
    n = w.shape[1]
    tn = COL_TILE
    pos_tiles = cos.shape[0] // tm
    kernel = functools.partial(_ret_in_kernel, n_q_tiles=RET_QK // tn, n_rope_tiles=2 * RET_QK // tn,
                               n_gate_first=(2 * RET_QK + RET_V) // tn)
    return pl.pallas_call(
        kernel,
        grid=(m // tm, n // tn),
        in_specs=[
            pl.BlockSpec((tm, d), lambda i, j: (i, 0)),
            pl.BlockSpec((1, d), lambda i, j: (0, 0)),
            pl.BlockSpec((tm, RET_DK // 2), lambda i, j: (i % pos_tiles, 0)),
            pl.BlockSpec((tm, RET_DK // 2), lambda i, j: (i % pos_tiles, 0)),
            pl.BlockSpec((d, tn), lambda i, j: (0, j)),
        ],
        out_specs=pl.BlockSpec((tm, tn), lambda i, j: (i, j)),
        out_shape=jax.ShapeDtypeStruct((m, n), BF16),
        scratch_shapes=[pltpu.VMEM((tm, d), BF16)],
        compiler_params=_params("parallel", "arbitrary"),
        name="ret_in_proj",
    )(x, gain, cos, sin, w)


def _retention_kernel(*refs, chunk, has_init):
    if has_init:
        q_ref, k_ref, v_ref, g_ref, s0_ref, o_ref, s_ref, dm_ref, rd_ref, kw_ref, raw_ref = refs
    else:
        q_ref, k_ref, v_ref, g_ref, o_ref, s_ref, dm_ref, rd_ref, kw_ref, raw_ref = refs
        s0_ref = None
    c = pl.program_id(2)
    heads = RET_HEADS_PER_STEP

    def log_gamma(hh):
        head = pl.program_id(1) * heads + hh
        return jnp.log(1.0 - jnp.exp2(-5.0 - jnp.full((1, 1), head, jnp.int32).astype(F32)))

    @pl.when(c == 0)
    def _():
        li = lax.broadcasted_iota(jnp.int32, (chunk, chunk), 0)
        mi = lax.broadcasted_iota(jnp.int32, (chunk, chunk), 1)
        diff = (li - mi).astype(F32)
        idx = lax.broadcasted_iota(jnp.int32, (chunk, 1), 0).astype(F32)
        for hh in range(heads):
            lg = log_gamma(hh)
            dm_ref[hh] = jnp.where(diff >= 0, jnp.exp(lg * jnp.maximum(diff, 0.0)), 0.0)
            rd_ref[hh] = jnp.exp(lg * (idx + 1.0))
            kw_ref[hh] = jnp.exp(lg * (chunk - 1.0 - idx))
        if has_init:
            s_ref[...] = s0_ref[...]
        else:
            s_ref[...] = jnp.zeros_like(s_ref)

    for hh in range(heads):
        q = q_ref[:, hh * RET_DK:(hh + 1) * RET_DK]
        k = k_ref[:, hh * RET_DK:(hh + 1) * RET_DK]
        v = v_ref[:, hh * RET_DV:(hh + 1) * RET_DV]
        s_old = s_ref[0, hh]
        scores = lax.dot_general(q, k, (((1,), (1,)), ((), ())), preferred_element_type=F32) * dm_ref[hh]
        inner = jnp.dot(scores.astype(BF16), v, preferred_element_type=F32)
        cross = jnp.dot(q, s_old.astype(BF16), preferred_element_type=F32) * rd_ref[hh]
        raw_ref[hh] = inner + cross
        kd = (k.astype(F32) * kw_ref[hh]).astype(BF16)
        s_ref[0, hh] = jnp.exp(log_gamma(hh) * float(chunk)) * s_old + lax.dot_general(
            kd, v, (((0,), (0,)), ((), ())), preferred_element_type=F32)

    for hh in range(heads):
        o = raw_ref[hh]
        mu = jnp.mean(o, axis=-1, keepdims=True)
        cen = o - mu
        var = jnp.mean(cen * cen, axis=-1, keepdims=True)
        on = cen * lax.rsqrt(var + GN_EPS)
        gate = g_ref[:, hh * RET_DV:(hh + 1) * RET_DV].astype(F32)
        o_ref[:, hh * RET_DV:(hh + 1) * RET_DV] = (on * gate).astype(BF16)


def _retention(proj, s0, *, batch, seq, chunk):
    nc = seq // chunk
    has_init = s0 is not None
    heads = RET_HEADS_PER_STEP
    qk_w = heads * RET_DK
    v_w = heads * RET_DV
    k_off = RET_QK // qk_w
    v_off = 2 * RET_QK // v_w
    g_off = (2 * RET_QK + RET_V) // v_w
    in_specs = [
        pl.BlockSpec((chunk, qk_w), lambda b, h, c: (b * nc + c, h)),
        pl.BlockSpec((chunk, qk_w), lambda b, h, c: (b * nc + c, k_off + h)),
        pl.BlockSpec((chunk, v_w), lambda b, h, c: (b * nc + c, v_off + h)),
        pl.BlockSpec((chunk, v_w), lambda b, h, c: (b * nc + c, g_off + h)),
    ]
    args = [proj, proj, proj, proj]
    if has_init:
        in_specs.append(pl.BlockSpec((1, heads, RET_DK, RET_DV), lambda b, h, c: (b, h, 0, 0)))
        args.append(s0)
    return pl.pallas_call(
        functools.partial(_retention_kernel, chunk=chunk, has_init=has_init),
        grid=(batch, RET_HEADS // heads, nc),
        in_specs=in_specs,
        out_specs=[
            pl.BlockSpec((chunk, v_w), lambda b, h, c: (b * nc + c, h)),
            pl.BlockSpec((1, heads, RET_DK, RET_DV), lambda b, h, c: (b, h, 0, 0)),
        ],
        out_shape=[
            jax.ShapeDtypeStruct((batch * seq, RET_V), BF16),
            jax.ShapeDtypeStruct((batch, RET_HEADS, RET_DK, RET_DV), F32),
        ],
        scratch_shapes=[
            pltpu.VMEM((heads, chunk, chunk), F32),
            pltpu.VMEM((heads, chunk, 1), F32),
            pltpu.VMEM((heads, chunk, 1), F32),
            pltpu.VMEM((heads, chunk, RET_DV), F32),
        ],
        compiler_params=_params("parallel", "parallel", "arbitrary"),
        name="retention",
    )(*args)


def _mm_res_kernel(a_ref, w_ref, r_ref, o_ref):
    o_ref[...] = r_ref[...] + jnp.dot(a_ref[...], w_ref[...], preferred_element_type=F32)


def _mm_res(a, w, res, *, tm, tn):
    m, kdim = a.shape
    n = w.shape[1]
    return pl.pallas_call(
        _mm_res_kernel,
        grid=(m // tm, n // tn),
        in_specs=[
            pl.BlockSpec((tm, kdim), lambda i, j: (i, 0)),
            pl.BlockSpec((kdim, tn), lambda i, j: (0, j)),
            pl.BlockSpec((tm, tn), lambda i, j: (i, j)),
        ],
        out_specs=pl.BlockSpec((tm, tn), lambda i, j: (i, j)),
        out_shape=jax.ShapeDtypeStruct((m, n), F32),
        compiler_params=_params("parallel", "arbitrary"),
        name="matmul_residual",
    )(a, w, res)


def _ffn_kernel(*refs, final_norm):
    if final_norm:
        x_ref, gain_ref, wg_ref, wu_ref, wd_ref, fgain_ref, o_ref, xn_ref = refs
    else:
        x_ref, gain_ref, wg_ref, wu_ref, wd_ref, o_ref, xn_ref = refs
    j = pl.program_id(1)

    @pl.when(j == 0)
    def _():
        x = x_ref[...]
        xn_ref[...] = _rmsnorm(x, gain_ref[...]).astype(BF16)
        o_ref[...] = x

    xn = xn_ref[...]
    gate = jnp.dot(xn, wg_ref[...], preferred_element_type=F32)
    up = jnp.dot(xn, wu_ref[...], preferred_element_type=F32)
    hidden = (_silu(gate) * up).astype(BF16)
    o_ref[...] += jnp.dot(hidden, wd_ref[...], preferred_element_type=F32)

    if final_norm:
        @pl.when(j == pl.num_programs(1) - 1)
        def _():
            o_ref[...] = _rmsnorm(o_ref[...], fgain_ref[...])


def _ffn(x, gain, wg, wu, wd, final_gain=None, *, tm):
    m, d = x.shape
    f = wg.shape[1]
    tf = FFN_COL_TILE
    final_norm = final_gain is not None
    in_specs = [
        pl.BlockSpec((tm, d), lambda i, j: (i, 0)),
        pl.BlockSpec((1, d), lambda i, j: (0, 0)),
        pl.BlockSpec((d, tf), lambda i, j: (0, j)),
        pl.BlockSpec((d, tf), lambda i, j: (0, j)),
        pl.BlockSpec((tf, d), lambda i, j: (j, 0)),
    ]
    args = [x, gain, wg, wu, wd]
    if final_norm:
        in_specs.append(pl.BlockSpec((1, d), lambda i, j: (0, 0)))
        args.append(final_gain)
    return pl.pallas_call(
        functools.partial(_ffn_kernel, final_norm=final_norm),
        grid=(m // tm, f // tf),
        in_specs=in_specs,
        out_specs=pl.BlockSpec((tm, d), lambda i, j: (i, 0)),
        out_shape=jax.ShapeDtypeStruct((m, d), F32),
        scratch_shapes=[pltpu.VMEM((tm, d), BF16)],
        compiler_params=_params("parallel", "arbitrary"),
        name="ffn",
    )(*args)


def _kv_kernel(x_ref, gain_ref, w_ref, bias_ref, k32_ref, v32_ref, kh_ref, vh_ref, lf_ref, lf16_ref, *, key_tile):
    xn = _rmsnorm(x_ref[...], gain_ref[...]).astype(BF16)
    y = jnp.dot(xn, w_ref[...], preferred_element_type=F32)
    k = y[:, :FOX_KV]
    v = y[:, FOX_KV:2 * FOX_KV]
    tm = k.shape[0]
    for h in range(FOX_KV_HEADS):
        hs = slice(h * FOX_HD, (h + 1) * FOX_HD)
        k32_ref[pl.ds(h, tm, stride=FOX_KV_HEADS), :] = k[:, hs]
        v32_ref[pl.ds(h, tm, stride=FOX_KV_HEADS), :] = v[:, hs]
        kh_ref[h] = k[:, hs].astype(BF16)
        if key_tile is None:
            vh_ref[h] = v[:, hs].astype(BF16)
        else:
            v_t = v[:, hs].T.astype(BF16)
            for t in range(tm // key_tile):
                vh_ref[h, t, :FOX_HD, :] = v_t[:, t * key_tile:(t + 1) * key_tile]
                vh_ref[h, t, FOX_HD:, :] = jnp.ones((FOX_V_ROWS - FOX_HD, key_tile), BF16)
    z = y[:, 2 * FOX_KV:] + bias_ref[...]
    lf = jnp.minimum(z, 0.0) - jnp.log1p(jnp.exp(-jnp.abs(z)))
    lf_ref[...] = lf
    lf16_ref[...] = lf[:, :FOX_HEADS]


def _kv_proj(x, gain, w, bias, *, tm, key_tile):
    m, d = x.shape
    n = w.shape[1]
    if key_tile is None:
        v_spec = pl.BlockSpec((FOX_KV_HEADS, tm, FOX_HD), lambda i: (0, i, 0))
        v_shape = jax.ShapeDtypeStruct((FOX_KV_HEADS, m, FOX_HD), BF16)
    else:
        v_spec = pl.BlockSpec((FOX_KV_HEADS, tm // key_tile, FOX_V_ROWS, key_tile), lambda i: (0, i, 0, 0))
        v_shape = jax.ShapeDtypeStruct((FOX_KV_HEADS, m // key_tile, FOX_V_ROWS, key_tile), BF16)
    return pl.pallas_call(
        functools.partial(_kv_kernel, key_tile=key_tile),
        grid=(m // tm,),
        in_specs=[
            pl.BlockSpec((tm, d), lambda i: (i, 0)),
            pl.BlockSpec((1, d), lambda i: (0, 0)),
            pl.BlockSpec((d, n), lambda i: (0, 0)),
            pl.BlockSpec((1, LANES), lambda i: (0, 0)),
        ],
        out_specs=[
            pl.BlockSpec((tm * FOX_KV_HEADS, FOX_HD), lambda i: (i, 0)),
            pl.BlockSpec((tm * FOX_KV_HEADS, FOX_HD), lambda i: (i, 0)),
            pl.BlockSpec((FOX_KV_HEADS, tm, FOX_HD), lambda i: (0, i, 0)),
            v_spec,
            pl.BlockSpec((tm, LANES), lambda i: (i, 0)),
            pl.BlockSpec((tm, FOX_HEADS), lambda i: (i, 0)),
        ],
        out_shape=[
            jax.ShapeDtypeStruct((m * FOX_KV_HEADS, FOX_HD), F32),
            jax.ShapeDtypeStruct((m * FOX_KV_HEADS, FOX_HD), F32),
            jax.ShapeDtypeStruct((FOX_KV_HEADS, m, FOX_HD), BF16),
            v_shape,
            jax.ShapeDtypeStruct((m, LANES), F32),
            jax.ShapeDtypeStruct((m, FOX_HEADS), F32),
        ],
        compiler_params=_params("parallel"),
        name="kv_proj",
    )(x, gain, w, bias)


def _q_kernel(x_ref, gain_ref, w_ref, o_ref, xn_ref):
    @pl.when(pl.program_id(1) == 0)
    def _():
        xn_ref[...] = _rmsnorm(x_ref[...], gain_ref[...]).astype(BF16)

    y = jnp.dot(xn_ref[...], w_ref[...], preferred_element_type=F32)
    o_ref[...] = (y * (FOX_HD ** -0.5 * LOG2_E)).astype(BF16)


def _q_proj(x, gain, w, *, tm):
    m, d = x.shape
    n = w.shape[1]
    tn = COL_TILE
    return pl.pallas_call(
        _q_kernel,
        grid=(m // tm, n // tn),
        in_specs=[
            pl.BlockSpec((tm, d), lambda i, j: (i, 0)),
            pl.BlockSpec((1, d), lambda i, j: (0, 0)),
            pl.BlockSpec((d, tn), lambda i, j: (0, j)),
        ],
        out_specs=pl.BlockSpec((tm, tn), lambda i, j: (i, j)),
        out_shape=jax.ShapeDtypeStruct((m, n), BF16),
        scratch_shapes=[pltpu.VMEM((tm, d), BF16)],
        compiler_params=_params("parallel", "arbitrary"),
        name="q_proj",
    )(x, gain, w)


def _cumsum_kernel(x_ref, o_ref, carry_ref):
    @pl.when(pl.program_id(1) == 0)
    def _():
        carry_ref[...] = jnp.zeros_like(carry_ref)

    x = x_ref[0]
    tl = x.shape[0]
    r = lax.broadcasted_iota(jnp.int32, (tl, tl), 0)
    c = lax.broadcasted_iota(jnp.int32, (tl, tl), 1)
    tri = jnp.where(c <= r, 1.0, 0.0).astype(F32)
    y = jnp.dot(tri, x, precision=lax.Precision.HIGHEST, preferred_element_type=F32) + carry_ref[...]
    carry_ref[...] = y[tl - 1:tl, :]
    y2 = y * LOG2_E
    hi = y2.astype(BF16).astype(F32)
    rest = y2 - hi
    mid = rest.astype(BF16).astype(F32)
    lo = rest - mid
    lane = lax.broadcasted_iota(jnp.int32, y.shape, 1)
    packed = jnp.where(
        lane < FOX_HEADS, hi,
        jnp.where(lane < 2 * FOX_HEADS, pltpu.roll(mid, FOX_HEADS, 1),
                  jnp.where(lane < 3 * FOX_HEADS, pltpu.roll(lo, 2 * FOX_HEADS, 1), 0.0)))
    o_ref[0] = packed.astype(BF16)


def _cumsum_pieces(x, *, tl):
    b, length, lanes = x.shape
    return pl.pallas_call(
        _cumsum_kernel,
        grid=(b, length // tl),
        in_specs=[pl.BlockSpec((1, tl, lanes), lambda i, t: (i, t, 0))],
        out_specs=pl.BlockSpec((1, tl, lanes), lambda i, t: (i, t, 0)),
        out_shape=jax.ShapeDtypeStruct((b, length, lanes), BF16),
        scratch_shapes=[pltpu.VMEM((1, lanes), F32)],
        compiler_params=_params("parallel", "arbitrary"),
        name="logf_cumsum",
    )(x)


def _fox_kernel(q_ref, k_ref, f_ref, v_ref, o_ref, qa_ref, z0_ref, z1_ref, mb0_ref, mb1_ref, m_ref, acc_ref,
                *, tq, tk, q_start, blocks):
    step = pl.program_id(2)
    rows = FOX_GROUP * tq
    lane = lax.broadcasted_iota(jnp.int32, (tq, FOX_HD), 1)
    for x in range(blocks):
        for g in range(FOX_GROUP):
            head = pl.program_id(1) * FOX_GROUP + g
            pick = jnp.where(lane < FOX_F_PIECES * FOX_HEADS,
                             jnp.where((lane & (FOX_HEADS - 1)) == head, -1.0, 0.0), 0.0)
            qa_ref[x, g * tq:(g + 1) * tq, :FOX_HD] = q_ref[0, x * tq:(x + 1) * tq, g * FOX_HD:(g + 1) * FOX_HD]
            qa_ref[x, g * tq:(g + 1) * tq, FOX_HD:] = pick.astype(BF16)
    n_full = (q_start + step * blocks * tq) // tk

    m_ref[...] = jnp.full_like(m_ref, MASK_VALUE)
    acc_ref[...] = jnp.zeros_like(acc_ref)

    cw = min(rows, FOX_COL_CHUNK)
    chunks = [slice(c * cw, (c + 1) * cw) for c in range(rows // cw)]
    buffers = ((z0_ref, mb0_ref), (z1_ref, mb1_ref))

    def score(x, kb, masked, buf):
        z_ref, mb_ref = buffers[buf]
        start = pl.multiple_of(kb * tk, tk)
        ka = jnp.concatenate([k_ref[0, 0, pl.ds(start, tk), :], f_ref[0, pl.ds(start, tk), :]], axis=1)
        for c, cs in enumerate(chunks):
            z = lax.dot_general(ka, qa_ref[x, cs, :], (((1,), (1,)), ((), ())), preferred_element_type=F32)
            if masked:
                kpos = kb * tk + lax.broadcasted_iota(jnp.int32, (tk, cw), 0)
                col = c * cw + lax.broadcasted_iota(jnp.int32, (tk, cw), 1)
                qpos = q_start + (step * blocks + x) * tq + (col & (tq - 1))
                z = jnp.where(kpos <= qpos, z, MASK_VALUE)
            z_ref[:, cs] = z
            mb_ref[:, cs] = jnp.max(z, axis=0, keepdims=True)

    def accumulate(x, kb, buf):
        z_ref, mb_ref = buffers[buf]
        vt = v_ref[0, 0, kb]
        for cs in chunks:
            m_old = m_ref[x, :, cs]
            m_new = jnp.maximum(m_old, mb_ref[:, cs])
            alpha = jnp.exp2(m_old - m_new)
            p = jnp.exp2(z_ref[:, cs] - m_new)
            acc_ref[x, :, cs] = alpha * acc_ref[x, :, cs] + jnp.dot(vt, p.astype(BF16), preferred_element_type=F32)
            m_ref[x, :, cs] = m_new

    if blocks == 2:
        score(0, n_full, True, 0)
        score(1, n_full, True, 1)
        accumulate(0, n_full, 0)

        def tile(kb, pending):
            score(0, kb, False, 0)
            accumulate(1, pending, 1)
            score(1, kb, False, 1)
            accumulate(0, kb, 0)
            return kb

        pending = lax.fori_loop(0, n_full, tile, n_full)
        accumulate(1, pending, 1)
    else:
        score(0, n_full, True, 0)

        def pair(i, pending):
            score(0, 2 * i, False, 1)
            accumulate(0, pending, 0)
            score(0, 2 * i + 1, False, 0)
            accumulate(0, 2 * i, 1)
            return 2 * i + 1

        pending = lax.fori_loop(0, n_full // 2, pair, n_full)

        @pl.when(n_full % 2 == 1)
        def _():
            score(0, n_full - 1, False, 1)
            accumulate(0, pending, 0)
            accumulate(0, n_full - 1, 1)

        @pl.when(n_full % 2 == 0)
        def _():
            accumulate(0, pending, 0)

    for x in range(blocks):
        acc = acc_ref[x]
        out = (acc[:FOX_HD] / acc[FOX_HD:FOX_HD + 1]).T
        for g in range(FOX_GROUP):
            o_ref[0, x * tq:(x + 1) * tq, g * FOX_HD:(g + 1) * FOX_HD] = out[g * tq:(g + 1) * tq, :].astype(BF16)


def _fox_attention(q, k_heads, f_pieces, v_t, *, tq, tk, q_start):
    b, lq, _ = q.shape
    lk = k_heads.shape[2]
    nk = lk // tk
    rows = FOX_GROUP * tq
    blocks = 2 if (2 * tq == tk and lq % tk == 0) else 1
    assert tq & (tq - 1) == 0 and tk % tq == 0 and q_start % tk == 0
    return pl.pallas_call(
        functools.partial(_fox_kernel, tq=tq, tk=tk, q_start=q_start, blocks=blocks),
        grid=(b, FOX_KV_HEADS, lq // (blocks * tq)),
        in_specs=[
            pl.BlockSpec((1, blocks * tq, FOX_GROUP * FOX_HD), lambda i, h, t: (i, t, h)),
            pl.BlockSpec((1, 1, lk, FOX_HD), lambda i, h, t: (h, i, 0, 0)),
            pl.BlockSpec((1, lk, LANES), lambda i, h, t: (i, 0, 0)),
            pl.BlockSpec((1, 1, nk, FOX_V_ROWS, tk), lambda i, h, t: (h, i, 0, 0, 0)),
        ],
        out_specs=pl.BlockSpec((1, blocks * tq, FOX_GROUP * FOX_HD), lambda i, h, t: (i, t, h)),
        out_shape=jax.ShapeDtypeStruct(q.shape, BF16),
        scratch_shapes=[
            pltpu.VMEM((blocks, rows, 2 * FOX_HD), BF16),
            pltpu.VMEM((tk, rows), F32),
            pltpu.VMEM((tk, rows), F32),
            pltpu.VMEM((1, rows), F32),
            pltpu.VMEM((1, rows), F32),
            pltpu.VMEM((blocks, 1, rows), F32),
            pltpu.VMEM((blocks, FOX_V_ROWS, rows), F32),
        ],
        compiler_params=_params("parallel", "parallel", "arbitrary"),
        name="fox_attention",
    )(q, k_heads, f_pieces, v_t)


def _rope_tables(pos):
    half = RET_DK // 2
    inv = ROPE_BASE ** (-jnp.arange(half, dtype=F32) / half)
    ang = pos.astype(F32)[:, None] * inv[None, :]
    return jnp.cos(ang), jnp.sin(ang)


def _with_cache(new_heads, cache, lk_pad):
    past = cache.transpose(2, 0, 1, 3).astype(BF16)
    both = jnp.concatenate([past, new_heads], axis=2)
    return jnp.pad(both, ((0, 0), (0, 0), (0, lk_pad - both.shape[2]), (0, 0)))


def _transpose_values(v_heads, tk):
    kvh, b, lk, _ = v_heads.shape
    v_t = v_heads.reshape(kvh, b, lk // tk, tk, FOX_HD).transpose(0, 1, 2, 4, 3)
    ones = jnp.ones((kvh, b, lk // tk, FOX_V_ROWS - FOX_HD, tk), BF16)
    return jnp.concatenate([v_t, ones], axis=3)


def _stream(x, *, pos, s0, cache, weights, tm, ffn_tm, chunk, tq):
    (norm_mix, norm_ffn, norm_kv, norm_final, w_ret_in, w_ret_o, w_kv, b_kv, w_fox_q, w_fox_o,
     w_ffn_gate, w_ffn_up, w_ffn_down) = weights
    b, length, d = x.shape
    m = b * length
    h = x.reshape(m, d)

    cos, sin = _rope_tables(pos)
    if cos.shape[0] < tm:
        cos = jnp.tile(cos, (tm // cos.shape[0], 1))
        sin = jnp.tile(sin, (tm // sin.shape[0], 1))
    proj = _ret_in(h, norm_mix[0:1], cos, sin, w_ret_in, tm=tm)
    o, s_fin = _retention(proj, s0, batch=b, seq=length, chunk=chunk)
    h = _mm_res(o, w_ret_o, h, tm=tm, tn=COL_TILE // 2)
    h = _ffn(h, norm_ffn[0:1], w_ffn_gate[0], w_ffn_up[0], w_ffn_down[0], tm=ffn_tm)

    tk = FOX_K_TILE
    if cache is None:
        q_start = 0
        k32, v32, k_heads, v_t, lf_pad, lf = _kv_proj(h, norm_kv[None, :], w_kv, b_kv, tm=tm, key_tile=tk)
        k_heads = k_heads.reshape(FOX_KV_HEADS, b, length, FOX_HD)
        v_t = v_t.reshape(FOX_KV_HEADS, b, length // tk, FOX_V_ROWS, tk)
        lf_all = lf_pad.reshape(b, length, LANES)
    else:
        cache_k, cache_v, cache_logf = cache
        q_start = cache_k.shape[1]
        lk_pad = -(-(q_start + length) // tk) * tk
        k32, v32, k_heads, v_heads, lf_pad, lf = _kv_proj(h, norm_kv[None, :], w_kv, b_kv, tm=tm, key_tile=None)
        k_heads = _with_cache(k_heads.reshape(FOX_KV_HEADS, b, length, FOX_HD), cache_k, lk_pad)
        v_t = _transpose_values(_with_cache(v_heads.reshape(FOX_KV_HEADS, b, length, FOX_HD), cache_v, lk_pad), tk)
        lf_all = jnp.concatenate([
            jnp.pad(cache_logf.astype(F32), ((0, 0), (0, 0), (0, LANES - FOX_HEADS))),
            lf_pad.reshape(b, length, LANES)], axis=1)
        lf_all = jnp.pad(lf_all, ((0, 0), (0, lk_pad - lf_all.shape[1]), (0, 0)))
    lf = lf.reshape(b, length, FOX_HEADS)
    f_pieces = _cumsum_pieces(lf_all, tl=CUMSUM_TILE)

    q = _q_proj(h, norm_mix[1:2], w_fox_q, tm=tm).reshape(b, length, d)
    att = _fox_attention(q, k_heads, f_pieces, v_t, tq=tq, tk=tk, q_start=q_start)
    h = _mm_res(att.reshape(m, d), w_fox_o, h, tm=tm, tn=COL_TILE // 2)
    y = _ffn(h, norm_ffn[1:2], w_ffn_gate[1], w_ffn_up[1], w_ffn_down[1], norm_final[None, :], tm=ffn_tm)

    return (y.reshape(b, length, d), s_fin[None],
            k32.reshape(b, length, FOX_KV_HEADS, FOX_HD), v32.reshape(b, length, FOX_KV_HEADS, FOX_HD), lf)


def kernel(x_prompt, x_sample, state_ret, cache_k, cache_v, cache_logf, norm_mix, norm_ffn, norm_kv, norm_final, w_ret_in, w_ret_o, w_kv_k, w_kv_v, w_kv_f, b_kv_f, w_fox_q, w_fox_o, w_ffn_gate, w_ffn_up, w_ffn_down):
    lp = x_prompt.shape[1]
    bs, ls, _ = x_sample.shape
    past = cache_k.shape[1]

    w_kv = jnp.concatenate(
        [w_kv_k, w_kv_v, jnp.pad(w_kv_f, ((0, 0), (0, LANES - FOX_HEADS)))], axis=1).astype(BF16)
    b_kv = jnp.pad(b_kv_f.astype(F32), (0, LANES - FOX_HEADS))[None, :]
    weights = (norm_mix.astype(F32), norm_ffn.astype(F32), norm_kv.astype(F32), norm_final.astype(F32),
               w_ret_in[0].astype(BF16), w_ret_o[0].astype(BF16), w_kv, b_kv,
               w_fox_q[0].astype(BF16), w_fox_o[0].astype(BF16),
               w_ffn_gate.astype(BF16), w_ffn_up.astype(BF16), w_ffn_down.astype(BF16))

    y_p, s_p, k_p, v_p, lf_p = _stream(
        x_prompt, pos=jnp.arange(lp), s0=None, cache=None, weights=weights,
        tm=ROW_TILE, ffn_tm=FFN_ROW_TILE, chunk=RET_CHUNK, tq=FOX_Q_TILE)
    y_s, s_s, k_s, v_s, lf_s = _stream(
        x_sample, pos=past + jnp.arange(ls), s0=state_ret[0], cache=(cache_k, cache_v, cache_logf),
        weights=weights, tm=bs * ls, ffn_tm=bs * ls, chunk=ls, tq=ls)
    return (y_p, y_s, s_p, k_p, v_p, lf_p, s_s, k_s, v_s, lf_s)
```

```python
import functools

import jax
import jax.numpy as jnp
from jax import lax
from jax.experimental import pallas as pl
from jax.experimental.pallas import tpu as pltpu

F32 = jnp.float32
BF16 = jnp.bfloat16

D_MODEL = 2048
RET_HEADS = 8
RET_DK = D_MODEL // RET_HEADS
RET_DV = 2 * D_MODEL // RET_HEADS
RET_QK = RET_HEADS * RET_DK
RET_V = RET_HEADS * RET_DV
RET_HEAD_COLS = 2 * RET_DK + 2 * RET_DV
ROPE_BASE = 10000.0
FOX_HEADS = 16
FOX_HD = D_MODEL // FOX_HEADS
FOX_KV_HEADS = 4
FOX_GROUP = FOX_HEADS // FOX_KV_HEADS
FOX_KV = FOX_KV_HEADS * FOX_HD
FOX_F_PIECES = 3
FOX_V_ROWS = FOX_HD + 16
LOG2_E = 1.4426950408889634
NORM_EPS = 1e-6
GN_EPS = 1e-5

LANES = 128
VMEM_LIMIT_BYTES = 56 * 1024 * 1024

ROW_TILE = 1024
COL_TILE = 1024
FFN_ROW_TILE = 1024
FFN_COL_TILE = 256
FFN_COL_TILE_SMALL_ROWS = 512
RET_CHUNK = 256
RET_HEADS_PER_STEP = 4
FOX_Q_TILE = 256
FOX_K_TILE = 512
FOX_COL_CHUNK = 512
CUMSUM_TILE = 512
MASK_VALUE = -1e30


def _params(*semantics):
    return pltpu.CompilerParams(dimension_semantics=semantics, vmem_limit_bytes=VMEM_LIMIT_BYTES)


def _rmsnorm(x, gain):
    y = x * lax.rsqrt(jnp.mean(x * x, axis=-1, keepdims=True) + NORM_EPS)
    return y * gain


def _silu(x):
    return x * (1.0 / (1.0 + jnp.exp(-x)))


def _ret_in_kernel(x_ref, gain_ref, cos_ref, sin_ref, w_ref, o_ref, xn_ref):
    @pl.when(pl.program_id(1) == 0)
    def _():
        xn_ref[...] = _rmsnorm(x_ref[...], gain_ref[...]).astype(BF16)

    y = jnp.dot(xn_ref[...], w_ref[...], preferred_element_type=F32)
    cos = cos_ref[...]
    sin = sin_ref[...]
    half = RET_DK // 2
    for base, scale in ((0, 1.0), (RET_DK, RET_DK ** -0.5)):
        x1 = y[:, base:base + half]
        x2 = y[:, base + half:base + RET_DK]
        o_ref[:, base:base + half] = ((x1 * cos - x2 * sin) * scale).astype(BF16)
        o_ref[:, base + half:base + RET_DK] = ((x1 * sin + x2 * cos) * scale).astype(BF16)
    v_first = 2 * RET_DK
    g_first = v_first + RET_DV
    o_ref[:, v_first:g_first] = y[:, v_first:g_first].astype(BF16)
    o_ref[:, g_first:] = _silu(y[:, g_first:]).astype(BF16)


def _ret_in(x, gain, cos, sin, w, *, tm):
    m, d = x.shape
    n = w.shape[1]
    tn = RET_HEAD_COLS
    pos_tiles = cos.shape[0] // tm
    return pl.pallas_call(
        _ret_in_kernel,
        grid=(m // tm, n // tn),
        in_specs=[
            pl.BlockSpec((tm, d), lambda i, j: (i, 0)),
            pl.BlockSpec((1, d), lambda i, j: (0, 0)),
            pl.BlockSpec((tm, RET_DK // 2), lambda i, j: (i % pos_tiles, 0)),
            pl.BlockSpec((tm, RET_DK // 2), lambda i, j: (i % pos_tiles, 0)),
            pl.BlockSpec((d, tn), lambda i, j: (0, j)),
        ],
        out_specs=pl.BlockSpec((tm, tn), lambda i, j: (i, j)),
        out_shape=jax.ShapeDtypeStruct((m, n), BF16),
        scratch_shapes=[pltpu.VMEM((tm, d), BF16)],
        compiler_params=_params("parallel", "arbitrary"),
        name="ret_in_proj",
    )(x, gain, cos, sin, w)


def _retention_kernel(*refs, chunk, has_init):
    if has_init:
        p_ref, s0_ref, o_ref, s_ref, dm_ref, rd_ref, kw_ref, raw_ref = refs
    else:
        p_ref, o_ref, s_ref, dm_ref, rd_ref, kw_ref, raw_ref = refs
        s0_ref = None
    c = pl.program_id(2)
    heads = RET_HEADS_PER_STEP

    def log_gamma(hh):
        head = pl.program_id(1) * heads + hh
        return jnp.log(1.0 - jnp.exp2(-5.0 - jnp.full((1, 1), head, jnp.int32).astype(F32)))

    @pl.when(c == 0)
    def _():
        li = lax.broadcasted_iota(jnp.int32, (chunk, chunk), 0)
        mi = lax.broadcasted_iota(jnp.int32, (chunk, chunk), 1)
        diff = (li - mi).astype(F32)
        idx = lax.broadcasted_iota(jnp.int32, (chunk, 1), 0).astype(F32)
        for hh in range(heads):
            lg = log_gamma(hh)
            dm_ref[hh] = jnp.where(diff >= 0, jnp.exp(lg * jnp.maximum(diff, 0.0)), 0.0)
            rd_ref[hh] = jnp.exp(lg * (idx + 1.0))
            kw_ref[hh] = jnp.exp(lg * (chunk - 1.0 - idx))
        if has_init:
            s_ref[...] = s0_ref[...]
        else:
            s_ref[...] = jnp.zeros_like(s_ref)

    for hh in range(heads):
        base = hh * RET_HEAD_COLS
        q = p_ref[:, base:base + RET_DK]
        k = p_ref[:, base + RET_DK:base + 2 * RET_DK]
        v = p_ref[:, base + 2 * RET_DK:base + 2 * RET_DK + RET_DV]
        s_old = s_ref[0, hh]
        scores = lax.dot_general(q, k, (((1,), (1,)), ((), ())), preferred_element_type=F32) * dm_ref[hh]
        inner = jnp.dot(scores.astype(BF16), v, preferred_element_type=F32)
        cross = jnp.dot(q, s_old.astype(BF16), preferred_element_type=F32) * rd_ref[hh]
        raw_ref[hh] = inner + cross
        kd = (k.astype(F32) * kw_ref[hh]).astype(BF16)
        s_ref[0, hh] = jnp.exp(log_gamma(hh) * float(chunk)) * s_old + lax.dot_general(
            kd, v, (((0,), (0,)), ((), ())), preferred_element_type=F32)

    for hh in range(heads):
        o = raw_ref[hh]
        mu = jnp.mean(o, axis=-1, keepdims=True)
        cen = o - mu
        var = jnp.mean(cen * cen, axis=-1, keepdims=True)
        on = cen * lax.rsqrt(var + GN_EPS)
        gate = p_ref[:, (hh + 1) * RET_HEAD_COLS - RET_DV:(hh + 1) * RET_HEAD_COLS].astype(F32)
        o_ref[:, hh * RET_DV:(hh + 1) * RET_DV] = (on * gate).astype(BF16)


def _retention(proj, s0, *, batch, seq, chunk):
    nc = seq // chunk
    has_init = s0 is not None
    heads = RET_HEADS_PER_STEP
    v_w = heads * RET_DV
    in_specs = [pl.BlockSpec((chunk, heads * RET_HEAD_COLS), lambda b, h, c: (b * nc + c, h))]
    args = [proj]
    if has_init:
        in_specs.append(pl.BlockSpec((1, heads, RET_DK, RET_DV), lambda b, h, c: (b, h, 0, 0)))
        args.append(s0)
    return pl.pallas_call(
        functools.partial(_retention_kernel, chunk=chunk, has_init=has_init),
        grid=(batch, RET_HEADS // heads, nc),
        in_specs=in_specs,
        out_specs=[
            pl.BlockSpec((chunk, v_w), lambda b, h, c: (b * nc + c, h)),
            pl.BlockSpec((1, heads, RET_DK, RET_DV), lambda b, h, c: (b, h, 0, 0)),
        ],
        out_shape=[
            jax.ShapeDtypeStruct((batch * seq, RET_V), BF16),
            jax.ShapeDtypeStruct((batch, RET_HEADS, RET_DK, RET_DV), F32),
        ],
        scratch_shapes=[
            pltpu.VMEM((heads, chunk, chunk), F32),
            pltpu.VMEM((heads, chunk, 1), F32),
            pltpu.VMEM((heads, chunk, 1), F32),
            pltpu.VMEM((heads, chunk, RET_DV), F32),
        ],
        compiler_params=_params("parallel", "parallel", "arbitrary"),
        name="retention",
    )(*args)


def _mm_res_kernel(a_ref, w_ref, r_ref, o_ref):
    o_ref[...] = r_ref[...] + jnp.dot(a_ref[...], w_ref[...], preferred_element_type=F32)


def _mm_res(a, w, res, *, tm, tn):
    m, kdim = a.shape
    n = w.shape[1]
    return pl.pallas_call(
        _mm_res_kernel,
        grid=(m // tm, n // tn),
        in_specs=[
            pl.BlockSpec((tm, kdim), lambda i, j: (i, 0)),
            pl.BlockSpec((kdim, tn), lambda i, j: (0, j)),
            pl.BlockSpec((tm, tn), lambda i, j: (i, j)),
        ],
        out_specs=pl.BlockSpec((tm, tn), lambda i, j: (i, j)),
        out_shape=jax.ShapeDtypeStruct((m, n), F32),
        compiler_params=_params("parallel", "arbitrary"),
        name="matmul_residual",
    )(a, w, res)


def _ffn_kernel(*refs, final_norm):
    if final_norm:
        x_ref, gain_ref, wg_ref, wu_ref, wd_ref, fgain_ref, o_ref, xn_ref = refs
    else:
        x_ref, gain_ref, wg_ref, wu_ref, wd_ref, o_ref, xn_ref = refs
    j = pl.program_id(1)

    @pl.when(j == 0)
    def _():
        x = x_ref[...]
        xn_ref[...] = _rmsnorm(x, gain_ref[...]).astype(BF16)
        o_ref[...] = x

    xn = xn_ref[...]
    gate = jnp.dot(xn, wg_ref[...], preferred_element_type=F32)
    up = jnp.dot(xn, wu_ref[...], preferred_element_type=F32)
    hidden = (_silu(gate) * up).astype(BF16)
    o_ref[...] += jnp.dot(hidden, wd_ref[...], preferred_element_type=F32)

    if final_norm:
        @pl.when(j == pl.num_programs(1) - 1)
        def _():
            o_ref[...] = _rmsnorm(o_ref[...], fgain_ref[...])


def _ffn(x, gain, wg, wu, wd, final_gain=None, *, tm):
    m, d = x.shape
    f = wg.shape[1]
    tf = FFN_COL_TILE if tm >= FFN_ROW_TILE else FFN_COL_TILE_SMALL_ROWS
    final_norm = final_gain is not None
    in_specs = [
        pl.BlockSpec((tm, d), lambda i, j: (i, 0)),
        pl.BlockSpec((1, d), lambda i, j: (0, 0)),
        pl.BlockSpec((d, tf), lambda i, j: (0, j)),
        pl.BlockSpec((d, tf), lambda i, j: (0, j)),
        pl.BlockSpec((tf, d), lambda i, j: (j, 0)),
    ]
    args = [x, gain, wg, wu, wd]
    if final_norm:
        in_specs.append(pl.BlockSpec((1, d), lambda i, j: (0, 0)))
        args.append(final_gain)
    return pl.pallas_call(
        functools.partial(_ffn_kernel, final_norm=final_norm),
        grid=(m // tm, f // tf),
        in_specs=in_specs,
        out_specs=pl.BlockSpec((tm, d), lambda i, j: (i, 0)),
        out_shape=jax.ShapeDtypeStruct((m, d), F32),
        scratch_shapes=[pltpu.VMEM((tm, d), BF16)],
        compiler_params=_params("parallel", "arbitrary"),
        name="ffn",
    )(*args)


def _kv_kernel(x_ref, gain_ref, w_ref, bias_ref, k32_ref, v32_ref, kh_ref, vh_ref, lf_ref, lf16_ref, *, key_tile):
    xn = _rmsnorm(x_ref[...], gain_ref[...]).astype(BF16)
    y = jnp.dot(xn, w_ref[...], preferred_element_type=F32)
    k = y[:, :FOX_KV]
    v = y[:, FOX_KV:2 * FOX_KV]
    tm = k.shape[0]
    for h in range(FOX_KV_HEADS):
        hs = slice(h * FOX_HD, (h + 1) * FOX_HD)
        k32_ref[pl.ds(h, tm, stride=FOX_KV_HEADS), :] = k[:, hs]
        v32_ref[pl.ds(h, tm, stride=FOX_KV_HEADS), :] = v[:, hs]
        kh_ref[h] = k[:, hs].astype(BF16)
        if key_tile is None:
            vh_ref[h] = v[:, hs].astype(BF16)
        else:
            v_t = v[:, hs].T.astype(BF16)
            for t in range(tm // key_tile):
                vh_ref[h, t, :FOX_HD, :] = v_t[:, t * key_tile:(t + 1) * key_tile]
                vh_ref[h, t, FOX_HD:, :] = jnp.ones((FOX_V_ROWS - FOX_HD, key_tile), BF16)
    z = y[:, 2 * FOX_KV:] + bias_ref[...]
    lf = jnp.minimum(z, 0.0) - jnp.log1p(jnp.exp(-jnp.abs(z)))
    lf_ref[...] = lf
    lf16_ref[...] = lf[:, :FOX_HEADS]


def _kv_proj(x, gain, w, bias, *, tm, key_tile):
    m, d = x.shape
    n = w.shape[1]
    if key_tile is None:
        v_spec = pl.BlockSpec((FOX_KV_HEADS, tm, FOX_HD), lambda i: (0, i, 0))
        v_shape = jax.ShapeDtypeStruct((FOX_KV_HEADS, m, FOX_HD), BF16)
    else:
        v_spec = pl.BlockSpec((FOX_KV_HEADS, tm // key_tile, FOX_V_ROWS, key_tile), lambda i: (0, i, 0, 0))
        v_shape = jax.ShapeDtypeStruct((FOX_KV_HEADS, m // key_tile, FOX_V_ROWS, key_tile), BF16)
    return pl.pallas_call(
        functools.partial(_kv_kernel, key_tile=key_tile),
        grid=(m // tm,),
        in_specs=[
            pl.BlockSpec((tm, d), lambda i: (i, 0)),
            pl.BlockSpec((1, d), lambda i: (0, 0)),
            pl.BlockSpec((d, n), lambda i: (0, 0)),
            pl.BlockSpec((1, LANES), lambda i: (0, 0)),
        ],
        out_specs=[
            pl.BlockSpec((tm * FOX_KV_HEADS, FOX_HD), lambda i: (i, 0)),
            pl.BlockSpec((tm * FOX_KV_HEADS, FOX_HD), lambda i: (i, 0)),
            pl.BlockSpec((FOX_KV_HEADS, tm, FOX_HD), lambda i: (0, i, 0)),
            v_spec,
            pl.BlockSpec((tm, LANES), lambda i: (i, 0)),
            pl.BlockSpec((tm, FOX_HEADS), lambda i: (i, 0)),
        ],
        out_shape=[
            jax.ShapeDtypeStruct((m * FOX_KV_HEADS, FOX_HD), F32),
            jax.ShapeDtypeStruct((m * FOX_KV_HEADS, FOX_HD), F32),
            jax.ShapeDtypeStruct((FOX_KV_HEADS, m, FOX_HD), BF16),
            v_shape,
            jax.ShapeDtypeStruct((m, LANES), F32),
            jax.ShapeDtypeStruct((m, FOX_HEADS), F32),
        ],
        compiler_params=_params("parallel"),
        name="kv_proj",
    )(x, gain, w, bias)


def _q_kernel(x_ref, gain_ref, w_ref, o_ref, xn_ref):
    @pl.when(pl.program_id(1) == 0)
    def _():
        xn_ref[...] = _rmsnorm(x_ref[...], gain_ref[...]).astype(BF16)

    y = jnp.dot(xn_ref[...], w_ref[...], preferred_element_type=F32)
    o_ref[...] = (y * (FOX_HD ** -0.5 * LOG2_E)).astype(BF16)


def _q_proj(x, gain, w, *, tm):
    m, d = x.shape
    n = w.shape[1]
    tn = COL_TILE
    return pl.pallas_call(
        _q_kernel,
        grid=(m // tm, n // tn),
        in_specs=[
            pl.BlockSpec((tm, d), lambda i, j: (i, 0)),
            pl.BlockSpec((1, d), lambda i, j: (0, 0)),
            pl.BlockSpec((d, tn), lambda i, j: (0, j)),
        ],
        out_specs=pl.BlockSpec((tm, tn), lambda i, j: (i, j)),
        out_shape=jax.ShapeDtypeStruct((m, n), BF16),
        scratch_shapes=[pltpu.VMEM((tm, d), BF16)],
        compiler_params=_params("parallel", "arbitrary"),
        name="q_proj",
    )(x, gain, w)


def _cumsum_kernel(x_ref, o_ref, carry_ref):
    @pl.when(pl.program_id(1) == 0)
    def _():
        carry_ref[...] = jnp.zeros_like(carry_ref)

    x = x_ref[0]
    tl = x.shape[0]
    r = lax.broadcasted_iota(jnp.int32, (tl, tl), 0)
    c = lax.broadcasted_iota(jnp.int32, (tl, tl), 1)
    tri = jnp.where(c <= r, 1.0, 0.0).astype(BF16)
    x_hi = x.astype(BF16)
    x_rest = x - x_hi.astype(F32)
    x_mid = x_rest.astype(BF16)
    x_lo = (x_rest - x_mid.astype(F32)).astype(BF16)
    y = (jnp.dot(tri, x_lo, preferred_element_type=F32) + jnp.dot(tri, x_mid, preferred_element_type=F32)
         + jnp.dot(tri, x_hi, preferred_element_type=F32) + carry_ref[...])
    carry_ref[...] = y[tl - 1:tl, :]
    y2 = y * LOG2_E
    hi = y2.astype(BF16).astype(F32)
    rest = y2 - hi
    mid = rest.astype(BF16).astype(F32)
    lo = rest - mid
    lane = lax.broadcasted_iota(jnp.int32, y.shape, 1)
    packed = jnp.where(
        lane < FOX_HEADS, hi,
        jnp.where(lane < 2 * FOX_HEADS, pltpu.roll(mid, FOX_HEADS, 1),
                  jnp.where(lane < 3 * FOX_HEADS, pltpu.roll(lo, 2 * FOX_HEADS, 1), 0.0)))
    o_ref[0] = packed.astype(BF16)


def _cumsum_pieces(x, *, tl):
    b, length, lanes = x.shape
    return pl.pallas_call(
        _cumsum_kernel,
        grid=(b, length // tl),
        in_specs=[pl.BlockSpec((1, tl, lanes), lambda i, t: (i, t, 0))],
        out_specs=pl.BlockSpec((1, tl, lanes), lambda i, t: (i, t, 0)),
        out_shape=jax.ShapeDtypeStruct((b, length, lanes), BF16),
        scratch_shapes=[pltpu.VMEM((1, lanes), F32)],
        compiler_params=_params("parallel", "arbitrary"),
        name="logf_cumsum",
    )(x)


def _fox_kernel(q_ref, k_ref, f_ref, v_ref, o_ref, qa_ref, z0_ref, z1_ref, mb0_ref, mb1_ref, m_ref, acc_ref,
                *, tq, tk, q_start, blocks):
    step = pl.program_id(2)
    rows = FOX_GROUP * tq
    lane = lax.broadcasted_iota(jnp.int32, (tq, FOX_HD), 1)
    for x in range(blocks):
        for g in range(FOX_GROUP):
            head = pl.program_id(1) * FOX_GROUP + g
            pick = jnp.where(lane < FOX_F_PIECES * FOX_HEADS,
                             jnp.where((lane & (FOX_HEADS - 1)) == head, -1.0, 0.0), 0.0)
            qa_ref[x, g * tq:(g + 1) * tq, :FOX_HD] = q_ref[0, x * tq:(x + 1) * tq, g * FOX_HD:(g + 1) * FOX_HD]
            qa_ref[x, g * tq:(g + 1) * tq, FOX_HD:] = pick.astype(BF16)
    n_full = (q_start + step * blocks * tq) // tk

    m_ref[...] = jnp.full_like(m_ref, MASK_VALUE)
    acc_ref[...] = jnp.zeros_like(acc_ref)

    cw = min(rows, FOX_COL_CHUNK)
    chunks = [slice(c * cw, (c + 1) * cw) for c in range(rows // cw)]
    buffers = ((z0_ref, mb0_ref), (z1_ref, mb1_ref))

    def score(x, kb, masked, buf):
        z_ref, mb_ref = buffers[buf]
        start = pl.multiple_of(kb * tk, tk)
        ka = jnp.concatenate([k_ref[0, 0, pl.ds(start, tk), :], f_ref[0, pl.ds(start, tk), :]], axis=1)
        for c, cs in enumerate(chunks):
            z = lax.dot_general(ka, qa_ref[x, cs, :], (((1,), (1,)), ((), ())), preferred_element_type=F32)
            if masked:
                kpos = kb * tk + lax.broadcasted_iota(jnp.int32, (tk, cw), 0)
                col = c * cw + lax.broadcasted_iota(jnp.int32, (tk, cw), 1)
                qpos = q_start + (step * blocks + x) * tq + (col & (tq - 1))
                z = jnp.where(kpos <= qpos, z, MASK_VALUE)
            z_ref[:, cs] = z
            mb_ref[:, cs] = jnp.max(z, axis=0, keepdims=True)

    def accumulate(x, kb, buf):
        z_ref, mb_ref = buffers[buf]
        vt = v_ref[0, 0, kb]
        for cs in chunks:
            m_old = m_ref[x, :, cs]
            m_new = jnp.maximum(m_old, mb_ref[:, cs])
            alpha = jnp.exp2(m_old - m_new)
            p = jnp.exp2(z_ref[:, cs] - m_new)
            acc_ref[x, :, cs] = alpha * acc_ref[x, :, cs] + jnp.dot(vt, p.astype(BF16), preferred_element_type=F32)
            m_ref[x, :, cs] = m_new

    if blocks == 2:
        score(0, n_full, True, 0)
        score(1, n_full, True, 1)
        accumulate(0, n_full, 0)

        def tile(kb, pending):
            score(0, kb, False, 0)
            accumulate(1, pending, 1)
            score(1, kb, False, 1)
            accumulate(0, kb, 0)
            return kb

        pending = lax.fori_loop(0, n_full, tile, n_full)
        accumulate(1, pending, 1)
    else:
        score(0, n_full, True, 0)

        def pair(i, pending):
            score(0, 2 * i, False, 1)
            accumulate(0, pending, 0)
            score(0, 2 * i + 1, False, 0)
            accumulate(0, 2 * i, 1)
            return 2 * i + 1

        pending = lax.fori_loop(0, n_full // 2, pair, n_full)

        @pl.when(n_full % 2 == 1)
        def _():
            score(0, n_full - 1, False, 1)
            accumulate(0, pending, 0)
            accumulate(0, n_full - 1, 1)

        @pl.when(n_full % 2 == 0)
        def _():
            accumulate(0, pending, 0)

    for x in range(blocks):
        acc = acc_ref[x]
        out = (acc[:FOX_HD] / acc[FOX_HD:FOX_HD + 1]).T
        for g in range(FOX_GROUP):
            o_ref[0, x * tq:(x + 1) * tq, g * FOX_HD:(g + 1) * FOX_HD] = out[g * tq:(g + 1) * tq, :].astype(BF16)


def _fox_attention(q, k_heads, f_pieces, v_t, *, tq, tk, q_start):
    b, lq, _ = q.shape
    lk = k_heads.shape[2]
    nk = lk // tk
    rows = FOX_GROUP * tq
    blocks = 2 if (2 * tq == tk and lq % tk == 0) else 1
    assert tq & (tq - 1) == 0 and tk % tq == 0 and q_start % tk == 0
    return pl.pallas_call(
        functools.partial(_fox_kernel, tq=tq, tk=tk, q_start=q_start, blocks=blocks),
        grid=(b, FOX_KV_HEADS, lq // (blocks * tq)),
        in_specs=[
            pl.BlockSpec((1, blocks * tq, FOX_GROUP * FOX_HD), lambda i, h, t: (i, t, h)),
            pl.BlockSpec((1, 1, lk, FOX_HD), lambda i, h, t: (h, i, 0, 0)),
            pl.BlockSpec((1, lk, LANES), lambda i, h, t: (i, 0, 0)),
            pl.BlockSpec((1, 1, nk, FOX_V_ROWS, tk), lambda i, h, t: (h, i, 0, 0, 0)),
        ],
        out_specs=pl.BlockSpec((1, blocks * tq, FOX_GROUP * FOX_HD), lambda i, h, t: (i, t, h)),
        out_shape=jax.ShapeDtypeStruct(q.shape, BF16),
        scratch_shapes=[
            pltpu.VMEM((blocks, rows, 2 * FOX_HD), BF16),
            pltpu.VMEM((tk, rows), F32),
            pltpu.VMEM((tk, rows), F32),
            pltpu.VMEM((1, rows), F32),
            pltpu.VMEM((1, rows), F32),
            pltpu.VMEM((blocks, 1, rows), F32),
            pltpu.VMEM((blocks, FOX_V_ROWS, rows), F32),
        ],
        compiler_params=_params("parallel", "parallel", "arbitrary"),
        name="fox_attention",
    )(q, k_heads, f_pieces, v_t)


def _rope_tables(pos):
    half = RET_DK // 2
    inv = ROPE_BASE ** (-jnp.arange(half, dtype=F32) / half)
    ang = pos.astype(F32)[:, None] * inv[None, :]
    return jnp.cos(ang), jnp.sin(ang)


def _ret_in_head_major(w):
    d = w.shape[0]
    bounds = (0, RET_QK, 2 * RET_QK, 2 * RET_QK + RET_V, 2 * RET_QK + 2 * RET_V)
    parts = [w[:, lo:hi].reshape(d, RET_HEADS, (hi - lo) // RET_HEADS) for lo, hi in zip(bounds[:-1], bounds[1:])]
    return jnp.concatenate(parts, axis=2).reshape(d, RET_HEADS * RET_HEAD_COLS).astype(BF16)


def _with_cache(new_heads, cache, lk_pad):
    past = cache.transpose(2, 0, 1, 3).astype(BF16)
    both = jnp.concatenate([past, new_heads], axis=2)
    return jnp.pad(both, ((0, 0), (0, 0), (0, lk_pad - both.shape[2]), (0, 0)))


def _transpose_values(v_heads, tk):
    kvh, b, lk, _ = v_heads.shape
    v_t = v_heads.reshape(kvh, b, lk // tk, tk, FOX_HD).transpose(0, 1, 2, 4, 3)
    ones = jnp.ones((kvh, b, lk // tk, FOX_V_ROWS - FOX_HD, tk), BF16)
    return jnp.concatenate([v_t, ones], axis=3)


def _stream(x, *, pos, s0, cache, weights, tm, ffn_tm, chunk, tq):
    (norm_mix, norm_ffn, norm_kv, norm_final, w_ret_in, w_ret_o, w_kv, b_kv, w_fox_q, w_fox_o,
     w_ffn_gate, w_ffn_up, w_ffn_down) = weights
    b, length, d = x.shape
    m = b * length
    h = x.reshape(m, d)

    cos, sin = _rope_tables(pos)
    if cos.shape[0] < tm:
        cos = jnp.tile(cos, (tm // cos.shape[0], 1))
        sin = jnp.tile(sin, (tm // sin.shape[0], 1))
    proj = _ret_in(h, norm_mix[0:1], cos, sin, w_ret_in, tm=tm)
    o, s_fin = _retention(proj, s0, batch=b, seq=length, chunk=chunk)
    h = _mm_res(o, w_ret_o, h, tm=tm, tn=COL_TILE // 2)
    h = _ffn(h, norm_ffn[0:1], w_ffn_gate[0], w_ffn_up[0], w_ffn_down[0], tm=ffn_tm)

    tk = FOX_K_TILE
    if cache is None:
        q_start = 0
        k32, v32, k_heads, v_t, lf_pad, lf = _kv_proj(h, norm_kv[None, :], w_kv, b_kv, tm=tm, key_tile=tk)
        k_heads = k_heads.reshape(FOX_KV_HEADS, b, length, FOX_HD)
        v_t = v_t.reshape(FOX_KV_HEADS, b, length // tk, FOX_V_ROWS, tk)
        lf_all = lf_pad.reshape(b, length, LANES)
    else:
        cache_k, cache_v, cache_logf = cache
        q_start = cache_k.shape[1]
        lk_pad = -(-(q_start + length) // tk) * tk
        k32, v32, k_heads, v_heads, lf_pad, lf = _kv_proj(h, norm_kv[None, :], w_kv, b_kv, tm=tm, key_tile=None)
        k_heads = _with_cache(k_heads.reshape(FOX_KV_HEADS, b, length, FOX_HD), cache_k, lk_pad)
        v_t = _transpose_values(_with_cache(v_heads.reshape(FOX_KV_HEADS, b, length, FOX_HD), cache_v, lk_pad), tk)
        lf_all = jnp.concatenate([
            jnp.pad(cache_logf.astype(F32), ((0, 0), (0, 0), (0, LANES - FOX_HEADS))),
            lf_pad.reshape(b, length, LANES)], axis=1)
        lf_all = jnp.pad(lf_all, ((0, 0), (0, lk_pad - lf_all.shape[1]), (0, 0)))
    lf = lf.reshape(b, length, FOX_HEADS)
    f_pieces = _cumsum_pieces(lf_all, tl=CUMSUM_TILE)

    q = _q_proj(h, norm_mix[1:2], w_fox_q, tm=tm).reshape(b, length, d)
    att = _fox_attention(q, k_heads, f_pieces, v_t, tq=tq, tk=tk, q_start=q_start)
    h = _mm_res(att.reshape(m, d), w_fox_o, h, tm=tm, tn=COL_TILE // 2)
    y = _ffn(h, norm_ffn[1:2], w_ffn_gate[1], w_ffn_up[1], w_ffn_down[1], norm_final[None, :], tm=ffn_tm)

    return (y.reshape(b, length, d), s_fin[None],
            k32.reshape(b, length, FOX_KV_HEADS, FOX_HD), v32.reshape(b, length, FOX_KV_HEADS, FOX_HD), lf)


def kernel(x_prompt, x_sample, state_ret, cache_k, cache_v, cache_logf, norm_mix, norm_ffn, norm_kv, norm_final, w_ret_in, w_ret_o, w_kv_k, w_kv_v, w_kv_f, b_kv_f, w_fox_q, w_fox_o, w_ffn_gate, w_ffn_up, w_ffn_down):
    lp = x_prompt.shape[1]
    bs, ls, _ = x_sample.shape
    past = cache_k.shape[1]

    w_kv = jnp.concatenate(
        [w_kv_k, w_kv_v, jnp.pad(w_kv_f, ((0, 0), (0, LANES - FOX_HEADS)))], axis=1).astype(BF16)
    b_kv = jnp.pad(b_kv_f.astype(F32), (0, LANES - FOX_HEADS))[None, :]
    layers = range(w_ffn_gate.shape[0])
    weights = (norm_mix.astype(F32), norm_ffn.astype(F32), norm_kv.astype(F32), norm_final.astype(F32),
               _ret_in_head_major(w_ret_in[0]), w_ret_o[0].astype(BF16), w_kv, b_kv,
               w_fox_q[0].astype(BF16), w_fox_o[0].astype(BF16),
               [w_ffn_gate[i].astype(BF16) for i in layers], [w_ffn_up[i].astype(BF16) for i in layers],
               [w_ffn_down[i].astype(BF16) for i in layers])

    y_p, s_p, k_p, v_p, lf_p = _stream(
        x_prompt, pos=jnp.arange(lp), s0=None, cache=None, weights=weights,
        tm=ROW_TILE, ffn_tm=FFN_ROW_TILE, chunk=RET_CHUNK, tq=FOX_Q_TILE)
    y_s, s_s, k_s, v_s, lf_s = _stream(
        x_sample, pos=past + jnp.arange(ls), s0=state_ret[0], cache=(cache_k, cache_v, cache_logf),
        weights=weights, tm=bs * ls, ffn_tm=bs * ls, chunk=ls, tq=ls)
    return (y_p, y_s, s_p, k_p, v_p, lf_p, s_s, k_s, v_s, lf_s)
```

```python
import functools

import jax
import jax.numpy as jnp
from jax import lax
from jax.experimental import pallas as pl
from jax.experimental.pallas import tpu as pltpu

F32 = jnp.float32
BF16 = jnp.bfloat16

D_MODEL = 2048
RET_HEADS = 8
RET_DK = D_MODEL // RET_HEADS
RET_DV = 2 * D_MODEL // RET_HEADS
RET_QK = RET_HEADS * RET_DK
RET_V = RET_HEADS * RET_DV
RET_HEAD_COLS = 2 * RET_DK + 2 * RET_DV
ROPE_BASE = 10000.0
FOX_HEADS = 16
FOX_HD = D_MODEL // FOX_HEADS
FOX_KV_HEADS = 4
FOX_GROUP = FOX_HEADS // FOX_KV_HEADS
FOX_KV = FOX_KV_HEADS * FOX_HD
FOX_F_PIECES = 3
FOX_V_ROWS = FOX_HD + 16
LOG2_E = 1.4426950408889634
NORM_EPS = 1e-6
GN_EPS = 1e-5

LANES = 128
VMEM_LIMIT_BYTES = 56 * 1024 * 1024

ROW_TILE = 1024
COL_TILE = 1024
ROW_GROUPS = 2
FFN_ROW_TILE = 1024
FFN_COL_TILE = 256
FFN_COL_TILE_SMALL_ROWS = 512
RET_CHUNK = 256
RET_HEADS_PER_STEP = 4
FOX_Q_TILE = 256
FOX_K_TILE = 512
FOX_COL_CHUNK = 512
CUMSUM_TILE = 512
MASK_VALUE = -1e30


def _params(*semantics):
    return pltpu.CompilerParams(dimension_semantics=semantics, vmem_limit_bytes=VMEM_LIMIT_BYTES)


def _rmsnorm(x, gain):
    y = x * lax.rsqrt(jnp.mean(x * x, axis=-1, keepdims=True) + NORM_EPS)
    return y * gain


def _silu(x):
    return x * (1.0 / (1.0 + jnp.exp(-x)))


def _ret_in_kernel(x_ref, gain_ref, cos_ref, sin_ref, wq_ref, wk_ref, wv_ref, wg_ref, o_ref, xn_ref):
    @pl.when(pl.program_id(1) == 0)
    def _():
        xn_ref[...] = _rmsnorm(x_ref[...], gain_ref[...]).astype(BF16)

    xn = xn_ref[...]
    cos = cos_ref[...]
    sin = sin_ref[...]
    half = RET_DK // 2
    for w_ref, base, scale in ((wq_ref, 0, 1.0), (wk_ref, RET_DK, RET_DK ** -0.5)):
        y = jnp.dot(xn, w_ref[...], preferred_element_type=F32)
        x1 = y[:, :half]
        x2 = y[:, half:]
        o_ref[:, base:base + half] = ((x1 * cos - x2 * sin) * scale).astype(BF16)
        o_ref[:, base + half:base + RET_DK] = ((x1 * sin + x2 * cos) * scale).astype(BF16)
    v_first = 2 * RET_DK
    g_first = v_first + RET_DV
    o_ref[:, v_first:g_first] = jnp.dot(xn, wv_ref[...], preferred_element_type=F32).astype(BF16)
    o_ref[:, g_first:] = _silu(jnp.dot(xn, wg_ref[...], preferred_element_type=F32)).astype(BF16)


def _ret_in(x, gain, cos, sin, w, *, tm):
    m, d = x.shape
    pos_tiles = cos.shape[0] // tm
    k_first = RET_QK // RET_DK
    v_first = 2 * RET_QK // RET_DV
    g_first = (2 * RET_QK + RET_V) // RET_DV
    return pl.pallas_call(
        _ret_in_kernel,
        grid=(m // tm, RET_HEADS),
        in_specs=[
            pl.BlockSpec((tm, d), lambda i, j: (i, 0)),
            pl.BlockSpec((1, d), lambda i, j: (0, 0)),
            pl.BlockSpec((tm, RET_DK // 2), lambda i, j: (i % pos_tiles, 0)),
            pl.BlockSpec((tm, RET_DK // 2), lambda i, j: (i % pos_tiles, 0)),
            pl.BlockSpec((d, RET_DK), lambda i, j: (0, j)),
            pl.BlockSpec((d, RET_DK), lambda i, j: (0, k_first + j)),
            pl.BlockSpec((d, RET_DV), lambda i, j: (0, v_first + j)),
            pl.BlockSpec((d, RET_DV), lambda i, j: (0, g_first + j)),
        ],
        out_specs=pl.BlockSpec((tm, RET_HEAD_COLS), lambda i, j: (i, j)),
        out_shape=jax.ShapeDtypeStruct((m, RET_HEADS * RET_HEAD_COLS), BF16),
        scratch_shapes=[pltpu.VMEM((tm, d), BF16)],
        compiler_params=_params("parallel", "arbitrary"),
        name="ret_in_proj",
    )(x, gain, cos, sin, w, w, w, w)


def _retention_kernel(*refs, chunk, has_init):
    if has_init:
        p_ref, s0_ref, o_ref, s_ref, dm_ref, rd_ref, kw_ref, raw_ref = refs
    else:
        p_ref, o_ref, s_ref, dm_ref, rd_ref, kw_ref, raw_ref = refs
        s0_ref = None
    c = pl.program_id(2)
    heads = RET_HEADS_PER_STEP

    def log_gamma(hh):
        head = pl.program_id(1) * heads + hh
        return jnp.log(1.0 - jnp.exp2(-5.0 - jnp.full((1, 1), head, jnp.int32).astype(F32)))

    @pl.when(c == 0)
    def _():
        li = lax.broadcasted_iota(jnp.int32, (chunk, chunk), 0)
        mi = lax.broadcasted_iota(jnp.int32, (chunk, chunk), 1)
        diff = (li - mi).astype(F32)
        idx = lax.broadcasted_iota(jnp.int32, (chunk, 1), 0).astype(F32)
        for hh in range(heads):
            lg = log_gamma(hh)
            dm_ref[hh] = jnp.where(diff >= 0, jnp.exp(lg * jnp.maximum(diff, 0.0)), 0.0)
            rd_ref[hh] = jnp.exp(lg * (idx + 1.0))
            kw_ref[hh] = jnp.exp(lg * (chunk - 1.0 - idx))
        if has_init:
            s_ref[...] = s0_ref[...]
        else:
            s_ref[...] = jnp.zeros_like(s_ref)

    for hh in range(heads):
        base = hh * RET_HEAD_COLS
        q = p_ref[:, base:base + RET_DK]
        k = p_ref[:, base + RET_DK:base + 2 * RET_DK]
        v = p_ref[:, base + 2 * RET_DK:base + 2 * RET_DK + RET_DV]
        s_old = s_ref[0, hh]
        scores = lax.dot_general(q, k, (((1,), (1,)), ((), ())), preferred_element_type=F32) * dm_ref[hh]
        inner = jnp.dot(scores.astype(BF16), v, preferred_element_type=F32)
        cross = jnp.dot(q, s_old.astype(BF16), preferred_element_type=F32) * rd_ref[hh]
        raw_ref[hh] = inner + cross
        kd = (k.astype(F32) * kw_ref[hh]).astype(BF16)
        s_ref[0, hh] = jnp.exp(log_gamma(hh) * float(chunk)) * s_old + lax.dot_general(
            kd, v, (((0,), (0,)), ((), ())), preferred_element_type=F32)

    for hh in range(heads):
        o = raw_ref[hh]
        mu = jnp.mean(o, axis=-1, keepdims=True)
        cen = o - mu
        var = jnp.mean(cen * cen, axis=-1, keepdims=True)
        on = cen * lax.rsqrt(var + GN_EPS)
        gate = p_ref[:, (hh + 1) * RET_HEAD_COLS - RET_DV:(hh + 1) * RET_HEAD_COLS].astype(F32)
        o_ref[:, hh * RET_DV:(hh + 1) * RET_DV] = (on * gate).astype(BF16)


def _retention(proj, s0, *, batch, seq, chunk):
    nc = seq // chunk
    has_init = s0 is not None
    heads = RET_HEADS_PER_STEP
    v_w = heads * RET_DV
    in_specs = [pl.BlockSpec((chunk, heads * RET_HEAD_COLS), lambda b, h, c: (b * nc + c, h))]
    args = [proj]
    if has_init:
        in_specs.append(pl.BlockSpec((1, heads, RET_DK, RET_DV), lambda b, h, c: (b, h, 0, 0)))
        args.append(s0)
    return pl.pallas_call(
        functools.partial(_retention_kernel, chunk=chunk, has_init=has_init),
        grid=(batch, RET_HEADS // heads, nc),
        in_specs=in_specs,
        out_specs=[
            pl.BlockSpec((chunk, v_w), lambda b, h, c: (b * nc + c, h)),
            pl.BlockSpec((1, heads, RET_DK, RET_DV), lambda b, h, c: (b, h, 0, 0)),
        ],
        out_shape=[
            jax.ShapeDtypeStruct((batch * seq, RET_V), BF16),
            jax.ShapeDtypeStruct((batch, RET_HEADS, RET_DK, RET_DV), F32),
        ],
        scratch_shapes=[
            pltpu.VMEM((heads, chunk, chunk), F32),
            pltpu.VMEM((heads, chunk, 1), F32),
            pltpu.VMEM((heads, chunk, 1), F32),
            pltpu.VMEM((heads, chunk, RET_DV), F32),
        ],
        compiler_params=_params("parallel", "parallel", "arbitrary"),
        name="retention",
    )(*args)


def _mm_res_kernel(a_ref, w_ref, r_ref, o_ref):
    o_ref[...] = r_ref[...] + jnp.dot(a_ref[...], w_ref[...], preferred_element_type=F32)


def _mm_res(a, w, res, *, tm, tn):
    m, kdim = a.shape
    n = w.shape[1]
    return pl.pallas_call(
        _mm_res_kernel,
        grid=(m // tm, n // tn),
        in_specs=[
            pl.BlockSpec((tm, kdim), lambda i, j: (i, 0)),
            pl.BlockSpec((kdim, tn), lambda i, j: (0, j)),
            pl.BlockSpec((tm, tn), lambda i, j: (i, j)),
        ],
        out_specs=pl.BlockSpec((tm, tn), lambda i, j: (i, j)),
        out_shape=jax.ShapeDtypeStruct((m, n), F32),
        compiler_params=_params("parallel", "arbitrary"),
        name="matmul_residual",
    )(a, w, res)


def _ffn_kernel(*refs, final_norm):
    if final_norm:
        x_ref, gain_ref, wg_ref, wu_ref, wd_ref, fgain_ref, o_ref, xn_ref = refs
    else:
        x_ref, gain_ref, wg_ref, wu_ref, wd_ref, o_ref, xn_ref = refs
    j = pl.program_id(1)

    @pl.when(j == 0)
    def _():
        x = x_ref[...]
        xn_ref[...] = _rmsnorm(x, gain_ref[...]).astype(BF16)
        o_ref[...] = x

    xn = xn_ref[...]
    gate = jnp.dot(xn, wg_ref[...], preferred_element_type=F32)
    up = jnp.dot(xn, wu_ref[...], preferred_element_type=F32)
    hidden = (_silu(gate) * up).astype(BF16)
    o_ref[...] += jnp.dot(hidden, wd_ref[...], preferred_element_type=F32)

    if final_norm:
        @pl.when(j == pl.num_programs(1) - 1)
        def _():
            o_ref[...] = _rmsnorm(o_ref[...], fgain_ref[...])


def _ffn(x, gain, wg, wu, wd, final_gain=None, *, layer, tm):
    m, d = x.shape
    f = wg.shape[2]
    tf = FFN_COL_TILE if tm >= FFN_ROW_TILE else FFN_COL_TILE_SMALL_ROWS
    final_norm = final_gain is not None
    in_specs = [
        pl.BlockSpec((tm, d), lambda i, j: (i, 0)),
        pl.BlockSpec((1, d), lambda i, j: (0, 0)),
        pl.BlockSpec((None, d, tf), lambda i, j: (layer, 0, j)),
        pl.BlockSpec((None, d, tf), lambda i, j: (layer, 0, j)),
        pl.BlockSpec((None, tf, d), lambda i, j: (layer, j, 0)),
    ]
    args = [x, gain, wg, wu, wd]
    if final_norm:
        in_specs.append(pl.BlockSpec((1, d), lambda i, j: (0, 0)))
        args.append(final_gain)
    return pl.pallas_call(
        functools.partial(_ffn_kernel, final_norm=final_norm),
        grid=(m // tm, f // tf),
        in_specs=in_specs,
        out_specs=pl.BlockSpec((tm, d), lambda i, j: (i, 0)),
        out_shape=jax.ShapeDtypeStruct((m, d), F32),
        scratch_shapes=[pltpu.VMEM((tm, d), BF16)],
        compiler_params=_params("parallel", "arbitrary"),
        name="ffn",
    )(*args)


def _kv_kernel(x_ref, gain_ref, w_ref, bias_ref, k32_ref, v32_ref, kh_ref, vh_ref, lf_ref, lf16_ref, *, key_tile):
    rows = x_ref.shape[0] // ROW_GROUPS
    for r in range(ROW_GROUPS):
        rs = slice(r * rows, (r + 1) * rows)
        xn = _rmsnorm(x_ref[rs, :], gain_ref[...]).astype(BF16)
        y = jnp.dot(xn, w_ref[...], preferred_element_type=F32)
        k = y[:, :FOX_KV]
        v = y[:, FOX_KV:2 * FOX_KV]
        for h in range(FOX_KV_HEADS):
            hs = slice(h * FOX_HD, (h + 1) * FOX_HD)
            out_rows = pl.ds(r * rows * FOX_KV_HEADS + h, rows, stride=FOX_KV_HEADS)
            k32_ref[out_rows, :] = k[:, hs]
            v32_ref[out_rows, :] = v[:, hs]
            kh_ref[h, rs, :] = k[:, hs].astype(BF16)
            if key_tile is None:
                vh_ref[h, rs, :] = v[:, hs].astype(BF16)
            else:
                v_t = v[:, hs].T.astype(BF16)
                for t in range(rows // key_tile):
                    tile = r * (rows // key_tile) + t
                    vh_ref[h, tile, :FOX_HD, :] = v_t[:, t * key_tile:(t + 1) * key_tile]
                    vh_ref[h, tile, FOX_HD:, :] = jnp.ones((FOX_V_ROWS - FOX_HD, key_tile), BF16)
        z = y[:, 2 * FOX_KV:] + bias_ref[...]
        lf = jnp.minimum(z, 0.0) - jnp.log1p(jnp.exp(-jnp.abs(z)))
        lf_ref[rs, :] = lf
        lf16_ref[rs, :] = lf[:, :FOX_HEADS]


def _kv_proj(x, gain, w, bias, *, tm, key_tile):
    m, d = x.shape
    n = w.shape[1]
    if key_tile is None:
        v_spec = pl.BlockSpec((FOX_KV_HEADS, tm, FOX_HD), lambda i: (0, i, 0))
        v_shape = jax.ShapeDtypeStruct((FOX_KV_HEADS, m, FOX_HD), BF16)
    else:
        v_spec = pl.BlockSpec((FOX_KV_HEADS, tm // key_tile, FOX_V_ROWS, key_tile), lambda i: (0, i, 0, 0))
        v_shape = jax.ShapeDtypeStruct((FOX_KV_HEADS, m // key_tile, FOX_V_ROWS, key_tile), BF16)
    return pl.pallas_call(
        functools.partial(_kv_kernel, key_tile=key_tile),
        grid=(m // tm,),
        in_specs=[
            pl.BlockSpec((tm, d), lambda i: (i, 0)),
            pl.BlockSpec((1, d), lambda i: (0, 0)),
            pl.BlockSpec((d, n), lambda i: (0, 0)),
            pl.BlockSpec((1, LANES), lambda i: (0, 0)),
        ],
        out_specs=[
            pl.BlockSpec((tm * FOX_KV_HEADS, FOX_HD), lambda i: (i, 0)),
            pl.BlockSpec((tm * FOX_KV_HEADS, FOX_HD), lambda i: (i, 0)),
            pl.BlockSpec((FOX_KV_HEADS, tm, FOX_HD), lambda i: (0, i, 0)),
            v_spec,
            pl.BlockSpec((tm, LANES), lambda i: (i, 0)),
            pl.BlockSpec((tm, FOX_HEADS), lambda i: (i, 0)),
        ],
        out_shape=[
            jax.ShapeDtypeStruct((m * FOX_KV_HEADS, FOX_HD), F32),
            jax.ShapeDtypeStruct((m * FOX_KV_HEADS, FOX_HD), F32),
            jax.ShapeDtypeStruct((FOX_KV_HEADS, m, FOX_HD), BF16),
            v_shape,
            jax.ShapeDtypeStruct((m, LANES), F32),
            jax.ShapeDtypeStruct((m, FOX_HEADS), F32),
        ],
        compiler_params=_params("parallel"),
        name="kv_proj",
    )(x, gain, w, bias)


def _q_kernel(x_ref, gain_ref, w_ref, o_ref):
    rows = x_ref.shape[0] // ROW_GROUPS
    for r in range(ROW_GROUPS):
        rs = slice(r * rows, (r + 1) * rows)
        xn = _rmsnorm(x_ref[rs, :], gain_ref[...]).astype(BF16)
        y = jnp.dot(xn, w_ref[...], preferred_element_type=F32)
        o_ref[rs, :] = (y * (FOX_HD ** -0.5 * LOG2_E)).astype(BF16)


def _q_proj(x, gain, w, *, tm):
    m, d = x.shape
    n = w.shape[1]
    return pl.pallas_call(
        _q_kernel,
        grid=(m // tm,),
        in_specs=[
            pl.BlockSpec((tm, d), lambda i: (i, 0)),
            pl.BlockSpec((1, d), lambda i: (0, 0)),
            pl.BlockSpec((d, n), lambda i: (0, 0)),
        ],
        out_specs=pl.BlockSpec((tm, n), lambda i: (i, 0)),
        out_shape=jax.ShapeDtypeStruct((m, n), BF16),
        compiler_params=_params("parallel"),
        name="q_proj",
    )(x, gain, w)


def _cumsum_kernel(x_ref, o_ref, carry_ref):
    @pl.when(pl.program_id(1) == 0)
    def _():
        carry_ref[...] = jnp.zeros_like(carry_ref)

    x = x_ref[0]
    tl = x.shape[0]
    r = lax.broadcasted_iota(jnp.int32, (tl, tl), 0)
    c = lax.broadcasted_iota(jnp.int32, (tl, tl), 1)
    tri = jnp.where(c <= r, 1.0, 0.0).astype(BF16)
    x_hi = x.astype(BF16)
    x_rest = x - x_hi.astype(F32)
    x_mid = x_rest.astype(BF16)
    x_lo = (x_rest - x_mid.astype(F32)).astype(BF16)
    y = (jnp.dot(tri, x_lo, preferred_element_type=F32) + jnp.dot(tri, x_mid, preferred_element_type=F32)
         + jnp.dot(tri, x_hi, preferred_element_type=F32) + carry_ref[...])
    carry_ref[...] = y[tl - 1:tl, :]
    y2 = y * LOG2_E
    hi = y2.astype(BF16).astype(F32)
    rest = y2 - hi
    mid = rest.astype(BF16).astype(F32)
    lo = rest - mid
    lane = lax.broadcasted_iota(jnp.int32, y.shape, 1)
    packed = jnp.where(
        lane < FOX_HEADS, hi,
        jnp.where(lane < 2 * FOX_HEADS, pltpu.roll(mid, FOX_HEADS, 1),
                  jnp.where(lane < 3 * FOX_HEADS, pltpu.roll(lo, 2 * FOX_HEADS, 1), 0.0)))
    o_ref[0] = packed.astype(BF16)


def _cumsum_pieces(x, *, tl):
    b, length, lanes = x.shape
    return pl.pallas_call(
        _cumsum_kernel,
        grid=(b, length // tl),
        in_specs=[pl.BlockSpec((1, tl, lanes), lambda i, t: (i, t, 0))],
        out_specs=pl.BlockSpec((1, tl, lanes), lambda i, t: (i, t, 0)),
        out_shape=jax.ShapeDtypeStruct((b, length, lanes), BF16),
        scratch_shapes=[pltpu.VMEM((1, lanes), F32)],
        compiler_params=_params("parallel", "arbitrary"),
        name="logf_cumsum",
    )(x)


def _fox_kernel(q_ref, k_ref, f_ref, v_ref, o_ref, qa_ref, z0_ref, z1_ref, mb0_ref, mb1_ref, m_ref, acc_ref,
                *, tq, tk, q_start, blocks):
    step = pl.program_id(2)
    rows = FOX_GROUP * tq
    lane = lax.broadcasted_iota(jnp.int32, (tq, FOX_HD), 1)
    for x in range(blocks):
        for g in range(FOX_GROUP):
            head = pl.program_id(1) * FOX_GROUP + g
            pick = jnp.where(lane < FOX_F_PIECES * FOX_HEADS,
                             jnp.where((lane & (FOX_HEADS - 1)) == head, -1.0, 0.0), 0.0)
            qa_ref[x, g * tq:(g + 1) * tq, :FOX_HD] = q_ref[0, x * tq:(x + 1) * tq, g * FOX_HD:(g + 1) * FOX_HD]
            qa_ref[x, g * tq:(g + 1) * tq, FOX_HD:] = pick.astype(BF16)
    n_full = (q_start + step * blocks * tq) // tk

    m_ref[...] = jnp.full_like(m_ref, MASK_VALUE)
    acc_ref[...] = jnp.zeros_like(acc_ref)

    cw = min(rows, FOX_COL_CHUNK)
    chunks = [slice(c * cw, (c + 1) * cw) for c in range(rows // cw)]
    buffers = ((z0_ref, mb0_ref), (z1_ref, mb1_ref))

    def score(x, kb, masked, buf):
        z_ref, mb_ref = buffers[buf]
        start = pl.multiple_of(kb * tk, tk)
        ka = jnp.concatenate([k_ref[0, 0, pl.ds(start, tk), :], f_ref[0, pl.ds(start, tk), :]], axis=1)
        for c, cs in enumerate(chunks):
            z = lax.dot_general(ka, qa_ref[x, cs, :], (((1,), (1,)), ((), ())), preferred_element_type=F32)
            if masked:
                kpos = kb * tk + lax.broadcasted_iota(jnp.int32, (tk, cw), 0)
                col = c * cw + lax.broadcasted_iota(jnp.int32, (tk, cw), 1)
                qpos = q_start + (step * blocks + x) * tq + (col & (tq - 1))
                z = jnp.where(kpos <= qpos, z, MASK_VALUE)
            z_ref[:, cs] = z
            mb_ref[:, cs] = jnp.max(z, axis=0, keepdims=True)

    def accumulate(x, kb, buf):
        z_ref, mb_ref = buffers[buf]
        vt = v_ref[0, 0, kb]
        for cs in chunks:
            m_old = m_ref[x, :, cs]
            m_new = jnp.maximum(m_old, mb_ref[:, cs])
            alpha = jnp.exp2(m_old - m_new)
            p = jnp.exp2(z_ref[:, cs] - m_new)
            acc_ref[x, :, cs] = alpha * acc_ref[x, :, cs] + jnp.dot(vt, p.astype(BF16), preferred_element_type=F32)
            m_ref[x, :, cs] = m_new

    if blocks == 2:
        score(0, n_full, True, 0)
        score(1, n_full, True, 1)
        accumulate(0, n_full, 0)

        def tile(kb, pending):
            score(0, kb, False, 0)
            accumulate(1, pending, 1)
            score(1, kb, False, 1)
            accumulate(0, kb, 0)
            return kb

        pending = lax.fori_loop(0, n_full, tile, n_full)
        accumulate(1, pending, 1)
    else:
        score(0, n_full, True, 0)

        def pair(i, pending):
            score(0, 2 * i, False, 1)
            accumulate(0, pending, 0)
            score(0, 2 * i + 1, False, 0)
            accumulate(0, 2 * i, 1)
            return 2 * i + 1

        pending = lax.fori_loop(0, n_full // 2, pair, n_full)

        @pl.when(n_full % 2 == 1)
        def _():
            score(0, n_full - 1, False, 1)
            accumulate(0, pending, 0)
            accumulate(0, n_full - 1, 1)

        @pl.when(n_full % 2 == 0)
        def _():
            accumulate(0, pending, 0)

    for x in range(blocks):
        acc = acc_ref[x]
        out = (acc[:FOX_HD] / acc[FOX_HD:FOX_HD + 1]).T
        for g in range(FOX_GROUP):
            o_ref[0, x * tq:(x + 1) * tq, g * FOX_HD:(g + 1) * FOX_HD] = out[g * tq:(g + 1) * tq, :].astype(BF16)


def _fox_attention(q, k_heads, f_pieces, v_t, *, tq, tk, q_start):
    b, lq, _ = q.shape
    lk = k_heads.shape[2]
    nk = lk // tk
    rows = FOX_GROUP * tq
    blocks = 2 if (2 * tq == tk and lq % tk == 0) else 1
    assert tq & (tq - 1) == 0 and tk % tq == 0 and q_start % tk == 0
    return pl.pallas_call(
        functools.partial(_fox_kernel, tq=tq, tk=tk, q_start=q_start, blocks=blocks),
        grid=(b, FOX_KV_HEADS, lq // (blocks * tq)),
        in_specs=[
            pl.BlockSpec((1, blocks * tq, FOX_GROUP * FOX_HD), lambda i, h, t: (i, t, h)),
            pl.BlockSpec((1, 1, lk, FOX_HD), lambda i, h, t: (h, i, 0, 0)),
            pl.BlockSpec((1, lk, LANES), lambda i, h, t: (i, 0, 0)),
            pl.BlockSpec((1, 1, nk, FOX_V_ROWS, tk), lambda i, h, t: (h, i, 0, 0, 0)),
        ],
        out_specs=pl.BlockSpec((1, blocks * tq, FOX_GROUP * FOX_HD), lambda i, h, t: (i, t, h)),
        out_shape=jax.ShapeDtypeStruct(q.shape, BF16),
        scratch_shapes=[
            pltpu.VMEM((blocks, rows, 2 * FOX_HD), BF16),
            pltpu.VMEM((tk, rows), F32),
            pltpu.VMEM((tk, rows), F32),
            pltpu.VMEM((1, rows), F32),
            pltpu.VMEM((1, rows), F32),
            pltpu.VMEM((blocks, 1, rows), F32),
            pltpu.VMEM((blocks, FOX_V_ROWS, rows), F32),
        ],
        compiler_params=_params("parallel", "parallel", "arbitrary"),
        name="fox_attention",
    )(q, k_heads, f_pieces, v_t)


def _rope_tables(pos):
    half = RET_DK // 2
    inv = ROPE_BASE ** (-jnp.arange(half, dtype=F32) / half)
    ang = pos.astype(F32)[:, None] * inv[None, :]
    return jnp.cos(ang), jnp.sin(ang)


def _with_cache(new_heads, cache, lk_pad):
    past = cache.transpose(2, 0, 1, 3).astype(BF16)
    both = jnp.concatenate([past, new_heads], axis=2)
    return jnp.pad(both, ((0, 0), (0, 0), (0, lk_pad - both.shape[2]), (0, 0)))


def _transpose_values(v_heads, tk):
    kvh, b, lk, _ = v_heads.shape
    v_t = v_heads.reshape(kvh, b, lk // tk, tk, FOX_HD).transpose(0, 1, 2, 4, 3)
    ones = jnp.ones((kvh, b, lk // tk, FOX_V_ROWS - FOX_HD, tk), BF16)
    return jnp.concatenate([v_t, ones], axis=3)


def _stream(x, *, pos, s0, cache, weights, tm, ffn_tm, chunk, tq):
    (norm_mix, norm_ffn, norm_kv, norm_final, w_ret_in, w_ret_o, w_kv, b_kv, w_fox_q, w_fox_o,
     w_ffn_gate, w_ffn_up, w_ffn_down) = weights
    b, length, d = x.shape
    m = b * length
    h = x.reshape(m, d)

    cos, sin = _rope_tables(pos)
    if cos.shape[0] < tm:
        cos = jnp.tile(cos, (tm // cos.shape[0], 1))
        sin = jnp.tile(sin, (tm // sin.shape[0], 1))
    proj = _ret_in(h, norm_mix[0:1], cos, sin, w_ret_in, tm=tm)
    o, s_fin = _retention(proj, s0, batch=b, seq=length, chunk=chunk)
    h = _mm_res(o, w_ret_o, h, tm=tm, tn=COL_TILE // 2)
    h = _ffn(h, norm_ffn[0:1], w_ffn_gate, w_ffn_up, w_ffn_down, layer=0, tm=ffn_tm)

    tk = FOX_K_TILE
    if cache is None:
        q_start = 0
        k32, v32, k_heads, v_t, lf_pad, lf = _kv_proj(h, norm_kv[None, :], w_kv, b_kv, tm=tm, key_tile=tk)
        k_heads = k_heads.reshape(FOX_KV_HEADS, b, length, FOX_HD)
        v_t = v_t.reshape(FOX_KV_HEADS, b, length // tk, FOX_V_ROWS, tk)
        lf_all = lf_pad.reshape(b, length, LANES)
    else:
        cache_k, cache_v, cache_logf = cache
        q_start = cache_k.shape[1]
        lk_pad = -(-(q_start + length) // tk) * tk
        k32, v32, k_heads, v_heads, lf_pad, lf = _kv_proj(h, norm_kv[None, :], w_kv, b_kv, tm=tm, key_tile=None)
        k_heads = _with_cache(k_heads.reshape(FOX_KV_HEADS, b, length, FOX_HD), cache_k, lk_pad)
        v_t = _transpose_values(_with_cache(v_heads.reshape(FOX_KV_HEADS, b, length, FOX_HD), cache_v, lk_pad), tk)
        lf_all = jnp.concatenate([
            jnp.pad(cache_logf.astype(F32), ((0, 0), (0, 0), (0, LANES - FOX_HEADS))),
            lf_pad.reshape(b, length, LANES)], axis=1)
        lf_all = jnp.pad(lf_all, ((0, 0), (0, lk_pad - lf_all.shape[1]), (0, 0)))
    lf = lf.reshape(b, length, FOX_HEADS)
    f_pieces = _cumsum_pieces(lf_all, tl=CUMSUM_TILE)

    q = _q_proj(h, norm_mix[1:2], w_fox_q, tm=tm).reshape(b, length, d)
    att = _fox_attention(q, k_heads, f_pieces, v_t, tq=tq, tk=tk, q_start=q_start)
    h = _mm_res(att.reshape(m, d), w_fox_o, h, tm=min(tm, ROW_TILE // 2), tn=d)
    y = _ffn(h, norm_ffn[1:2], w_ffn_gate, w_ffn_up, w_ffn_down, norm_final[None, :], layer=1, tm=ffn_tm)

    return (y.reshape(b, length, d), s_fin[None],
            k32.reshape(b, length, FOX_KV_HEADS, FOX_HD), v32.reshape(b, length, FOX_KV_HEADS, FOX_HD), lf)


def kernel(x_prompt, x_sample, state_ret, cache_k, cache_v, cache_logf, norm_mix, norm_ffn, norm_kv, norm_final, w_ret_in, w_ret_o, w_kv_k, w_kv_v, w_kv_f, b_kv_f, w_fox_q, w_fox_o, w_ffn_gate, w_ffn_up, w_ffn_down):
    lp = x_prompt.shape[1]
    bs, ls, _ = x_sample.shape
    past = cache_k.shape[1]

    w_kv = jnp.concatenate(
        [w_kv_k, w_kv_v, jnp.pad(w_kv_f, ((0, 0), (0, LANES - FOX_HEADS)))], axis=1).astype(BF16)
    b_kv = jnp.pad(b_kv_f.astype(F32), (0, LANES - FOX_HEADS))[None, :]
    weights = (norm_mix.astype(F32), norm_ffn.astype(F32), norm_kv.astype(F32), norm_final.astype(F32),
               w_ret_in[0].astype(BF16), w_ret_o[0].astype(BF16), w_kv, b_kv,
               w_fox_q[0].astype(BF16), w_fox_o[0].astype(BF16),
               w_ffn_gate.astype(BF16), w_ffn_up.astype(BF16), w_ffn_down.astype(BF16))

    y_p, s_p, k_p, v_p, lf_p = _stream(
        x_prompt, pos=jnp.arange(lp), s0=None, cache=None, weights=weights,
        tm=ROW_TILE, ffn_tm=FFN_ROW_TILE, chunk=RET_CHUNK, tq=FOX_Q_TILE)
    y_s, s_s, k_s, v_s, lf_s = _stream(
        x_sample, pos=past + jnp.arange(ls), s0=state_ret[0], cache=(cache_k, cache_v, cache_logf),
        weights=weights, tm=bs * ls, ffn_tm=bs * ls, chunk=ls, tq=ls)
    return (y_p, y_s, s_p, k_p, v_p, lf_p, s_s, k_s, v_s, lf_s)
```

```python
import functools

import jax
import jax.numpy as jnp
from jax import lax
from jax.experimental import pallas as pl
from jax.experimental.pallas import tpu as pltpu

F32 = jnp.float32
BF16 = jnp.bfloat16

D_MODEL = 2048
RET_HEADS = 8
RET_DK = D_MODEL // RET_HEADS
RET_DV = 2 * D_MODEL // RET_HEADS
RET_QK = RET_HEADS * RET_DK
RET_V = RET_HEADS * RET_DV
RET_HEAD_COLS = 2 * RET_DK + 2 * RET_DV
ROPE_BASE = 10000.0
FOX_HEADS = 16
FOX_HD = D_MODEL // FOX_HEADS
FOX_KV_HEADS = 4
FOX_GROUP = FOX_HEADS // FOX_KV_HEADS
FOX_KV = FOX_KV_HEADS * FOX_HD
FOX_F_PIECES = 3
FOX_V_ROWS = FOX_HD + 16
LOG2_E = 1.4426950408889634
NORM_EPS = 1e-6
GN_EPS = 1e-5

LANES = 128
VMEM_LIMIT_BYTES = 56 * 1024 * 1024

ROW_TILE = 1024
COL_TILE = 1024
ROW_GROUPS = 2
FFN_ROW_TILE = 1024
FFN_COL_TILE = 512
FFN_COL_TILE_SMALL_ROWS = 512
RET_CHUNK = 256
RET_HEADS_PER_STEP = 4
FOX_Q_TILE = 256
FOX_K_TILE = 512
FOX_COL_CHUNK = 512
CUMSUM_TILE = 512
MASK_VALUE = -1e30


def _params(*semantics):
    return pltpu.CompilerParams(dimension_semantics=semantics, vmem_limit_bytes=VMEM_LIMIT_BYTES)


def _rmsnorm(x, gain):
    y = x * lax.rsqrt(jnp.mean(x * x, axis=-1, keepdims=True) + NORM_EPS)
    return y * gain


def _silu(x):
    return x * (1.0 / (1.0 + jnp.exp(-x)))


def _ret_in_kernel(x_ref, gain_ref, cos_ref, sin_ref, wq_ref, wk_ref, wv_ref, wg_ref, o_ref, xn_ref):
    @pl.when(pl.program_id(1) == 0)
    def _():
        xn_ref[...] = _rmsnorm(x_ref[...], gain_ref[...]).astype(BF16)

    xn = xn_ref[...]
    cos = cos_ref[...]
    sin = sin_ref[...]
    half = RET_DK // 2
    for w_ref, base, scale in ((wq_ref, 0, 1.0), (wk_ref, RET_DK, RET_DK ** -0.5)):
        y = jnp.dot(xn, w_ref[...], preferred_element_type=F32)
        x1 = y[:, :half]
        x2 = y[:, half:]
        o_ref[:, base:base + half] = ((x1 * cos - x2 * sin) * scale).astype(BF16)
        o_ref[:, base + half:base + RET_DK] = ((x1 * sin + x2 * cos) * scale).astype(BF16)
    v_first = 2 * RET_DK
    g_first = v_first + RET_DV
    o_ref[:, v_first:g_first] = jnp.dot(xn, wv_ref[...], preferred_element_type=F32).astype(BF16)
    o_ref[:, g_first:] = _silu(jnp.dot(xn, wg_ref[...], preferred_element_type=F32)).astype(BF16)


def _ret_in(x, gain, cos, sin, w, *, tm):
    m, d = x.shape
    pos_tiles = cos.shape[0] // tm
    k_first = RET_QK // RET_DK
    v_first = 2 * RET_QK // RET_DV
    g_first = (2 * RET_QK + RET_V) // RET_DV
    return pl.pallas_call(
        _ret_in_kernel,
        grid=(m // tm, RET_HEADS),
        in_specs=[
            pl.BlockSpec((tm, d), lambda i, j: (i, 0)),
            pl.BlockSpec((1, d), lambda i, j: (0, 0)),
            pl.BlockSpec((tm, RET_DK // 2), lambda i, j: (i % pos_tiles, 0)),
            pl.BlockSpec((tm, RET_DK // 2), lambda i, j: (i % pos_tiles, 0)),
            pl.BlockSpec((d, RET_DK), lambda i, j: (0, j)),
            pl.BlockSpec((d, RET_DK), lambda i, j: (0, k_first + j)),
            pl.BlockSpec((d, RET_DV), lambda i, j: (0, v_first + j)),
            pl.BlockSpec((d, RET_DV), lambda i, j: (0, g_first + j)),
        ],
        out_specs=pl.BlockSpec((tm, RET_HEAD_COLS), lambda i, j: (i, j)),
        out_shape=jax.ShapeDtypeStruct((m, RET_HEADS * RET_HEAD_COLS), BF16),
        scratch_shapes=[pltpu.VMEM((tm, d), BF16)],
        compiler_params=_params("parallel", "arbitrary"),
        name="ret_in_proj",
    )(x, gain, cos, sin, w, w, w, w)


def _retention_kernel(*refs, chunk, has_init):
    if has_init:
        p_ref, s0_ref, o_ref, s_ref, dm_ref, rd_ref, kw_ref, raw_ref = refs
    else:
        p_ref, o_ref, s_ref, dm_ref, rd_ref, kw_ref, raw_ref = refs
        s0_ref = None
    c = pl.program_id(2)
    heads = RET_HEADS_PER_STEP

    def log_gamma(hh):
        head = pl.program_id(1) * heads + hh
        return jnp.log(1.0 - jnp.exp2(-5.0 - jnp.full((1, 1), head, jnp.int32).astype(F32)))

    @pl.when(c == 0)
    def _():
        li = lax.broadcasted_iota(jnp.int32, (chunk, chunk), 0)
        mi = lax.broadcasted_iota(jnp.int32, (chunk, chunk), 1)
        diff = (li - mi).astype(F32)
        idx = lax.broadcasted_iota(jnp.int32, (chunk, 1), 0).astype(F32)
        for hh in range(heads):
            lg = log_gamma(hh)
            dm_ref[hh] = jnp.where(diff >= 0, jnp.exp(lg * jnp.maximum(diff, 0.0)), 0.0)
            rd_ref[hh] = jnp.exp(lg * (idx + 1.0))
            kw_ref[hh] = jnp.exp(lg * (chunk - 1.0 - idx))
        if has_init:
            s_ref[...] = s0_ref[...]
        else:
            s_ref[...] = jnp.zeros_like(s_ref)

    for hh in range(heads):
        base = hh * RET_HEAD_COLS
        q = p_ref[:, base:base + RET_DK]
        k = p_ref[:, base + RET_DK:base + 2 * RET_DK]
        v = p_ref[:, base + 2 * RET_DK:base + 2 * RET_DK + RET_DV]
        s_old = s_ref[0, hh]
        scores = lax.dot_general(q, k, (((1,), (1,)), ((), ())), preferred_element_type=F32) * dm_ref[hh]
        inner = jnp.dot(scores.astype(BF16), v, preferred_element_type=F32)
        cross = jnp.dot(q, s_old.astype(BF16), preferred_element_type=F32) * rd_ref[hh]
        raw_ref[hh] = inner + cross
        kd = (k.astype(F32) * kw_ref[hh]).astype(BF16)
        s_ref[0, hh] = jnp.exp(log_gamma(hh) * float(chunk)) * s_old + lax.dot_general(
            kd, v, (((0,), (0,)), ((), ())), preferred_element_type=F32)

    for hh in range(heads):
        o = raw_ref[hh]
        mu = jnp.mean(o, axis=-1, keepdims=True)
        cen = o - mu
        var = jnp.mean(cen * cen, axis=-1, keepdims=True)
        on = cen * lax.rsqrt(var + GN_EPS)
        gate = p_ref[:, (hh + 1) * RET_HEAD_COLS - RET_DV:(hh + 1) * RET_HEAD_COLS].astype(F32)
        o_ref[:, hh * RET_DV:(hh + 1) * RET_DV] = (on * gate).astype(BF16)


def _retention(proj, s0, *, batch, seq, chunk):
    nc = seq // chunk
    has_init = s0 is not None
    heads = RET_HEADS_PER_STEP
    v_w = heads * RET_DV
    in_specs = [pl.BlockSpec((chunk, heads * RET_HEAD_COLS), lambda b, h, c: (b * nc + c, h))]
    args = [proj]
    if has_init:
        in_specs.append(pl.BlockSpec((1, heads, RET_DK, RET_DV), lambda b, h, c: (b, h, 0, 0)))
        args.append(s0)
    return pl.pallas_call(
        functools.partial(_retention_kernel, chunk=chunk, has_init=has_init),
        grid=(batch, RET_HEADS // heads, nc),
        in_specs=in_specs,
        out_specs=[
            pl.BlockSpec((chunk, v_w), lambda b, h, c: (b * nc + c, h)),
            pl.BlockSpec((1, heads, RET_DK, RET_DV), lambda b, h, c: (b, h, 0, 0)),
        ],
        out_shape=[
            jax.ShapeDtypeStruct((batch * seq, RET_V), BF16),
            jax.ShapeDtypeStruct((batch, RET_HEADS, RET_DK, RET_DV), F32),
        ],
        scratch_shapes=[
            pltpu.VMEM((heads, chunk, chunk), F32),
            pltpu.VMEM((heads, chunk, 1), F32),
            pltpu.VMEM((heads, chunk, 1), F32),
            pltpu.VMEM((heads, chunk, RET_DV), F32),
        ],
        compiler_params=_params("parallel", "parallel", "arbitrary"),
        name="retention",
    )(*args)


def _mm_res_kernel(a_ref, w_ref, r_ref, o_ref):
    o_ref[...] = r_ref[...] + jnp.dot(a_ref[...], w_ref[...], preferred_element_type=F32)


def _mm_res(a, w, res, *, tm, tn):
    m, kdim = a.shape
    n = w.shape[1]
    return pl.pallas_call(
        _mm_res_kernel,
        grid=(m // tm, n // tn),
        in_specs=[
            pl.BlockSpec((tm, kdim), lambda i, j: (i, 0)),
            pl.BlockSpec((kdim, tn), lambda i, j: (0, j)),
            pl.BlockSpec((tm, tn), lambda i, j: (i, j)),
        ],
        out_specs=pl.BlockSpec((tm, tn), lambda i, j: (i, j)),
        out_shape=jax.ShapeDtypeStruct((m, n), F32),
        compiler_params=_params("parallel", "arbitrary"),
        name="matmul_residual",
    )(a, w, res)


def _ffn_kernel(*refs, final_norm):
    if final_norm:
        x_ref, gain_ref, wg_ref, wu_ref, wd_ref, fgain_ref, o_ref, xn_ref = refs
    else:
        x_ref, gain_ref, wg_ref, wu_ref, wd_ref, o_ref, xn_ref = refs
    j = pl.program_id(1)

    @pl.when(j == 0)
    def _():
        x = x_ref[...]
        xn_ref[...] = _rmsnorm(x, gain_ref[...]).astype(BF16)
        o_ref[...] = x

    xn = xn_ref[...]
    gate = jnp.dot(xn, wg_ref[...], preferred_element_type=F32)
    up = jnp.dot(xn, wu_ref[...], preferred_element_type=F32)
    hidden = (_silu(gate) * up).astype(BF16)
    o_ref[...] += jnp.dot(hidden, wd_ref[...], preferred_element_type=F32)

    if final_norm:
        @pl.when(j == pl.num_programs(1) - 1)
        def _():
            o_ref[...] = _rmsnorm(o_ref[...], fgain_ref[...])


def _ffn(x, gain, wg, wu, wd, final_gain=None, *, layer, tm):
    m, d = x.shape
    f = wg.shape[2]
    tf = FFN_COL_TILE if tm >= FFN_ROW_TILE else FFN_COL_TILE_SMALL_ROWS
    final_norm = final_gain is not None
    in_specs = [
        pl.BlockSpec((tm, d), lambda i, j: (i, 0)),
        pl.BlockSpec((1, d), lambda i, j: (0, 0)),
        pl.BlockSpec((None, d, tf), lambda i, j: (layer, 0, j)),
        pl.BlockSpec((None, d, tf), lambda i, j: (layer, 0, j)),
        pl.BlockSpec((None, tf, d), lambda i, j: (layer, j, 0)),
    ]
    args = [x, gain, wg, wu, wd]
    if final_norm:
        in_specs.append(pl.BlockSpec((1, d), lambda i, j: (0, 0)))
        args.append(final_gain)
    return pl.pallas_call(
        functools.partial(_ffn_kernel, final_norm=final_norm),
        grid=(m // tm, f // tf),
        in_specs=in_specs,
        out_specs=pl.BlockSpec((tm, d), lambda i, j: (i, 0)),
        out_shape=jax.ShapeDtypeStruct((m, d), F32),
        scratch_shapes=[pltpu.VMEM((tm, d), BF16)],
        compiler_params=_params("parallel", "arbitrary"),
        name="ffn",
    )(*args)


def _kv_kernel(x_ref, gain_ref, w_ref, bias_ref, k32_ref, v32_ref, kh_ref, vh_ref, lf_ref, lf16_ref, *, key_tile):
    rows = x_ref.shape[0] // ROW_GROUPS
    for r in range(ROW_GROUPS):
        rs = slice(r * rows, (r + 1) * rows)
        xn = _rmsnorm(x_ref[rs, :], gain_ref[...]).astype(BF16)
        y = jnp.dot(xn, w_ref[...], preferred_element_type=F32)
        k = y[:, :FOX_KV]
        v = y[:, FOX_KV:2 * FOX_KV]
        for h in range(FOX_KV_HEADS):
            hs = slice(h * FOX_HD, (h + 1) * FOX_HD)
            out_rows = pl.ds(r * rows * FOX_KV_HEADS + h, rows, stride=FOX_KV_HEADS)
            k32_ref[out_rows, :] = k[:, hs]
            v32_ref[out_rows, :] = v[:, hs]
            kh_ref[h, rs, :] = k[:, hs].astype(BF16)
            if key_tile is None:
                vh_ref[h, rs, :] = v[:, hs].astype(BF16)
            else:
                v_t = v[:, hs].T.astype(BF16)
                for t in range(rows // key_tile):
                    tile = r * (rows // key_tile) + t
                    vh_ref[h, tile, :FOX_HD, :] = v_t[:, t * key_tile:(t + 1) * key_tile]
                    vh_ref[h, tile, FOX_HD:, :] = jnp.ones((FOX_V_ROWS - FOX_HD, key_tile), BF16)
        z = y[:, 2 * FOX_KV:] + bias_ref[...]
        lf = jnp.minimum(z, 0.0) - jnp.log1p(jnp.exp(-jnp.abs(z)))
        lf_ref[rs, :] = lf
        lf16_ref[rs, :] = lf[:, :FOX_HEADS]


def _kv_proj(x, gain, w, bias, *, tm, key_tile):
    m, d = x.shape
    n = w.shape[1]
    if key_tile is None:
        v_spec = pl.BlockSpec((FOX_KV_HEADS, tm, FOX_HD), lambda i: (0, i, 0))
        v_shape = jax.ShapeDtypeStruct((FOX_KV_HEADS, m, FOX_HD), BF16)
    else:
        v_spec = pl.BlockSpec((FOX_KV_HEADS, tm // key_tile, FOX_V_ROWS, key_tile), lambda i: (0, i, 0, 0))
        v_shape = jax.ShapeDtypeStruct((FOX_KV_HEADS, m // key_tile, FOX_V_ROWS, key_tile), BF16)
    return pl.pallas_call(
        functools.partial(_kv_kernel, key_tile=key_tile),
        grid=(m // tm,),
        in_specs=[
            pl.BlockSpec((tm, d), lambda i: (i, 0)),
            pl.BlockSpec((1, d), lambda i: (0, 0)),
            pl.BlockSpec((d, n), lambda i: (0, 0)),
            pl.BlockSpec((1, LANES), lambda i: (0, 0)),
        ],
        out_specs=[
            pl.BlockSpec((tm * FOX_KV_HEADS, FOX_HD), lambda i: (i, 0)),
            pl.BlockSpec((tm * FOX_KV_HEADS, FOX_HD), lambda i: (i, 0)),
            pl.BlockSpec((FOX_KV_HEADS, tm, FOX_HD), lambda i: (0, i, 0)),
            v_spec,
            pl.BlockSpec((tm, LANES), lambda i: (i, 0)),
            pl.BlockSpec((tm, FOX_HEADS), lambda i: (i, 0)),
        ],
        out_shape=[
            jax.ShapeDtypeStruct((m * FOX_KV_HEADS, FOX_HD), F32),
            jax.ShapeDtypeStruct((m * FOX_KV_HEADS, FOX_HD), F32),
            jax.ShapeDtypeStruct((FOX_KV_HEADS, m, FOX_HD), BF16),
            v_shape,
            jax.ShapeDtypeStruct((m, LANES), F32),
            jax.ShapeDtypeStruct((m, FOX_HEADS), F32),
        ],
        compiler_params=_params("parallel"),
        name="kv_proj",
    )(x, gain, w, bias)


def _q_kernel(x_ref, gain_ref, w_ref, o_ref):
    rows = x_ref.shape[0] // ROW_GROUPS
    for r in range(ROW_GROUPS):
        rs = slice(r * rows, (r + 1) * rows)
        xn = _rmsnorm(x_ref[rs, :], gain_ref[...]).astype(BF16)
        y = jnp.dot(xn, w_ref[...], preferred_element_type=F32)
        o_ref[rs, :] = (y * (FOX_HD ** -0.5 * LOG2_E)).astype(BF16)


def _q_proj(x, gain, w, *, tm):
    m, d = x.shape
    n = w.shape[1]
    return pl.pallas_call(
        _q_kernel,
        grid=(m // tm,),
        in_specs=[
            pl.BlockSpec((tm, d), lambda i: (i, 0)),
            pl.BlockSpec((1, d), lambda i: (0, 0)),
            pl.BlockSpec((d, n), lambda i: (0, 0)),
        ],
        out_specs=pl.BlockSpec((tm, n), lambda i: (i, 0)),
        out_shape=jax.ShapeDtypeStruct((m, n), BF16),
        compiler_params=_params("parallel"),
        name="q_proj",
    )(x, gain, w)


def _cumsum_kernel(x_ref, o_ref, carry_ref):
    @pl.when(pl.program_id(1) == 0)
    def _():
        carry_ref[...] = jnp.zeros_like(carry_ref)

    x = x_ref[0]
    tl = x.shape[0]
    r = lax.broadcasted_iota(jnp.int32, (tl, tl), 0)
    c = lax.broadcasted_iota(jnp.int32, (tl, tl), 1)
    tri = jnp.where(c <= r, 1.0, 0.0).astype(BF16)
    x_hi = x.astype(BF16)
    x_rest = x - x_hi.astype(F32)
    x_mid = x_rest.astype(BF16)
    x_lo = (x_rest - x_mid.astype(F32)).astype(BF16)
    y = (jnp.dot(tri, x_lo, preferred_element_type=F32) + jnp.dot(tri, x_mid, preferred_element_type=F32)
         + jnp.dot(tri, x_hi, preferred_element_type=F32) + carry_ref[...])
    carry_ref[...] = y[tl - 1:tl, :]
    y2 = y * LOG2_E
    hi = y2.astype(BF16).astype(F32)
    rest = y2 - hi
    mid = rest.astype(BF16).astype(F32)
    lo = rest - mid
    lane = lax.broadcasted_iota(jnp.int32, y.shape, 1)
    packed = jnp.where(
        lane < FOX_HEADS, hi,
        jnp.where(lane < 2 * FOX_HEADS, pltpu.roll(mid, FOX_HEADS, 1),
                  jnp.where(lane < 3 * FOX_HEADS, pltpu.roll(lo, 2 * FOX_HEADS, 1), 0.0)))
    o_ref[0] = packed.astype(BF16)


def _cumsum_pieces(x, *, tl):
    b, length, lanes = x.shape
    return pl.pallas_call(
        _cumsum_kernel,
        grid=(b, length // tl),
        in_specs=[pl.BlockSpec((1, tl, lanes), lambda i, t: (i, t, 0))],
        out_specs=pl.BlockSpec((1, tl, lanes), lambda i, t: (i, t, 0)),
        out_shape=jax.ShapeDtypeStruct((b, length, lanes), BF16),
        scratch_shapes=[pltpu.VMEM((1, lanes), F32)],
        compiler_params=_params("parallel", "arbitrary"),
        name="logf_cumsum",
    )(x)


def _fox_kernel(q_ref, k_ref, f_ref, v_ref, o_ref, qa_ref, z0_ref, z1_ref, mb0_ref, mb1_ref, m_ref, acc_ref,
                *, tq, tk, q_start, blocks):
    step = pl.program_id(2)
    rows = FOX_GROUP * tq
    lane = lax.broadcasted_iota(jnp.int32, (tq, FOX_HD), 1)
    for x in range(blocks):
        for g in range(FOX_GROUP):
            head = pl.program_id(1) * FOX_GROUP + g
            pick = jnp.where(lane < FOX_F_PIECES * FOX_HEADS,
                             jnp.where((lane & (FOX_HEADS - 1)) == head, -1.0, 0.0), 0.0)
            qa_ref[x, g * tq:(g + 1) * tq, :FOX_HD] = q_ref[0, x * tq:(x + 1) * tq, g * FOX_HD:(g + 1) * FOX_HD]
            qa_ref[x, g * tq:(g + 1) * tq, FOX_HD:] = pick.astype(BF16)
    n_full = (q_start + step * blocks * tq) // tk

    m_ref[...] = jnp.full_like(m_ref, MASK_VALUE)
    acc_ref[...] = jnp.zeros_like(acc_ref)

    cw = min(rows, FOX_COL_CHUNK)
    chunks = [slice(c * cw, (c + 1) * cw) for c in range(rows // cw)]
    buffers = ((z0_ref, mb0_ref), (z1_ref, mb1_ref))

    def score(x, kb, masked, buf):
        z_ref, mb_ref = buffers[buf]
        start = pl.multiple_of(kb * tk, tk)
        ka = jnp.concatenate([k_ref[0, 0, pl.ds(start, tk), :], f_ref[0, pl.ds(start, tk), :]], axis=1)
        for c, cs in enumerate(chunks):
            z = lax.dot_general(ka, qa_ref[x, cs, :], (((1,), (1,)), ((), ())), preferred_element_type=F32)
            if masked:
                kpos = kb * tk + lax.broadcasted_iota(jnp.int32, (tk, cw), 0)
                col = c * cw + lax.broadcasted_iota(jnp.int32, (tk, cw), 1)
                qpos = q_start + (step * blocks + x) * tq + (col & (tq - 1))
                z = jnp.where(kpos <= qpos, z, MASK_VALUE)
            z_ref[:, cs] = z
            mb_ref[:, cs] = jnp.max(z, axis=0, keepdims=True)

    def accumulate(x, kb, buf):
        z_ref, mb_ref = buffers[buf]
        vt = v_ref[0, 0, kb]
        for cs in chunks:
            m_old = m_ref[x, :, cs]
            m_new = jnp.maximum(m_old, mb_ref[:, cs])
            alpha = jnp.exp2(m_old - m_new)
            p = jnp.exp2(z_ref[:, cs] - m_new)
            acc_ref[x, :, cs] = alpha * acc_ref[x, :, cs] + jnp.dot(vt, p.astype(BF16), preferred_element_type=F32)
            m_ref[x, :, cs] = m_new

    if blocks == 2:
        score(0, n_full, True, 0)
        score(1, n_full, True, 1)
        accumulate(0, n_full, 0)

        def tile(kb, pending):
            score(0, kb, False, 0)
            accumulate(1, pending, 1)
            score(1, kb, False, 1)
            accumulate(0, kb, 0)
            return kb

        lax.fori_loop(0, n_full // 2, lambda i, pending: tile(2 * i + 1, tile(2 * i, pending)), n_full)

        @pl.when(n_full % 2 == 1)
        def _():
            tile(n_full - 1, jnp.where(n_full >= 2, n_full - 2, n_full))

        accumulate(1, jnp.where(n_full >= 1, n_full - 1, n_full), 1)
    else:
        score(0, n_full, True, 0)

        def pair(i, pending):
            score(0, 2 * i, False, 1)
            accumulate(0, pending, 0)
            score(0, 2 * i + 1, False, 0)
            accumulate(0, 2 * i, 1)
            return 2 * i + 1

        pending = lax.fori_loop(0, n_full // 2, pair, n_full)

        @pl.when(n_full % 2 == 1)
        def _():
            score(0, n_full - 1, False, 1)
            accumulate(0, pending, 0)
            accumulate(0, n_full - 1, 1)

        @pl.when(n_full % 2 == 0)
        def _():
            accumulate(0, pending, 0)

    for x in range(blocks):
        acc = acc_ref[x]
        out = (acc[:FOX_HD] / acc[FOX_HD:FOX_HD + 1]).T
        for g in range(FOX_GROUP):
            o_ref[0, x * tq:(x + 1) * tq, g * FOX_HD:(g + 1) * FOX_HD] = out[g * tq:(g + 1) * tq, :].astype(BF16)


def _fox_attention(q, k_heads, f_pieces, v_t, *, tq, tk, q_start):
    b, lq, _ = q.shape
    lk = k_heads.shape[2]
    nk = lk // tk
    rows = FOX_GROUP * tq
    blocks = 2 if (2 * tq == tk and lq % tk == 0) else 1
    assert tq & (tq - 1) == 0 and tk % tq == 0 and q_start % tk == 0
    return pl.pallas_call(
        functools.partial(_fox_kernel, tq=tq, tk=tk, q_start=q_start, blocks=blocks),
        grid=(b, FOX_KV_HEADS, lq // (blocks * tq)),
        in_specs=[
            pl.BlockSpec((1, blocks * tq, FOX_GROUP * FOX_HD), lambda i, h, t: (i, t, h)),
            pl.BlockSpec((1, 1, lk, FOX_HD), lambda i, h, t: (h, i, 0, 0)),
            pl.BlockSpec((1, lk, LANES), lambda i, h, t: (i, 0, 0)),
            pl.BlockSpec((1, 1, nk, FOX_V_ROWS, tk), lambda i, h, t: (h, i, 0, 0, 0)),
        ],
        out_specs=pl.BlockSpec((1, blocks * tq, FOX_GROUP * FOX_HD), lambda i, h, t: (i, t, h)),
        out_shape=jax.ShapeDtypeStruct(q.shape, BF16),
        scratch_shapes=[
            pltpu.VMEM((blocks, rows, 2 * FOX_HD), BF16),
            pltpu.VMEM((tk, rows), F32),
            pltpu.VMEM((tk, rows), F32),
            pltpu.VMEM((1, rows), F32),
            pltpu.VMEM((1, rows), F32),
            pltpu.VMEM((blocks, 1, rows), F32),
            pltpu.VMEM((blocks, FOX_V_ROWS, rows), F32),
        ],
        compiler_params=_params("parallel", "parallel", "arbitrary"),
        name="fox_attention",
    )(q, k_heads, f_pieces, v_t)


def _rope_tables(pos):
    half = RET_DK // 2
    inv = ROPE_BASE ** (-jnp.arange(half, dtype=F32) / half)
    ang = pos.astype(F32)[:, None] * inv[None, :]
    return jnp.cos(ang), jnp.sin(ang)


def _with_cache(new_heads, cache, lk_pad):
    past = cache.transpose(2, 0, 1, 3).astype(BF16)
    both = jnp.concatenate([past, new_heads], axis=2)
    return jnp.pad(both, ((0, 0), (0, 0), (0, lk_pad - both.shape[2]), (0, 0)))


def _transpose_values(v_heads, tk):
    kvh, b, lk, _ = v_heads.shape
    v_t = v_heads.reshape(kvh, b, lk // tk, tk, FOX_HD).transpose(0, 1, 2, 4, 3)
    ones = jnp.ones((kvh, b, lk // tk, FOX_V_ROWS - FOX_HD, tk), BF16)
    return jnp.concatenate([v_t, ones], axis=3)


def _stream(x, *, pos, s0, cache, weights, tm, ffn_tm, chunk, tq):
    (norm_mix, norm_ffn, norm_kv, norm_final, w_ret_in, w_ret_o, w_kv, b_kv, w_fox_q, w_fox_o,
     w_ffn_gate, w_ffn_up, w_ffn_down) = weights
    b, length, d = x.shape
    m = b * length
    h = x.reshape(m, d)

    cos, sin = _rope_tables(pos)
    if cos.shape[0] < tm:
        cos = jnp.tile(cos, (tm // cos.shape[0], 1))
        sin = jnp.tile(sin, (tm // sin.shape[0], 1))
    proj = _ret_in(h, norm_mix[0:1], cos, sin, w_ret_in, tm=tm)
    o, s_fin = _retention(proj, s0, batch=b, seq=length, chunk=chunk)
    h = _mm_res(o, w_ret_o, h, tm=tm, tn=COL_TILE // 2)
    h = _ffn(h, norm_ffn[0:1], w_ffn_gate, w_ffn_up, w_ffn_down, layer=0, tm=ffn_tm)

    tk = FOX_K_TILE
    if cache is None:
        q_start = 0
        k32, v32, k_heads, v_t, lf_pad, lf = _kv_proj(h, norm_kv[None, :], w_kv, b_kv, tm=tm, key_tile=tk)
        k_heads = k_heads.reshape(FOX_KV_HEADS, b, length, FOX_HD)
        v_t = v_t.reshape(FOX_KV_HEADS, b, length // tk, FOX_V_ROWS, tk)
        lf_all = lf_pad.reshape(b, length, LANES)
    else:
        cache_k, cache_v, cache_logf = cache
        q_start = cache_k.shape[1]
        lk_pad = -(-(q_start + length) // tk) * tk
        k32, v32, k_heads, v_heads, lf_pad, lf = _kv_proj(h, norm_kv[None, :], w_kv, b_kv, tm=tm, key_tile=None)
        k_heads = _with_cache(k_heads.reshape(FOX_KV_HEADS, b, length, FOX_HD), cache_k, lk_pad)
        v_t = _transpose_values(_with_cache(v_heads.reshape(FOX_KV_HEADS, b, length, FOX_HD), cache_v, lk_pad), tk)
        lf_all = jnp.concatenate([
            jnp.pad(cache_logf.astype(F32), ((0, 0), (0, 0), (0, LANES - FOX_HEADS))),
            lf_pad.reshape(b, length, LANES)], axis=1)
        lf_all = jnp.pad(lf_all, ((0, 0), (0, lk_pad - lf_all.shape[1]), (0, 0)))
    lf = lf.reshape(b, length, FOX_HEADS)
    f_pieces = _cumsum_pieces(lf_all, tl=CUMSUM_TILE)

    q = _q_proj(h, norm_mix[1:2], w_fox_q, tm=tm).reshape(b, length, d)
    att = _fox_attention(q, k_heads, f_pieces, v_t, tq=tq, tk=tk, q_start=q_start)
    h = _mm_res(att.reshape(m, d), w_fox_o, h, tm=min(tm, ROW_TILE // 2), tn=d)
    y = _ffn(h, norm_ffn[1:2], w_ffn_gate, w_ffn_up, w_ffn_down, norm_final[None, :], layer=1, tm=ffn_tm)

    return (y.reshape(b, length, d), s_fin[None],
            k32.reshape(b, length, FOX_KV_HEADS, FOX_HD), v32.reshape(b, length, FOX_KV_HEADS, FOX_HD), lf)


def kernel(x_prompt, x_sample, state_ret, cache_k, cache_v, cache_logf, norm_mix, norm_ffn, norm_kv, norm_final, w_ret_in, w_ret_o, w_kv_k, w_kv_v, w_kv_f, b_kv_f, w_fox_q, w_fox_o, w_ffn_gate, w_ffn_up, w_ffn_down):
    lp = x_prompt.shape[1]
    bs, ls, _ = x_sample.shape
    past = cache_k.shape[1]

    w_kv = jnp.concatenate(
        [w_kv_k, w_kv_v, jnp.pad(w_kv_f, ((0, 0), (0, LANES - FOX_HEADS)))], axis=1).astype(BF16)
    b_kv = jnp.pad(b_kv_f.astype(F32), (0, LANES - FOX_HEADS))[None, :]
    weights = (norm_mix.astype(F32), norm_ffn.astype(F32), norm_kv.astype(F32), norm_final.astype(F32),
               w_ret_in[0].astype(BF16), w_ret_o[0].astype(BF16), w_kv, b_kv,
               w_fox_q[0].astype(BF16), w_fox_o[0].astype(BF16),
               w_ffn_gate.astype(BF16), w_ffn_up.astype(BF16), w_ffn_down.astype(BF16))

    y_p, s_p, k_p, v_p, lf_p = _stream(
        x_prompt, pos=jnp.arange(lp), s0=None, cache=None, weights=weights,
        tm=ROW_TILE, ffn_tm=FFN_ROW_TILE, chunk=RET_CHUNK, tq=FOX_Q_TILE)
    y_s, s_s, k_s, v_s, lf_s = _stream(
        x_sample, pos=past + jnp.arange(ls), s0=state_ret[0], cache=(cache_k, cache_v, cache_logf),
        weights=weights, tm=bs * ls, ffn_tm=bs * ls, chunk=ls, tq=ls)
    return (y_p, y_s, s_p, k_p, v_p, lf_p, s_s, k_s, v_s, lf_s)
```

```python
import functools

import jax
import jax.numpy as jnp
from jax import lax
from jax.experimental import pallas as pl
from jax.experimental.pallas import tpu as pltpu

F32 = jnp.float32
BF16 = jnp.bfloat16

D_MODEL = 2048
RET_HEADS = 8
RET_DK = D_MODEL // RET_HEADS
RET_DV = 2 * D_MODEL // RET_HEADS
RET_QK = RET_HEADS * RET_DK
RET_V = RET_HEADS * RET_DV
RET_HEAD_COLS = 2 * RET_DK + 2 * RET_DV
ROPE_BASE = 10000.0
FOX_HEADS = 16
FOX_HD = D_MODEL // FOX_HEADS
FOX_KV_HEADS = 4
FOX_GROUP = FOX_HEADS // FOX_KV_HEADS
FOX_KV = FOX_KV_HEADS * FOX_HD
FOX_F_PIECES = 3
FOX_V_ROWS = FOX_HD + 16
LOG2_E = 1.4426950408889634
NORM_EPS = 1e-6
GN_EPS = 1e-5

LANES = 128
BF16_SUBLANES = 16
VMEM_LIMIT_BYTES = 60 * 1024 * 1024

ROW_TILE = 1024
COL_TILE = 1024
ROW_GROUPS = 2
FFN_ROW_TILE = 1024
FFN_COL_TILE = 512
FFN_COL_TILE_SMALL_ROWS = 512
RET_CHUNK = 256
RET_HEADS_PER_STEP = 8
FOX_Q_TILE = 256
FOX_K_TILE = 512
FOX_COL_CHUNK = 512
CUMSUM_TILE = 512
MASK_VALUE = -1e30


def _params(*semantics):
    return pltpu.CompilerParams(dimension_semantics=semantics, vmem_limit_bytes=VMEM_LIMIT_BYTES)


def _rmsnorm(x, gain):
    y = x * lax.rsqrt(jnp.mean(x * x, axis=-1, keepdims=True) + NORM_EPS)
    return y * gain


def _silu(x):
    return x * (1.0 / (1.0 + jnp.exp(-x)))


def _ret_in_kernel(x_ref, gain_ref, cos_ref, sin_ref, wq_ref, wk_ref, wv_ref, wg_ref, *rest, n_cast):
    cast_in = rest[:n_cast]
    o_ref = rest[n_cast]
    cast_out = rest[n_cast + 1:2 * n_cast + 1]
    xn_ref = rest[2 * n_cast + 1]

    @pl.when(pl.program_id(1) == 0)
    def _():
        xn_ref[...] = _rmsnorm(x_ref[...], gain_ref[...]).astype(BF16)

    for src, dst in zip(cast_in, cast_out):
        dst[...] = src[...].astype(BF16)

    w = jnp.concatenate([wq_ref[...], wk_ref[...], wv_ref[...], wg_ref[...]], axis=1)
    y = jnp.dot(xn_ref[...], w, preferred_element_type=F32)
    cos = cos_ref[...]
    sin = sin_ref[...]
    half = RET_DK // 2
    for base, scale in ((0, 1.0), (RET_DK, RET_DK ** -0.5)):
        x1 = y[:, base:base + half]
        x2 = y[:, base + half:base + RET_DK]
        o_ref[:, base:base + half] = ((x1 * cos - x2 * sin) * scale).astype(BF16)
        o_ref[:, base + half:base + RET_DK] = ((x1 * sin + x2 * cos) * scale).astype(BF16)
    v_first = 2 * RET_DK
    g_first = v_first + RET_DV
    o_ref[:, v_first:g_first] = y[:, v_first:g_first].astype(BF16)
    o_ref[:, g_first:] = _silu(y[:, g_first:]).astype(BF16)


def _slab_spec(shape, steps):
    layers, rows, cols = shape
    slab = next(r for r in range(BF16_SUBLANES, rows + 1, BF16_SUBLANES)
                if rows % r == 0 and layers * rows // r <= steps)
    per_layer = rows // slab
    last = layers * per_layer - 1

    def index(i, j):
        b = jnp.minimum(i * RET_HEADS + j, last)
        return (b // per_layer, b % per_layer, 0)

    return pl.BlockSpec((None, slab, cols), index)


def _ret_in(x, gain, cos, sin, w, cast=(), *, tm):
    m, d = x.shape
    pos_tiles = cos.shape[0] // tm
    k_first = RET_QK // RET_DK
    v_first = 2 * RET_QK // RET_DV
    g_first = (2 * RET_QK + RET_V) // RET_DV
    grid = (m // tm, RET_HEADS)
    cast_specs = [_slab_spec(c.shape, grid[0] * grid[1]) for c in cast]
    outs = pl.pallas_call(
        functools.partial(_ret_in_kernel, n_cast=len(cast)),
        grid=grid,
        in_specs=[
            pl.BlockSpec((tm, d), lambda i, j: (i, 0)),
            pl.BlockSpec((1, d), lambda i, j: (0, 0)),
            pl.BlockSpec((tm, RET_DK // 2), lambda i, j: (i % pos_tiles, 0)),
            pl.BlockSpec((tm, RET_DK // 2), lambda i, j: (i % pos_tiles, 0)),
            pl.BlockSpec((d, RET_DK), lambda i, j: (0, j)),
            pl.BlockSpec((d, RET_DK), lambda i, j: (0, k_first + j)),
            pl.BlockSpec((d, RET_DV), lambda i, j: (0, v_first + j)),
            pl.BlockSpec((d, RET_DV), lambda i, j: (0, g_first + j)),
        ] + cast_specs,
        out_specs=[pl.BlockSpec((tm, RET_HEAD_COLS), lambda i, j: (i, j))] + cast_specs,
        out_shape=[jax.ShapeDtypeStruct((m, RET_HEADS * RET_HEAD_COLS), BF16)]
        + [jax.ShapeDtypeStruct(c.shape, BF16) for c in cast],
        scratch_shapes=[pltpu.VMEM((tm, d), BF16)],
        compiler_params=_params("arbitrary", "arbitrary"),
        name="ret_in_proj",
    )(x, gain, cos, sin, w, w, w, w, *cast)
    return outs[0], tuple(outs[1:])


def _retention_kernel(*refs, chunk, has_init):
    if has_init:
        p_ref, s0_ref, o_ref, s_ref, dm_ref, rd_ref, kw_ref, raw_ref = refs
    else:
        p_ref, o_ref, s_ref, dm_ref, rd_ref, kw_ref, raw_ref = refs
        s0_ref = None
    c = pl.program_id(2)
    heads = RET_HEADS_PER_STEP

    def log_gamma(hh):
        head = pl.program_id(1) * heads + hh
        return jnp.log(1.0 - jnp.exp2(-5.0 - jnp.full((1, 1), head, jnp.int32).astype(F32)))

    @pl.when(c == 0)
    def _():
        li = lax.broadcasted_iota(jnp.int32, (chunk, chunk), 0)
        mi = lax.broadcasted_iota(jnp.int32, (chunk, chunk), 1)
        diff = (li - mi).astype(F32)
        idx = lax.broadcasted_iota(jnp.int32, (chunk, 1), 0).astype(F32)
        for hh in range(heads):
            lg = log_gamma(hh)
            dm_ref[hh] = jnp.where(diff >= 0, jnp.exp(lg * jnp.maximum(diff, 0.0)), 0.0)
            rd_ref[hh] = jnp.exp(lg * (idx + 1.0))
            kw_ref[hh] = jnp.exp(lg * (chunk - 1.0 - idx))
        if has_init:
            s_ref[...] = s0_ref[...]
        else:
            s_ref[...] = jnp.zeros_like(s_ref)

    for hh in range(heads):
        base = hh * RET_HEAD_COLS
        q = p_ref[:, base:base + RET_DK]
        k = p_ref[:, base + RET_DK:base + 2 * RET_DK]
        v = p_ref[:, base + 2 * RET_DK:base + 2 * RET_DK + RET_DV]
        s_old = s_ref[0, hh]
        scores = lax.dot_general(q, k, (((1,), (1,)), ((), ())), preferred_element_type=F32) * dm_ref[hh]
        inner = jnp.dot(scores.astype(BF16), v, preferred_element_type=F32)
        cross = jnp.dot(q, s_old.astype(BF16), preferred_element_type=F32) * rd_ref[hh]
        raw_ref[hh] = inner + cross
        kd = (k.astype(F32) * kw_ref[hh]).astype(BF16)
        s_ref[0, hh] = jnp.exp(log_gamma(hh) * float(chunk)) * s_old + lax.dot_general(
            kd, v, (((0,), (0,)), ((), ())), preferred_element_type=F32)

    for hh in range(heads):
        o = raw_ref[hh]
        mu = jnp.mean(o, axis=-1, keepdims=True)
        cen = o - mu
        var = jnp.mean(cen * cen, axis=-1, keepdims=True)
        on = cen * lax.rsqrt(var + GN_EPS)
        gate = p_ref[:, (hh + 1) * RET_HEAD_COLS - RET_DV:(hh + 1) * RET_HEAD_COLS].astype(F32)
        o_ref[:, hh * RET_DV:(hh + 1) * RET_DV] = (on * gate).astype(BF16)


def _retention(proj, s0, *, batch, seq, chunk):
    nc = seq // chunk
    has_init = s0 is not None
    heads = RET_HEADS_PER_STEP
    v_w = heads * RET_DV
    in_specs = [pl.BlockSpec((chunk, heads * RET_HEAD_COLS), lambda b, h, c: (b * nc + c, h))]
    args = [proj]
    if has_init:
        in_specs.append(pl.BlockSpec((1, heads, RET_DK, RET_DV), lambda b, h, c: (b, h, 0, 0)))
        args.append(s0)
    return pl.pallas_call(
        functools.partial(_retention_kernel, chunk=chunk, has_init=has_init),
        grid=(batch, RET_HEADS // heads, nc),
        in_specs=in_specs,
        out_specs=[
            pl.BlockSpec((chunk, v_w), lambda b, h, c: (b * nc + c, h)),
            pl.BlockSpec((1, heads, RET_DK, RET_DV), lambda b, h, c: (b, h, 0, 0)),
        ],
        out_shape=[
            jax.ShapeDtypeStruct((batch * seq, RET_V), BF16),
            jax.ShapeDtypeStruct((batch, RET_HEADS, RET_DK, RET_DV), F32),
        ],
        scratch_shapes=[
            pltpu.VMEM((heads, chunk, chunk), F32),
            pltpu.VMEM((heads, chunk, 1), F32),
            pltpu.VMEM((heads, chunk, 1), F32),
            pltpu.VMEM((heads, chunk, RET_DV), F32),
        ],
        compiler_params=_params("parallel", "parallel", "arbitrary"),
        name="retention",
    )(*args)


def _mm_res_kernel(a_ref, w_ref, r_ref, o_ref):
    o_ref[...] = r_ref[...] + jnp.dot(a_ref[...], w_ref[...], preferred_element_type=F32)


def _mm_res(a, w, res, *, tm, tn):
    m, kdim = a.shape
    n = w.shape[1]
    return pl.pallas_call(
        _mm_res_kernel,
        grid=(m // tm, n // tn),
        in_specs=[
            pl.BlockSpec((tm, kdim), lambda i, j: (i, 0)),
            pl.BlockSpec((kdim, tn), lambda i, j: (0, j)),
            pl.BlockSpec((tm, tn), lambda i, j: (i, j)),
        ],
        out_specs=pl.BlockSpec((tm, tn), lambda i, j: (i, j)),
        out_shape=jax.ShapeDtypeStruct((m, n), F32),
        compiler_params=_params("parallel", "arbitrary"),
        name="matmul_residual",
    )(a, w, res)


def _ffn_kernel(*refs, final_norm):
    if final_norm:
        x_ref, gain_ref, wg_ref, wu_ref, wd_ref, fgain_ref, o_ref, xn_ref = refs
    else:
        x_ref, gain_ref, wg_ref, wu_ref, wd_ref, o_ref, xn_ref = refs
    j = pl.program_id(1)

    @pl.when(j == 0)
    def _():
        x = x_ref[...]
        xn_ref[...] = _rmsnorm(x, gain_ref[...]).astype(BF16)
        o_ref[...] = x

    xn = xn_ref[...]
    gate = jnp.dot(xn, wg_ref[...], preferred_element_type=F32)
    up = jnp.dot(xn, wu_ref[...], preferred_element_type=F32)
    hidden = (_silu(gate) * up).astype(BF16)
    o_ref[...] += jnp.dot(hidden, wd_ref[...], preferred_element_type=F32)

    if final_norm:
        @pl.when(j == pl.num_programs(1) - 1)
        def _():
            o_ref[...] = _rmsnorm(o_ref[...], fgain_ref[...])


def _ffn(x, gain, wg, wu, wd, final_gain=None, *, layer, tm):
    m, d = x.shape
    f = wg.shape[2]
    tf = FFN_COL_TILE if tm >= FFN_ROW_TILE else FFN_COL_TILE_SMALL_ROWS
    final_norm = final_gain is not None
    in_specs = [
        pl.BlockSpec((tm, d), lambda i, j: (i, 0)),
        pl.BlockSpec((1, d), lambda i, j: (0, 0)),
        pl.BlockSpec((None, d, tf), lambda i, j: (layer, 0, j)),
        pl.BlockSpec((None, d, tf), lambda i, j: (layer, 0, j)),
        pl.BlockSpec((None, tf, d), lambda i, j: (layer, j, 0)),
    ]
    args = [x, gain, wg, wu, wd]
    if final_norm:
        in_specs.append(pl.BlockSpec((1, d), lambda i, j: (0, 0)))
        args.append(final_gain)
    return pl.pallas_call(
        functools.partial(_ffn_kernel, final_norm=final_norm),
        grid=(m // tm, f // tf),
        in_specs=in_specs,
        out_specs=pl.BlockSpec((tm, d), lambda i, j: (i, 0)),
        out_shape=jax.ShapeDtypeStruct((m, d), F32),
        scratch_shapes=[pltpu.VMEM((tm, d), BF16)],
        compiler_params=_params("parallel", "arbitrary"),
        name="ffn",
    )(*args)


def _kv_kernel(x_ref, gain_ref, w_ref, bias_ref, k32_ref, v32_ref, kh_ref, vh_ref, lf_ref, lf16_ref, *, key_tile):
    rows = x_ref.shape[0] // ROW_GROUPS
    for r in range(ROW_GROUPS):
        rs = slice(r * rows, (r + 1) * rows)
        xn = _rmsnorm(x_ref[rs, :], gain_ref[...]).astype(BF16)
        y = jnp.dot(xn, w_ref[...], preferred_element_type=F32)
        k = y[:, :FOX_KV]
        v = y[:, FOX_KV:2 * FOX_KV]
        for h in range(FOX_KV_HEADS):
            hs = slice(h * FOX_HD, (h + 1) * FOX_HD)
            out_rows = pl.ds(r * rows * FOX_KV_HEADS + h, rows, stride=FOX_KV_HEADS)
            k32_ref[out_rows, :] = k[:, hs]
            v32_ref[out_rows, :] = v[:, hs]
            kh_ref[h, rs, :] = k[:, hs].astype(BF16)
            if key_tile is None:
                vh_ref[h, rs, :] = v[:, hs].astype(BF16)
            else:
                v_t = v[:, hs].T.astype(BF16)
                for t in range(rows // key_tile):
                    tile = r * (rows // key_tile) + t
                    vh_ref[h, tile, :FOX_HD, :] = v_t[:, t * key_tile:(t + 1) * key_tile]
                    vh_ref[h, tile, FOX_HD:, :] = jnp.ones((FOX_V_ROWS - FOX_HD, key_tile), BF16)
        z = y[:, 2 * FOX_KV:] + bias_ref[...]
        lf = jnp.minimum(z, 0.0) - jnp.log1p(jnp.exp(-jnp.abs(z)))
        lf_ref[rs, :] = lf
        lf16_ref[rs, :] = lf[:, :FOX_HEADS]


def _kv_proj(x, gain, w, bias, *, tm, key_tile):
    m, d = x.shape
    n = w.shape[1]
    if key_tile is None:
        v_spec = pl.BlockSpec((FOX_KV_HEADS, tm, FOX_HD), lambda i: (0, i, 0))
        v_shape = jax.ShapeDtypeStruct((FOX_KV_HEADS, m, FOX_HD), BF16)
    else:
        v_spec = pl.BlockSpec((FOX_KV_HEADS, tm // key_tile, FOX_V_ROWS, key_tile), lambda i: (0, i, 0, 0))
        v_shape = jax.ShapeDtypeStruct((FOX_KV_HEADS, m // key_tile, FOX_V_ROWS, key_tile), BF16)
    return pl.pallas_call(
        functools.partial(_kv_kernel, key_tile=key_tile),
        grid=(m // tm,),
        in_specs=[
            pl.BlockSpec((tm, d), lambda i: (i, 0)),
            pl.BlockSpec((1, d), lambda i: (0, 0)),
            pl.BlockSpec((d, n), lambda i: (0, 0)),
            pl.BlockSpec((1, LANES), lambda i: (0, 0)),
        ],
        out_specs=[
            pl.BlockSpec((tm * FOX_KV_HEADS, FOX_HD), lambda i: (i, 0)),
            pl.BlockSpec((tm * FOX_KV_HEADS, FOX_HD), lambda i: (i, 0)),
            pl.BlockSpec((FOX_KV_HEADS, tm, FOX_HD), lambda i: (0, i, 0)),
            v_spec,
            pl.BlockSpec((tm, LANES), lambda i: (i, 0)),
            pl.BlockSpec((tm, FOX_HEADS), lambda i: (i, 0)),
        ],
        out_shape=[
            jax.ShapeDtypeStruct((m * FOX_KV_HEADS, FOX_HD), F32),
            jax.ShapeDtypeStruct((m * FOX_KV_HEADS, FOX_HD), F32),
            jax.ShapeDtypeStruct((FOX_KV_HEADS, m, FOX_HD), BF16),
            v_shape,
            jax.ShapeDtypeStruct((m, LANES), F32),
            jax.ShapeDtypeStruct((m, FOX_HEADS), F32),
        ],
        compiler_params=_params("parallel"),
        name="kv_proj",
    )(x, gain, w, bias)


def _q_kernel(x_ref, gain_ref, w_ref, o_ref):
    rows = x_ref.shape[0] // ROW_GROUPS
    for r in range(ROW_GROUPS):
        rs = slice(r * rows, (r + 1) * rows)
        xn = _rmsnorm(x_ref[rs, :], gain_ref[...]).astype(BF16)
        y = jnp.dot(xn, w_ref[...], preferred_element_type=F32)
        o_ref[rs, :] = (y * (FOX_HD ** -0.5 * LOG2_E)).astype(BF16)


def _q_proj(x, gain, w, *, tm):
    m, d = x.shape
    n = w.shape[1]
    return pl.pallas_call(
        _q_kernel,
        grid=(m // tm,),
        in_specs=[
            pl.BlockSpec((tm, d), lambda i: (i, 0)),
            pl.BlockSpec((1, d), lambda i: (0, 0)),
            pl.BlockSpec((d, n), lambda i: (0, 0)),
        ],
        out_specs=pl.BlockSpec((tm, n), lambda i: (i, 0)),
        out_shape=jax.ShapeDtypeStruct((m, n), BF16),
        compiler_params=_params("parallel"),
        name="q_proj",
    )(x, gain, w)


def _cumsum_kernel(x_ref, o_ref, carry_ref):
    @pl.when(pl.program_id(1) == 0)
    def _():
        carry_ref[...] = jnp.zeros_like(carry_ref)

    x = x_ref[0]
    tl = x.shape[0]
    r = lax.broadcasted_iota(jnp.int32, (tl, tl), 0)
    c = lax.broadcasted_iota(jnp.int32, (tl, tl), 1)
    tri = jnp.where(c <= r, 1.0, 0.0).astype(BF16)
    x_hi = x.astype(BF16)
    x_rest = x - x_hi.astype(F32)
    x_mid = x_rest.astype(BF16)
    x_lo = (x_rest - x_mid.astype(F32)).astype(BF16)
    y = (jnp.dot(tri, x_lo, preferred_element_type=F32) + jnp.dot(tri, x_mid, preferred_element_type=F32)
         + jnp.dot(tri, x_hi, preferred_element_type=F32) + carry_ref[...])
    carry_ref[...] = y[tl - 1:tl, :]
    y2 = y * LOG2_E
    hi = y2.astype(BF16).astype(F32)
    rest = y2 - hi
    mid = rest.astype(BF16).astype(F32)
    lo = rest - mid
    lane = lax.broadcasted_iota(jnp.int32, y.shape, 1)
    packed = jnp.where(
        lane < FOX_HEADS, hi,
        jnp.where(lane < 2 * FOX_HEADS, pltpu.roll(mid, FOX_HEADS, 1),
                  jnp.where(lane < 3 * FOX_HEADS, pltpu.roll(lo, 2 * FOX_HEADS, 1), 0.0)))
    o_ref[0] = packed.astype(BF16)


def _cumsum_pieces(x, *, tl):
    b, length, lanes = x.shape
    return pl.pallas_call(
        _cumsum_kernel,
        grid=(b, length // tl),
        in_specs=[pl.BlockSpec((1, tl, lanes), lambda i, t: (i, t, 0))],
        out_specs=pl.BlockSpec((1, tl, lanes), lambda i, t: (i, t, 0)),
        out_shape=jax.ShapeDtypeStruct((b, length, lanes), BF16),
        scratch_shapes=[pltpu.VMEM((1, lanes), F32)],
        compiler_params=_params("parallel", "arbitrary"),
        name="logf_cumsum",
    )(x)


def _fox_kernel(q_ref, k_ref, f_ref, v_ref, o_ref, qa_ref, z0_ref, z1_ref, mb0_ref, mb1_ref, m_ref, acc_ref,
                *, tq, tk, q_start, blocks):
    step = pl.program_id(2)
    rows = FOX_GROUP * tq
    lane = lax.broadcasted_iota(jnp.int32, (tq, FOX_HD), 1)
    for x in range(blocks):
        for g in range(FOX_GROUP):
            head = pl.program_id(1) * FOX_GROUP + g
            pick = jnp.where(lane < FOX_F_PIECES * FOX_HEADS,
                             jnp.where((lane & (FOX_HEADS - 1)) == head, -1.0, 0.0), 0.0)
            qa_ref[x, g * tq:(g + 1) * tq, :FOX_HD] = q_ref[0, x * tq:(x + 1) * tq, g * FOX_HD:(g + 1) * FOX_HD]
            qa_ref[x, g * tq:(g + 1) * tq, FOX_HD:] = pick.astype(BF16)
    n_full = (q_start + step * blocks * tq) // tk

    m_ref[...] = jnp.full_like(m_ref, MASK_VALUE)
    acc_ref[...] = jnp.zeros_like(acc_ref)

    cw = min(rows, FOX_COL_CHUNK)
    chunks = [slice(c * cw, (c + 1) * cw) for c in range(rows // cw)]
    buffers = ((z0_ref, mb0_ref), (z1_ref, mb1_ref))

    def score(x, kb, masked, buf):
        z_ref, mb_ref = buffers[buf]
        start = pl.multiple_of(kb * tk, tk)
        ka = jnp.concatenate([k_ref[0, 0, pl.ds(start, tk), :], f_ref[0, pl.ds(start, tk), :]], axis=1)
        for c, cs in enumerate(chunks):
            z = lax.dot_general(ka, qa_ref[x, cs, :], (((1,), (1,)), ((), ())), preferred_element_type=F32)
            if masked:
                kpos = kb * tk + lax.broadcasted_iota(jnp.int32, (tk, cw), 0)
                col = c * cw + lax.broadcasted_iota(jnp.int32, (tk, cw), 1)
                qpos = q_start + (step * blocks + x) * tq + (col & (tq - 1))
                z = jnp.where(kpos <= qpos, z, MASK_VALUE)
            z_ref[:, cs] = z
            mb_ref[:, cs] = jnp.max(z, axis=0, keepdims=True)

    def accumulate(x, kb, buf):
        z_ref, mb_ref = buffers[buf]
        vt = v_ref[0, 0, kb]
        for cs in chunks:
            m_old = m_ref[x, :, cs]
            m_new = jnp.maximum(m_old, mb_ref[:, cs])
            alpha = jnp.exp2(m_old - m_new)
            p = jnp.exp2(z_ref[:, cs] - m_new)
            acc_ref[x, :, cs] = alpha * acc_ref[x, :, cs] + jnp.dot(vt, p.astype(BF16), preferred_element_type=F32)
            m_ref[x, :, cs] = m_new

    if blocks == 2:
        score(0, n_full, True, 0)
        score(1, n_full, True, 1)
        accumulate(0, n_full, 0)

        def tile(kb, pending):
            score(0, kb, False, 0)
            accumulate(1, pending, 1)
            score(1, kb, False, 1)
            accumulate(0, kb, 0)
            return kb

        lax.fori_loop(0, n_full // 2, lambda i, pending: tile(2 * i + 1, tile(2 * i, pending)), n_full)

        @pl.when(n_full % 2 == 1)
        def _():
            tile(n_full - 1, jnp.where(n_full >= 2, n_full - 2, n_full))

        accumulate(1, jnp.where(n_full >= 1, n_full - 1, n_full), 1)
    else:
        score(0, n_full, True, 0)

        def pair(i, pending):
            score(0, 2 * i, False, 1)
            accumulate(0, pending, 0)
            score(0, 2 * i + 1, False, 0)
            accumulate(0, 2 * i, 1)
            return 2 * i + 1

        pending = lax.fori_loop(0, n_full // 2, pair, n_full)

        @pl.when(n_full % 2 == 1)
        def _():
            score(0, n_full - 1, False, 1)
            accumulate(0, pending, 0)
            accumulate(0, n_full - 1, 1)

        @pl.when(n_full % 2 == 0)
        def _():
            accumulate(0, pending, 0)

    for x in range(blocks):
        acc = acc_ref[x]
        out = (acc[:FOX_HD] / acc[FOX_HD:FOX_HD + 1]).T
        for g in range(FOX_GROUP):
            o_ref[0, x * tq:(x + 1) * tq, g * FOX_HD:(g + 1) * FOX_HD] = out[g * tq:(g + 1) * tq, :].astype(BF16)


def _fox_attention(q, k_heads, f_pieces, v_t, *, tq, tk, q_start):
    b, lq, _ = q.shape
    lk = k_heads.shape[2]
    nk = lk // tk
    rows = FOX_GROUP * tq
    blocks = 2 if (2 * tq == tk and lq % tk == 0) else 1
    assert tq & (tq - 1) == 0 and tk % tq == 0 and q_start % tk == 0
    return pl.pallas_call(
        functools.partial(_fox_kernel, tq=tq, tk=tk, q_start=q_start, blocks=blocks),
        grid=(b, FOX_KV_HEADS, lq // (blocks * tq)),
        in_specs=[
            pl.BlockSpec((1, blocks * tq, FOX_GROUP * FOX_HD), lambda i, h, t: (i, t, h)),
            pl.BlockSpec((1, 1, lk, FOX_HD), lambda i, h, t: (h, i, 0, 0)),
            pl.BlockSpec((1, lk, LANES), lambda i, h, t: (i, 0, 0)),
            pl.BlockSpec((1, 1, nk, FOX_V_ROWS, tk), lambda i, h, t: (h, i, 0, 0, 0)),
        ],
        out_specs=pl.BlockSpec((1, blocks * tq, FOX_GROUP * FOX_HD), lambda i, h, t: (i, t, h)),
        out_shape=jax.ShapeDtypeStruct(q.shape, BF16),
        scratch_shapes=[
            pltpu.VMEM((blocks, rows, 2 * FOX_HD), BF16),
            pltpu.VMEM((tk, rows), F32),
            pltpu.VMEM((tk, rows), F32),
            pltpu.VMEM((1, rows), F32),
            pltpu.VMEM((1, rows), F32),
            pltpu.VMEM((blocks, 1, rows), F32),
            pltpu.VMEM((blocks, FOX_V_ROWS, rows), F32),
        ],
        compiler_params=_params("parallel", "parallel", "arbitrary"),
        name="fox_attention",
    )(q, k_heads, f_pieces, v_t)


def _rope_tables(pos):
    half = RET_DK // 2
    inv = ROPE_BASE ** (-jnp.arange(half, dtype=F32) / half)
    ang = pos.astype(F32)[:, None] * inv[None, :]
    return jnp.cos(ang), jnp.sin(ang)


def _with_cache(new_heads, cache, lk_pad):
    past = cache.transpose(2, 0, 1, 3).astype(BF16)
    both = jnp.concatenate([past, new_heads], axis=2)
    return jnp.pad(both, ((0, 0), (0, 0), (0, lk_pad - both.shape[2]), (0, 0)))


def _transpose_values(v_heads, tk):
    kvh, b, lk, _ = v_heads.shape
    v_t = v_heads.reshape(kvh, b, lk // tk, tk, FOX_HD).transpose(0, 1, 2, 4, 3)
    ones = jnp.ones((kvh, b, lk // tk, FOX_V_ROWS - FOX_HD, tk), BF16)
    return jnp.concatenate([v_t, ones], axis=3)


def _stream(x, *, pos, s0, cache, weights, later_weights, tm, ffn_tm, chunk, tq):
    norm_mix, norm_ffn, norm_kv, norm_final, w_ret_in, w_ret_o, w_kv, b_kv, w_fox_q, w_fox_o = weights
    b, length, d = x.shape
    m = b * length
    h = x.reshape(m, d)

    cos, sin = _rope_tables(pos)
    if cos.shape[0] < tm:
        cos = jnp.tile(cos, (tm // cos.shape[0], 1))
        sin = jnp.tile(sin, (tm // sin.shape[0], 1))
    if later_weights[0].dtype == BF16:
        proj, _ = _ret_in(h, norm_mix[0:1], cos, sin, w_ret_in, tm=tm)
    else:
        proj, later_weights = _ret_in(h, norm_mix[0:1], cos, sin, w_ret_in, later_weights, tm=tm)
    w_ffn_gate, w_ffn_up, w_ffn_down = later_weights
    o, s_fin = _retention(proj, s0, batch=b, seq=length, chunk=chunk)
    h = _mm_res(o, w_ret_o, h, tm=tm, tn=COL_TILE)
    h = _ffn(h, norm_ffn[0:1], w_ffn_gate, w_ffn_up, w_ffn_down, layer=0, tm=ffn_tm)

    tk = FOX_K_TILE
    if cache is None:
        q_start = 0
        k32, v32, k_heads, v_t, lf_pad, lf = _kv_proj(h, norm_kv[None, :], w_kv, b_kv, tm=tm, key_tile=tk)
        k_heads = k_heads.reshape(FOX_KV_HEADS, b, length, FOX_HD)
        v_t = v_t.reshape(FOX_KV_HEADS, b, length // tk, FOX_V_ROWS, tk)
        lf_all = lf_pad.reshape(b, length, LANES)
    else:
        cache_k, cache_v, cache_logf = cache
        q_start = cache_k.shape[1]
        lk_pad = -(-(q_start + length) // tk) * tk
        k32, v32, k_heads, v_heads, lf_pad, lf = _kv_proj(h, norm_kv[None, :], w_kv, b_kv, tm=tm, key_tile=None)
        k_heads = _with_cache(k_heads.reshape(FOX_KV_HEADS, b, length, FOX_HD), cache_k, lk_pad)
        v_t = _transpose_values(_with_cache(v_heads.reshape(FOX_KV_HEADS, b, length, FOX_HD), cache_v, lk_pad), tk)
        lf_all = jnp.concatenate([
            jnp.pad(cache_logf.astype(F32), ((0, 0), (0, 0), (0, LANES - FOX_HEADS))),
            lf_pad.reshape(b, length, LANES)], axis=1)
        lf_all = jnp.pad(lf_all, ((0, 0), (0, lk_pad - lf_all.shape[1]), (0, 0)))
    lf = lf.reshape(b, length, FOX_HEADS)
    f_pieces = _cumsum_pieces(lf_all, tl=CUMSUM_TILE)

    q = _q_proj(h, norm_mix[1:2], w_fox_q, tm=tm).reshape(b, length, d)
    att = _fox_attention(q, k_heads, f_pieces, v_t, tq=tq, tk=tk, q_start=q_start)
    h = _mm_res(att.reshape(m, d), w_fox_o, h, tm=min(tm, ROW_TILE // 2), tn=d)
    y = _ffn(h, norm_ffn[1:2], w_ffn_gate, w_ffn_up, w_ffn_down, norm_final[None, :], layer=1, tm=ffn_tm)

    outputs = (y.reshape(b, length, d), s_fin[None],
               k32.reshape(b, length, FOX_KV_HEADS, FOX_HD), v32.reshape(b, length, FOX_KV_HEADS, FOX_HD), lf)
    return outputs, later_weights


def kernel(x_prompt, x_sample, state_ret, cache_k, cache_v, cache_logf, norm_mix, norm_ffn, norm_kv, norm_final, w_ret_in, w_ret_o, w_kv_k, w_kv_v, w_kv_f, b_kv_f, w_fox_q, w_fox_o, w_ffn_gate, w_ffn_up, w_ffn_down):
    lp = x_prompt.shape[1]
    bs, ls, _ = x_sample.shape
    past = cache_k.shape[1]

    w_kv = jnp.concatenate(
        [w_kv_k, w_kv_v, jnp.pad(w_kv_f, ((0, 0), (0, LANES - FOX_HEADS)))], axis=1).astype(BF16)
    b_kv = jnp.pad(b_kv_f.astype(F32), (0, LANES - FOX_HEADS))[None, :]
    weights = (norm_mix.astype(F32), norm_ffn.astype(F32), norm_kv.astype(F32), norm_final.astype(F32),
               w_ret_in[0].astype(BF16), w_ret_o[0].astype(BF16), w_kv, b_kv,
               w_fox_q[0].astype(BF16), w_fox_o[0].astype(BF16))
    later_weights = tuple(w.astype(F32) for w in (w_ffn_gate, w_ffn_up, w_ffn_down))

    (y_p, s_p, k_p, v_p, lf_p), later_weights = _stream(
        x_prompt, pos=jnp.arange(lp), s0=None, cache=None, weights=weights, later_weights=later_weights,
        tm=ROW_TILE, ffn_tm=FFN_ROW_TILE, chunk=RET_CHUNK, tq=FOX_Q_TILE)
    (y_s, s_s, k_s, v_s, lf_s), _ = _stream(
        x_sample, pos=past + jnp.arange(ls), s0=state_ret[0], cache=(cache_k, cache_v, cache_logf),
        weights=weights, later_weights=later_weights, tm=bs * ls, ffn_tm=bs * ls, chunk=ls, tq=ls)
    return (y_p, y_s, s_p, k_p, v_p, lf_p, s_s, k_s, v_s, lf_s)
```

```python
import functools

import jax
import jax.numpy as jnp
from jax import lax
from jax.experimental import pallas as pl
from jax.experimental.pallas import tpu as pltpu

F32 = jnp.float32
BF16 = jnp.bfloat16

D_MODEL = 2048
RET_HEADS = 8
RET_DK = D_MODEL // RET_HEADS
RET_DV = 2 * D_MODEL // RET_HEADS
RET_QK = RET_HEADS * RET_DK
RET_V = RET_HEADS * RET_DV
RET_HEAD_COLS = 2 * RET_DK + 2 * RET_DV
ROPE_BASE = 10000.0
FOX_HEADS = 16
FOX_HD = D_MODEL // FOX_HEADS
FOX_KV_HEADS = 4
FOX_GROUP = FOX_HEADS // FOX_KV_HEADS
FOX_KV = FOX_KV_HEADS * FOX_HD
FOX_F_PIECES = 3
FOX_V_ROWS = FOX_HD + 16
LOG2_E = 1.4426950408889634
NORM_EPS = 1e-6
GN_EPS = 1e-5

LANES = 128
BF16_SUBLANES = 16
VMEM_LIMIT_BYTES = 60 * 1024 * 1024

ROW_TILE = 1024
COL_TILE = 1024
ROW_GROUPS = 2
FFN_ROW_TILE = 1024
FFN_COL_TILE = 512
FFN_COL_TILE_SMALL_ROWS = 512
RET_CHUNK = 256
RET_HEADS_PER_STEP = 8
FOX_Q_TILE = 256
FOX_K_TILE = 512
FOX_TILES_PER_TRIP = 4
FOX_COL_CHUNK = 512
CUMSUM_TILE = 512
MASK_VALUE = -1e30


def _params(*semantics):
    return pltpu.CompilerParams(dimension_semantics=semantics, vmem_limit_bytes=VMEM_LIMIT_BYTES)


def _rmsnorm(x, gain):
    y = x * lax.rsqrt(jnp.mean(x * x, axis=-1, keepdims=True) + NORM_EPS)
    return y * gain


def _silu(x):
    return x * (1.0 / (1.0 + jnp.exp(-x)))


def _ret_in_kernel(x_ref, gain_ref, cos_ref, sin_ref, wq_ref, wk_ref, wv_ref, wg_ref, *rest, n_cast):
    cast_in = rest[:n_cast]
    o_ref = rest[n_cast]
    cast_out = rest[n_cast + 1:2 * n_cast + 1]
    xn_ref = rest[2 * n_cast + 1]

    @pl.when(pl.program_id(1) == 0)
    def _():
        xn_ref[...] = _rmsnorm(x_ref[...], gain_ref[...]).astype(BF16)

    for src, dst in zip(cast_in, cast_out):
        dst[...] = src[...].astype(BF16)

    w = jnp.concatenate([wg_ref[...], wq_ref[...], wk_ref[...], wv_ref[...]], axis=1)
    y = jnp.dot(xn_ref[...], w, preferred_element_type=F32)
    cos = cos_ref[...]
    sin = sin_ref[...]
    half = RET_DK // 2
    v_first = 2 * RET_DK
    g_first = v_first + RET_DV
    o_ref[:, g_first:] = _silu(y[:, :RET_DV]).astype(BF16)
    for src, dst, scale in ((RET_DV, 0, 1.0), (RET_DV + RET_DK, RET_DK, RET_DK ** -0.5)):
        x1 = y[:, src:src + half]
        x2 = y[:, src + half:src + RET_DK]
        o_ref[:, dst:dst + half] = ((x1 * cos - x2 * sin) * scale).astype(BF16)
        o_ref[:, dst + half:dst + RET_DK] = ((x1 * sin + x2 * cos) * scale).astype(BF16)
    o_ref[:, v_first:g_first] = y[:, RET_DV + 2 * RET_DK:].astype(BF16)


def _slab_spec(shape, steps):
    layers, rows, cols = shape
    slab = next(r for r in range(BF16_SUBLANES, rows + 1, BF16_SUBLANES)
                if rows % r == 0 and layers * rows // r <= steps)
    per_layer = rows // slab
    last = layers * per_layer - 1

    def index(i, j):
        b = jnp.minimum(i * RET_HEADS + j, last)
        return (b // per_layer, b % per_layer, 0)

    return pl.BlockSpec((None, slab, cols), index)


def _ret_in(x, gain, cos, sin, w, cast=(), *, tm):
    m, d = x.shape
    pos_tiles = cos.shape[0] // tm
    k_first = RET_QK // RET_DK
    v_first = 2 * RET_QK // RET_DV
    g_first = (2 * RET_QK + RET_V) // RET_DV
    grid = (m // tm, RET_HEADS)
    cast_specs = [_slab_spec(c.shape, grid[0] * grid[1]) for c in cast]
    outs = pl.pallas_call(
        functools.partial(_ret_in_kernel, n_cast=len(cast)),
        grid=grid,
        in_specs=[
            pl.BlockSpec((tm, d), lambda i, j: (i, 0)),
            pl.BlockSpec((1, d), lambda i, j: (0, 0)),
            pl.BlockSpec((tm, RET_DK // 2), lambda i, j: (i % pos_tiles, 0)),
            pl.BlockSpec((tm, RET_DK // 2), lambda i, j: (i % pos_tiles, 0)),
            pl.BlockSpec((d, RET_DK), lambda i, j: (0, j)),
            pl.BlockSpec((d, RET_DK), lambda i, j: (0, k_first + j)),
            pl.BlockSpec((d, RET_DV), lambda i, j: (0, v_first + j)),
            pl.BlockSpec((d, RET_DV), lambda i, j: (0, g_first + j)),
        ] + cast_specs,
        out_specs=[pl.BlockSpec((tm, RET_HEAD_COLS), lambda i, j: (i, j))] + cast_specs,
        out_shape=[jax.ShapeDtypeStruct((m, RET_HEADS * RET_HEAD_COLS), BF16)]
        + [jax.ShapeDtypeStruct(c.shape, BF16) for c in cast],
        scratch_shapes=[pltpu.VMEM((tm, d), BF16)],
        compiler_params=_params("arbitrary", "arbitrary"),
        name="ret_in_proj",
    )(x, gain, cos, sin, w, w, w, w, *cast)
    return outs[0], tuple(outs[1:])


def _retention_kernel(*refs, chunk, has_init):
    if has_init:
        p_ref, s0_ref, o_ref, s_ref, dm_ref, rd_ref, kw_ref, raw_ref = refs
    else:
        p_ref, o_ref, s_ref, dm_ref, rd_ref, kw_ref, raw_ref = refs
        s0_ref = None
    c = pl.program_id(2)
    heads = RET_HEADS_PER_STEP

    def log_gamma(hh):
        head = pl.program_id(1) * heads + hh
        return jnp.log(1.0 - jnp.exp2(-5.0 - jnp.full((1, 1), head, jnp.int32).astype(F32)))

    @pl.when(c == 0)
    def _():
        li = lax.broadcasted_iota(jnp.int32, (chunk, chunk), 0)
        mi = lax.broadcasted_iota(jnp.int32, (chunk, chunk), 1)
        diff = (li - mi).astype(F32)
        idx = lax.broadcasted_iota(jnp.int32, (chunk, 1), 0).astype(F32)
        for hh in range(heads):
            lg = log_gamma(hh)
            dm_ref[hh] = jnp.where(diff >= 0, jnp.exp(lg * jnp.maximum(diff, 0.0)), 0.0)
            rd_ref[hh] = jnp.exp(lg * (idx + 1.0))
            kw_ref[hh] = jnp.exp(lg * (chunk - 1.0 - idx))
        if has_init:
            s_ref[...] = s0_ref[...]
        else:
            s_ref[...] = jnp.zeros_like(s_ref)

    for hh in range(heads):
        base = hh * RET_HEAD_COLS
        q = p_ref[:, base:base + RET_DK]
        k = p_ref[:, base + RET_DK:base + 2 * RET_DK]
        v = p_ref[:, base + 2 * RET_DK:base + 2 * RET_DK + RET_DV]
        s_old = s_ref[0, hh]
        scores = lax.dot_general(q, k, (((1,), (1,)), ((), ())), preferred_element_type=F32) * dm_ref[hh]
        inner = jnp.dot(scores.astype(BF16), v, preferred_element_type=F32)
        cross = jnp.dot(q, s_old.astype(BF16), preferred_element_type=F32) * rd_ref[hh]
        raw_ref[hh] = inner + cross
        kd = (k.astype(F32) * kw_ref[hh]).astype(BF16)
        s_ref[0, hh] = jnp.exp(log_gamma(hh) * float(chunk)) * s_old + lax.dot_general(
            kd, v, (((0,), (0,)), ((), ())), preferred_element_type=F32)

    for hh in range(heads):
        o = raw_ref[hh]
        mu = jnp.mean(o, axis=-1, keepdims=True)
        cen = o - mu
        var = jnp.mean(cen * cen, axis=-1, keepdims=True)
        on = cen * lax.rsqrt(var + GN_EPS)
        gate = p_ref[:, (hh + 1) * RET_HEAD_COLS - RET_DV:(hh + 1) * RET_HEAD_COLS].astype(F32)
        o_ref[:, hh * RET_DV:(hh + 1) * RET_DV] = (on * gate).astype(BF16)


def _retention(proj, s0, *, batch, seq, chunk):
    nc = seq // chunk
    has_init = s0 is not None
    heads = RET_HEADS_PER_STEP
    v_w = heads * RET_DV
    in_specs = [pl.BlockSpec((chunk, heads * RET_HEAD_COLS), lambda b, h, c: (b * nc + c, h))]
    args = [proj]
    if has_init:
        in_specs.append(pl.BlockSpec((1, heads, RET_DK, RET_DV), lambda b, h, c: (b, h, 0, 0)))
        args.append(s0)
    return pl.pallas_call(
        functools.partial(_retention_kernel, chunk=chunk, has_init=has_init),
        grid=(batch, RET_HEADS // heads, nc),
        in_specs=in_specs,
        out_specs=[
            pl.BlockSpec((chunk, v_w), lambda b, h, c: (b * nc + c, h)),
            pl.BlockSpec((1, heads, RET_DK, RET_DV), lambda b, h, c: (b, h, 0, 0)),
        ],
        out_shape=[
            jax.ShapeDtypeStruct((batch * seq, RET_V), BF16),
            jax.ShapeDtypeStruct((batch, RET_HEADS, RET_DK, RET_DV), F32),
        ],
        scratch_shapes=[
            pltpu.VMEM((heads, chunk, chunk), F32),
            pltpu.VMEM((heads, chunk, 1), F32),
            pltpu.VMEM((heads, chunk, 1), F32),
            pltpu.VMEM((heads, chunk, RET_DV), F32),
        ],
        compiler_params=_params("parallel", "parallel", "arbitrary"),
        name="retention",
    )(*args)


def _mm_res_kernel(a_ref, w_ref, r_ref, o_ref):
    o_ref[...] = r_ref[...] + jnp.dot(a_ref[...], w_ref[...], preferred_element_type=F32)


def _mm_res(a, w, res, *, tm, tn):
    m, kdim = a.shape
    n = w.shape[1]
    return pl.pallas_call(
        _mm_res_kernel,
        grid=(m // tm, n // tn),
        in_specs=[
            pl.BlockSpec((tm, kdim), lambda i, j: (i, 0)),
            pl.BlockSpec((kdim, tn), lambda i, j: (0, j)),
            pl.BlockSpec((tm, tn), lambda i, j: (i, j)),
        ],
        out_specs=pl.BlockSpec((tm, tn), lambda i, j: (i, j)),
        out_shape=jax.ShapeDtypeStruct((m, n), F32),
        compiler_params=_params("parallel", "arbitrary"),
        name="matmul_residual",
    )(a, w, res)


def _ffn_kernel(*refs, final_norm):
    if final_norm:
        x_ref, gain_ref, wg_ref, wu_ref, wd_ref, fgain_ref, o_ref, xn_ref = refs
    else:
        x_ref, gain_ref, wg_ref, wu_ref, wd_ref, o_ref, xn_ref = refs
    j = pl.program_id(1)

    @pl.when(j == 0)
    def _():
        x = x_ref[...]
        xn_ref[...] = _rmsnorm(x, gain_ref[...]).astype(BF16)
        o_ref[...] = x

    xn = xn_ref[...]
    gate = jnp.dot(xn, wg_ref[...], preferred_element_type=F32)
    up = jnp.dot(xn, wu_ref[...], preferred_element_type=F32)
    hidden = (_silu(gate) * up).astype(BF16)
    o_ref[...] += jnp.dot(hidden, wd_ref[...], preferred_element_type=F32)

    if final_norm:
        @pl.when(j == pl.num_programs(1) - 1)
        def _():
            o_ref[...] = _rmsnorm(o_ref[...], fgain_ref[...])


def _ffn(x, gain, wg, wu, wd, final_gain=None, *, layer, tm):
    m, d = x.shape
    f = wg.shape[2]
    tf = FFN_COL_TILE if tm >= FFN_ROW_TILE else FFN_COL_TILE_SMALL_ROWS
    final_norm = final_gain is not None
    in_specs = [
        pl.BlockSpec((tm, d), lambda i, j: (i, 0)),
        pl.BlockSpec((1, d), lambda i, j: (0, 0)),
        pl.BlockSpec((None, d, tf), lambda i, j: (layer, 0, j)),
        pl.BlockSpec((None, d, tf), lambda i, j: (layer, 0, j)),
        pl.BlockSpec((None, tf, d), lambda i, j: (layer, j, 0)),
    ]
    args = [x, gain, wg, wu, wd]
    if final_norm:
        in_specs.append(pl.BlockSpec((1, d), lambda i, j: (0, 0)))
        args.append(final_gain)
    return pl.pallas_call(
        functools.partial(_ffn_kernel, final_norm=final_norm),
        grid=(m // tm, f // tf),
        in_specs=in_specs,
        out_specs=pl.BlockSpec((tm, d), lambda i, j: (i, 0)),
        out_shape=jax.ShapeDtypeStruct((m, d), F32),
        scratch_shapes=[pltpu.VMEM((tm, d), BF16)],
        compiler_params=_params("parallel", "arbitrary"),
        name="ffn",
    )(*args)


def _kv_kernel(x_ref, gain_ref, w_ref, bias_ref, k32_ref, v32_ref, kh_ref, vh_ref, lf_ref, lf16_ref, *, key_tile):
    rows = x_ref.shape[0] // ROW_GROUPS
    for r in range(ROW_GROUPS):
        rs = slice(r * rows, (r + 1) * rows)
        xn = _rmsnorm(x_ref[rs, :], gain_ref[...]).astype(BF16)
        y = jnp.dot(xn, w_ref[...], preferred_element_type=F32)
        k = y[:, :FOX_KV]
        v = y[:, FOX_KV:2 * FOX_KV]
        for h in range(FOX_KV_HEADS):
            hs = slice(h * FOX_HD, (h + 1) * FOX_HD)
            out_rows = pl.ds(r * rows * FOX_KV_HEADS + h, rows, stride=FOX_KV_HEADS)
            k32_ref[out_rows, :] = k[:, hs]
            v32_ref[out_rows, :] = v[:, hs]
            kh_ref[h, rs, :] = k[:, hs].astype(BF16)
            if key_tile is None:
                vh_ref[h, rs, :] = v[:, hs].astype(BF16)
            else:
                v_t = v[:, hs].T.astype(BF16)
                for t in range(rows // key_tile):
                    tile = r * (rows // key_tile) + t
                    vh_ref[h, tile, :FOX_HD, :] = v_t[:, t * key_tile:(t + 1) * key_tile]
                    vh_ref[h, tile, FOX_HD:, :] = jnp.ones((FOX_V_ROWS - FOX_HD, key_tile), BF16)
        z = y[:, 2 * FOX_KV:] + bias_ref[...]
        lf = jnp.minimum(z, 0.0) - jnp.log1p(jnp.exp(-jnp.abs(z)))
        lf_ref[rs, :] = lf
        lf16_ref[rs, :] = lf[:, :FOX_HEADS]


def _kv_proj(x, gain, w, bias, *, tm, key_tile):
    m, d = x.shape
    n = w.shape[1]
    if key_tile is None:
        v_spec = pl.BlockSpec((FOX_KV_HEADS, tm, FOX_HD), lambda i: (0, i, 0))
        v_shape = jax.ShapeDtypeStruct((FOX_KV_HEADS, m, FOX_HD), BF16)
    else:
        v_spec = pl.BlockSpec((FOX_KV_HEADS, tm // key_tile, FOX_V_ROWS, key_tile), lambda i: (0, i, 0, 0))
        v_shape = jax.ShapeDtypeStruct((FOX_KV_HEADS, m // key_tile, FOX_V_ROWS, key_tile), BF16)
    return pl.pallas_call(
        functools.partial(_kv_kernel, key_tile=key_tile),
        grid=(m // tm,),
        in_specs=[
            pl.BlockSpec((tm, d), lambda i: (i, 0)),
            pl.BlockSpec((1, d), lambda i: (0, 0)),
            pl.BlockSpec((d, n), lambda i: (0, 0)),
            pl.BlockSpec((1, LANES), lambda i: (0, 0)),
        ],
        out_specs=[
            pl.BlockSpec((tm * FOX_KV_HEADS, FOX_HD), lambda i: (i, 0)),
            pl.BlockSpec((tm * FOX_KV_HEADS, FOX_HD), lambda i: (i, 0)),
            pl.BlockSpec((FOX_KV_HEADS, tm, FOX_HD), lambda i: (0, i, 0)),
            v_spec,
            pl.BlockSpec((tm, LANES), lambda i: (i, 0)),
            pl.BlockSpec((tm, FOX_HEADS), lambda i: (i, 0)),
        ],
        out_shape=[
            jax.ShapeDtypeStruct((m * FOX_KV_HEADS, FOX_HD), F32),
            jax.ShapeDtypeStruct((m * FOX_KV_HEADS, FOX_HD), F32),
            jax.ShapeDtypeStruct((FOX_KV_HEADS, m, FOX_HD), BF16),
            v_shape,
            jax.ShapeDtypeStruct((m, LANES), F32),
            jax.ShapeDtypeStruct((m, FOX_HEADS), F32),
        ],
        compiler_params=_params("parallel"),
        name="kv_proj",
    )(x, gain, w, bias)


def _q_kernel(x_ref, gain_ref, w_ref, o_ref):
    rows = x_ref.shape[0] // ROW_GROUPS
    for r in range(ROW_GROUPS):
        rs = slice(r * rows, (r + 1) * rows)
        xn = _rmsnorm(x_ref[rs, :], gain_ref[...]).astype(BF16)
        y = jnp.dot(xn, w_ref[...], preferred_element_type=F32)
        o_ref[rs, :] = (y * (FOX_HD ** -0.5 * LOG2_E)).astype(BF16)


def _q_proj(x, gain, w, *, tm):
    m, d = x.shape
    n = w.shape[1]
    return pl.pallas_call(
        _q_kernel,
        grid=(m // tm,),
        in_specs=[
            pl.BlockSpec((tm, d), lambda i: (i, 0)),
            pl.BlockSpec((1, d), lambda i: (0, 0)),
            pl.BlockSpec((d, n), lambda i: (0, 0)),
        ],
        out_specs=pl.BlockSpec((tm, n), lambda i: (i, 0)),
        out_shape=jax.ShapeDtypeStruct((m, n), BF16),
        compiler_params=_params("parallel"),
        name="q_proj",
    )(x, gain, w)


def _cumsum_kernel(x_ref, o_ref, carry_ref):
    @pl.when(pl.program_id(1) == 0)
    def _():
        carry_ref[...] = jnp.zeros_like(carry_ref)

    x = x_ref[0]
    tl = x.shape[0]
    r = lax.broadcasted_iota(jnp.int32, (tl, tl), 0)
    c = lax.broadcasted_iota(jnp.int32, (tl, tl), 1)
    tri = jnp.where(c <= r, 1.0, 0.0).astype(BF16)
    x_hi = x.astype(BF16)
    x_rest = x - x_hi.astype(F32)
    x_mid = x_rest.astype(BF16)
    x_lo = (x_rest - x_mid.astype(F32)).astype(BF16)
    y = (jnp.dot(tri, x_lo, preferred_element_type=F32) + jnp.dot(tri, x_mid, preferred_element_type=F32)
         + jnp.dot(tri, x_hi, preferred_element_type=F32) + carry_ref[...])
    carry_ref[...] = y[tl - 1:tl, :]
    y2 = y * LOG2_E
    hi = y2.astype(BF16).astype(F32)
    rest = y2 - hi
    mid = rest.astype(BF16).astype(F32)
    lo = rest - mid
    lane = lax.broadcasted_iota(jnp.int32, y.shape, 1)
    packed = jnp.where(
        lane < FOX_HEADS, hi,
        jnp.where(lane < 2 * FOX_HEADS, pltpu.roll(mid, FOX_HEADS, 1),
                  jnp.where(lane < 3 * FOX_HEADS, pltpu.roll(lo, 2 * FOX_HEADS, 1), 0.0)))
    o_ref[0] = packed.astype(BF16)


def _cumsum_pieces(x, *, tl):
    b, length, lanes = x.shape
    return pl.pallas_call(
        _cumsum_kernel,
        grid=(b, length // tl),
        in_specs=[pl.BlockSpec((1, tl, lanes), lambda i, t: (i, t, 0))],
        out_specs=pl.BlockSpec((1, tl, lanes), lambda i, t: (i, t, 0)),
        out_shape=jax.ShapeDtypeStruct((b, length, lanes), BF16),
        scratch_shapes=[pltpu.VMEM((1, lanes), F32)],
        compiler_params=_params("parallel", "arbitrary"),
        name="logf_cumsum",
    )(x)


def _fox_kernel(q_ref, k_ref, f_ref, v_ref, o_ref, qa_ref, z0_ref, z1_ref, mb0_ref, mb1_ref, m_ref, acc_ref,
                *, tq, tk, q_start, blocks):
    step = pl.program_id(2)
    rows = FOX_GROUP * tq
    lane = lax.broadcasted_iota(jnp.int32, (tq, FOX_HD), 1)
    for x in range(blocks):
        for g in range(FOX_GROUP):
            head = pl.program_id(1) * FOX_GROUP + g
            pick = jnp.where(lane < FOX_F_PIECES * FOX_HEADS,
                             jnp.where((lane & (FOX_HEADS - 1)) == head, -1.0, 0.0), 0.0)
            qa_ref[x, g * tq:(g + 1) * tq, :FOX_HD] = q_ref[0, x * tq:(x + 1) * tq, g * FOX_HD:(g + 1) * FOX_HD]
            qa_ref[x, g * tq:(g + 1) * tq, FOX_HD:] = pick.astype(BF16)
    n_full = (q_start + step * blocks * tq) // tk

    m_ref[...] = jnp.full_like(m_ref, MASK_VALUE)
    acc_ref[...] = jnp.zeros_like(acc_ref)

    cw = min(rows, FOX_COL_CHUNK)
    chunks = [slice(c * cw, (c + 1) * cw) for c in range(rows // cw)]
    buffers = ((z0_ref, mb0_ref), (z1_ref, mb1_ref))

    def score(x, kb, masked, buf):
        z_ref, mb_ref = buffers[buf]
        start = pl.multiple_of(kb * tk, tk)
        ka = jnp.concatenate([k_ref[0, 0, pl.ds(start, tk), :], f_ref[0, pl.ds(start, tk), :]], axis=1)
        for c, cs in enumerate(chunks):
            z = lax.dot_general(ka, qa_ref[x, cs, :], (((1,), (1,)), ((), ())), preferred_element_type=F32)
            if masked:
                kpos = kb * tk + lax.broadcasted_iota(jnp.int32, (tk, cw), 0)
                col = c * cw + lax.broadcasted_iota(jnp.int32, (tk, cw), 1)
                qpos = q_start + (step * blocks + x) * tq + (col & (tq - 1))
                z = jnp.where(kpos <= qpos, z, MASK_VALUE)
            z_ref[:, cs] = z
            mb_ref[:, cs] = jnp.max(z, axis=0, keepdims=True)

    def accumulate(x, kb, buf):
        z_ref, mb_ref = buffers[buf]
        vt = v_ref[0, 0, kb]
        for cs in chunks:
            m_old = m_ref[x, :, cs]
            m_new = jnp.maximum(m_old, mb_ref[:, cs])
            alpha = jnp.exp2(m_old - m_new)
            p = jnp.exp2(z_ref[:, cs] - m_new)
            acc_ref[x, :, cs] = alpha * acc_ref[x, :, cs] + jnp.dot(vt, p.astype(BF16), preferred_element_type=F32)
            m_ref[x, :, cs] = m_new

    if blocks == 2:
        score(0, n_full, True, 0)
        score(1, n_full, True, 1)
        accumulate(0, n_full, 0)

        def tile(kb, pending):
            score(0, kb, False, 0)
            accumulate(1, pending, 1)
            score(1, kb, False, 1)
            accumulate(0, kb, 0)
            return kb

        def tiles(first, count, pending):
            for t in range(count):
                pending = tile(first + t, pending)
            return pending

        def before(kb):
            return jnp.where(kb >= 1, kb - 1, n_full)

        unroll = FOX_TILES_PER_TRIP
        lax.fori_loop(0, n_full // unroll, lambda i, pending: tiles(unroll * i, unroll, pending), n_full)
        done = (n_full // unroll) * unroll
        count = unroll // 2
        while count >= 1:
            @pl.when((n_full - done) & count != 0)
            def _(done=done, count=count):
                tiles(done, count, before(done))

            done = done + ((n_full - done) & count)
            count //= 2

        accumulate(1, before(n_full), 1)
    else:
        score(0, n_full, True, 0)

        def pair(i, pending):
            score(0, 2 * i, False, 1)
            accumulate(0, pending, 0)
            score(0, 2 * i + 1, False, 0)
            accumulate(0, 2 * i, 1)
            return 2 * i + 1

        pending = lax.fori_loop(0, n_full // 2, pair, n_full)

        @pl.when(n_full % 2 == 1)
        def _():
            score(0, n_full - 1, False, 1)
            accumulate(0, pending, 0)
            accumulate(0, n_full - 1, 1)

        @pl.when(n_full % 2 == 0)
        def _():
            accumulate(0, pending, 0)

    for x in range(blocks):
        acc = acc_ref[x]
        out = (acc[:FOX_HD] / acc[FOX_HD:FOX_HD + 1]).T
        for g in range(FOX_GROUP):
            o_ref[0, x * tq:(x + 1) * tq, g * FOX_HD:(g + 1) * FOX_HD] = out[g * tq:(g + 1) * tq, :].astype(BF16)


def _fox_attention(q, k_heads, f_pieces, v_t, *, tq, tk, q_start):
    b, lq, _ = q.shape
    lk = k_heads.shape[2]
    nk = lk // tk
    rows = FOX_GROUP * tq
    blocks = 2 if (2 * tq == tk and lq % tk == 0) else 1
    assert tq & (tq - 1) == 0 and tk % tq == 0 and q_start % tk == 0
    return pl.pallas_call(
        functools.partial(_fox_kernel, tq=tq, tk=tk, q_start=q_start, blocks=blocks),
        grid=(b, FOX_KV_HEADS, lq // (blocks * tq)),
        in_specs=[
            pl.BlockSpec((1, blocks * tq, FOX_GROUP * FOX_HD), lambda i, h, t: (i, t, h)),
            pl.BlockSpec((1, 1, lk, FOX_HD), lambda i, h, t: (h, i, 0, 0)),
            pl.BlockSpec((1, lk, LANES), lambda i, h, t: (i, 0, 0)),
            pl.BlockSpec((1, 1, nk, FOX_V_ROWS, tk), lambda i, h, t: (h, i, 0, 0, 0)),
        ],
        out_specs=pl.BlockSpec((1, blocks * tq, FOX_GROUP * FOX_HD), lambda i, h, t: (i, t, h)),
        out_shape=jax.ShapeDtypeStruct(q.shape, BF16),
        scratch_shapes=[
            pltpu.VMEM((blocks, rows, 2 * FOX_HD), BF16),
            pltpu.VMEM((tk, rows), F32),
            pltpu.VMEM((tk, rows), F32),
            pltpu.VMEM((1, rows), F32),
            pltpu.VMEM((1, rows), F32),
            pltpu.VMEM((blocks, 1, rows), F32),
            pltpu.VMEM((blocks, FOX_V_ROWS, rows), F32),
        ],
        compiler_params=_params("parallel", "parallel", "arbitrary"),
        name="fox_attention",
    )(q, k_heads, f_pieces, v_t)


def _rope_tables(pos):
    half = RET_DK // 2
    inv = ROPE_BASE ** (-jnp.arange(half, dtype=F32) / half)
    ang = pos.astype(F32)[:, None] * inv[None, :]
    return jnp.cos(ang), jnp.sin(ang)


def _with_cache(new_heads, cache, lk_pad):
    past = cache.transpose(2, 0, 1, 3).astype(BF16)
    both = jnp.concatenate([past, new_heads], axis=2)
    return jnp.pad(both, ((0, 0), (0, 0), (0, lk_pad - both.shape[2]), (0, 0)))


def _transpose_values(v_heads, tk):
    kvh, b, lk, _ = v_heads.shape
    v_t = v_heads.reshape(kvh, b, lk // tk, tk, FOX_HD).transpose(0, 1, 2, 4, 3)
    ones = jnp.ones((kvh, b, lk // tk, FOX_V_ROWS - FOX_HD, tk), BF16)
    return jnp.concatenate([v_t, ones], axis=3)


def _stream(x, *, pos, s0, cache, weights, later_weights, tm, ffn_tm, chunk, tq):
    norm_mix, norm_ffn, norm_kv, norm_final, w_ret_in, w_ret_o, w_kv, b_kv, w_fox_q, w_fox_o = weights
    b, length, d = x.shape
    m = b * length
    h = x.reshape(m, d)

    cos, sin = _rope_tables(pos)
    if cos.shape[0] < tm:
        cos = jnp.tile(cos, (tm // cos.shape[0], 1))
        sin = jnp.tile(sin, (tm // sin.shape[0], 1))
    if later_weights[0].dtype == BF16:
        proj, _ = _ret_in(h, norm_mix[0:1], cos, sin, w_ret_in, tm=tm)
    else:
        proj, later_weights = _ret_in(h, norm_mix[0:1], cos, sin, w_ret_in, later_weights, tm=tm)
    w_ffn_gate, w_ffn_up, w_ffn_down = later_weights
    o, s_fin = _retention(proj, s0, batch=b, seq=length, chunk=chunk)
    h = _mm_res(o, w_ret_o, h, tm=tm, tn=COL_TILE)
    h = _ffn(h, norm_ffn[0:1], w_ffn_gate, w_ffn_up, w_ffn_down, layer=0, tm=ffn_tm)

    tk = FOX_K_TILE
    if cache is None:
        q_start = 0
        k32, v32, k_heads, v_t, lf_pad, lf = _kv_proj(h, norm_kv[None, :], w_kv, b_kv, tm=tm, key_tile=tk)
        k_heads = k_heads.reshape(FOX_KV_HEADS, b, length, FOX_HD)
        v_t = v_t.reshape(FOX_KV_HEADS, b, length // tk, FOX_V_ROWS, tk)
        lf_all = lf_pad.reshape(b, length, LANES)
    else:
        cache_k, cache_v, cache_logf = cache
        q_start = cache_k.shape[1]
        lk_pad = -(-(q_start + length) // tk) * tk
        k32, v32, k_heads, v_heads, lf_pad, lf = _kv_proj(h, norm_kv[None, :], w_kv, b_kv, tm=tm, key_tile=None)
        k_heads = _with_cache(k_heads.reshape(FOX_KV_HEADS, b, length, FOX_HD), cache_k, lk_pad)
        v_t = _transpose_values(_with_cache(v_heads.reshape(FOX_KV_HEADS, b, length, FOX_HD), cache_v, lk_pad), tk)
        lf_all = jnp.concatenate([
            jnp.pad(cache_logf.astype(F32), ((0, 0), (0, 0), (0, LANES - FOX_HEADS))),
            lf_pad.reshape(b, length, LANES)], axis=1)
        lf_all = jnp.pad(lf_all, ((0, 0), (0, lk_pad - lf_all.shape[1]), (0, 0)))
    lf = lf.reshape(b, length, FOX_HEADS)
    f_pieces = _cumsum_pieces(lf_all, tl=CUMSUM_TILE)

    q = _q_proj(h, norm_mix[1:2], w_fox_q, tm=tm).reshape(b, length, d)
    att = _fox_attention(q, k_heads, f_pieces, v_t, tq=tq, tk=tk, q_start=q_start)
    h = _mm_res(att.reshape(m, d), w_fox_o, h, tm=min(tm, ROW_TILE // 2), tn=d)
    y = _ffn(h, norm_ffn[1:2], w_ffn_gate, w_ffn_up, w_ffn_down, norm_final[None, :], layer=1, tm=ffn_tm)

    outputs = (y.reshape(b, length, d), s_fin[None],
               k32.reshape(b, length, FOX_KV_HEADS, FOX_HD), v32.reshape(b, length, FOX_KV_HEADS, FOX_HD), lf)
    return outputs, later_weights


def kernel(x_prompt, x_sample, state_ret, cache_k, cache_v, cache_logf, norm_mix, norm_ffn, norm_kv, norm_final, w_ret_in, w_ret_o, w_kv_k, w_kv_v, w_kv_f, b_kv_f, w_fox_q, w_fox_o, w_ffn_gate, w_ffn_up, w_ffn_down):
    lp = x_prompt.shape[1]
    bs, ls, _ = x_sample.shape
    past = cache_k.shape[1]

    w_kv = jnp.concatenate(
        [w_kv_k, w_kv_v, jnp.pad(w_kv_f, ((0, 0), (0, LANES - FOX_HEADS)))], axis=1).astype(BF16)
    b_kv = jnp.pad(b_kv_f.astype(F32), (0, LANES - FOX_HEADS))[None, :]
    weights = (norm_mix.astype(F32), norm_ffn.astype(F32), norm_kv.astype(F32), norm_final.astype(F32),
               w_ret_in[0].astype(BF16), w_ret_o[0].astype(BF16), w_kv, b_kv,
               w_fox_q[0].astype(BF16), w_fox_o[0].astype(BF16))
    later_weights = tuple(w.astype(F32) for w in (w_ffn_gate, w_ffn_up, w_ffn_down))

    (y_p, s_p, k_p, v_p, lf_p), later_weights = _stream(
        x_prompt, pos=jnp.arange(lp), s0=None, cache=None, weights=weights, later_weights=later_weights,
        tm=ROW_TILE, ffn_tm=FFN_ROW_TILE, chunk=RET_CHUNK, tq=FOX_Q_TILE)
    (y_s, s_s, k_s, v_s, lf_s), _ = _stream(
        x_sample, pos=past + jnp.arange(ls), s0=state_ret[0], cache=(cache_k, cache_v, cache_logf),
        weights=weights, later_weights=later_weights, tm=bs * ls, ffn_tm=bs * ls, chunk=ls, tq=ls)
    return (y_p, y_s, s_p, k_p, v_p, lf_p, s_s, k_s, v_s, lf_s)
```

```python
import functools

import jax
import jax.numpy as jnp
from jax import lax
from jax.experimental import pallas as pl
from jax.experimental.pallas import tpu as pltpu

F32 = jnp.float32
BF16 = jnp.bfloat16

D_MODEL = 2048
RET_HEADS = 8
RET_DK = D_MODEL // RET_HEADS
RET_DV = 2 * D_MODEL // RET_HEADS
RET_QK = RET_HEADS * RET_DK
RET_V = RET_HEADS * RET_DV
RET_HEAD_COLS = 2 * RET_DK + 2 * RET_DV
ROPE_BASE = 10000.0
FOX_HEADS = 16
FOX_HD = D_MODEL // FOX_HEADS
FOX_KV_HEADS = 4
FOX_GROUP = FOX_HEADS // FOX_KV_HEADS
FOX_KV = FOX_KV_HEADS * FOX_HD
FOX_F_PIECES = 3
FOX_V_ROWS = FOX_HD + 16
LOG2_E = 1.4426950408889634
NORM_EPS = 1e-6
GN_EPS = 1e-5

LANES = 128
BF16_SUBLANES = 16
VMEM_LIMIT_BYTES = 60 * 1024 * 1024

ROW_TILE = 1024
COL_TILE = 1024
ROW_GROUPS = 2
FFN_ROW_TILE = 1024
FFN_COL_TILE = 512
FFN_COL_TILE_SMALL_ROWS = 512
RET_CHUNK = 256
RET_HEADS_PER_STEP = 8
FOX_Q_TILE = 256
FOX_K_TILE = 512
FOX_BLOCKS_PER_STEP = 4
FOX_TILES_PER_TRIP = 2
FOX_COL_CHUNK = 512
CUMSUM_TILE = 512
MASK_VALUE = -1e30


def _params(*semantics):
    return pltpu.CompilerParams(dimension_semantics=semantics, vmem_limit_bytes=VMEM_LIMIT_BYTES)


def _rmsnorm(x, gain):
    y = x * lax.rsqrt(jnp.mean(x * x, axis=-1, keepdims=True) + NORM_EPS)
    return y * gain


def _silu(x):
    return x * (1.0 / (1.0 + jnp.exp(-x)))


def _ret_in_kernel(x_ref, gain_ref, cos_ref, sin_ref, wq_ref, wk_ref, wv_ref, wg_ref, *rest, n_cast):
    cast_in = rest[:n_cast]
    o_ref = rest[n_cast]
    cast_out = rest[n_cast + 1:2 * n_cast + 1]
    xn_ref = rest[2 * n_cast + 1]

    @pl.when(pl.program_id(1) == 0)
    def _():
        xn_ref[...] = _rmsnorm(x_ref[...], gain_ref[...]).astype(BF16)

    for src, dst in zip(cast_in, cast_out):
        dst[...] = src[...].astype(BF16)

    w = jnp.concatenate([wg_ref[...], wq_ref[...], wk_ref[...], wv_ref[...]], axis=1)
    y = jnp.dot(xn_ref[...], w, preferred_element_type=F32)
    cos = cos_ref[...]
    sin = sin_ref[...]
    half = RET_DK // 2
    v_first = 2 * RET_DK
    g_first = v_first + RET_DV
    o_ref[:, g_first:] = _silu(y[:, :RET_DV]).astype(BF16)
    for src, dst, scale in ((RET_DV, 0, 1.0), (RET_DV + RET_DK, RET_DK, RET_DK ** -0.5)):
        x1 = y[:, src:src + half]
        x2 = y[:, src + half:src + RET_DK]
        o_ref[:, dst:dst + half] = ((x1 * cos - x2 * sin) * scale).astype(BF16)
        o_ref[:, dst + half:dst + RET_DK] = ((x1 * sin + x2 * cos) * scale).astype(BF16)
    o_ref[:, v_first:g_first] = y[:, RET_DV + 2 * RET_DK:].astype(BF16)


def _slab_spec(shape, steps):
    layers, rows, cols = shape
    slab = next(r for r in range(BF16_SUBLANES, rows + 1, BF16_SUBLANES)
                if rows % r == 0 and layers * rows // r <= steps)
    per_layer = rows // slab
    last = layers * per_layer - 1

    def index(i, j):
        b = jnp.minimum(i * RET_HEADS + j, last)
        return (b // per_layer, b % per_layer, 0)

    return pl.BlockSpec((None, slab, cols), index)


def _ret_in(x, gain, cos, sin, w, cast=(), *, tm):
    m, d = x.shape
    pos_tiles = cos.shape[0] // tm
    k_first = RET_QK // RET_DK
    v_first = 2 * RET_QK // RET_DV
    g_first = (2 * RET_QK + RET_V) // RET_DV
    grid = (m // tm, RET_HEADS)
    cast_specs = [_slab_spec(c.shape, grid[0] * grid[1]) for c in cast]
    outs = pl.pallas_call(
        functools.partial(_ret_in_kernel, n_cast=len(cast)),
        grid=grid,
        in_specs=[
            pl.BlockSpec((tm, d), lambda i, j: (i, 0)),
            pl.BlockSpec((1, d), lambda i, j: (0, 0)),
            pl.BlockSpec((tm, RET_DK // 2), lambda i, j: (i % pos_tiles, 0)),
            pl.BlockSpec((tm, RET_DK // 2), lambda i, j: (i % pos_tiles, 0)),
            pl.BlockSpec((d, RET_DK), lambda i, j: (0, j)),
            pl.BlockSpec((d, RET_DK), lambda i, j: (0, k_first + j)),
            pl.BlockSpec((d, RET_DV), lambda i, j: (0, v_first + j)),
            pl.BlockSpec((d, RET_DV), lambda i, j: (0, g_first + j)),
        ] + cast_specs,
        out_specs=[pl.BlockSpec((tm, RET_HEAD_COLS), lambda i, j: (i, j))] + cast_specs,
        out_shape=[jax.ShapeDtypeStruct((m, RET_HEADS * RET_HEAD_COLS), BF16)]
        + [jax.ShapeDtypeStruct(c.shape, BF16) for c in cast],
        scratch_shapes=[pltpu.VMEM((tm, d), BF16)],
        compiler_params=_params("arbitrary", "arbitrary"),
        name="ret_in_proj",
    )(x, gain, cos, sin, w, w, w, w, *cast)
    return outs[0], tuple(outs[1:])


def _retention_kernel(*refs, chunk, has_init):
    if has_init:
        p_ref, s0_ref, o_ref, s_ref, dm_ref, rd_ref, kw_ref, raw_ref = refs
    else:
        p_ref, o_ref, s_ref, dm_ref, rd_ref, kw_ref, raw_ref = refs
        s0_ref = None
    c = pl.program_id(2)
    heads = RET_HEADS_PER_STEP

    def log_gamma(hh):
        head = pl.program_id(1) * heads + hh
        return jnp.log(1.0 - jnp.exp2(-5.0 - jnp.full((1, 1), head, jnp.int32).astype(F32)))

    @pl.when(c == 0)
    def _():
        li = lax.broadcasted_iota(jnp.int32, (chunk, chunk), 0)
        mi = lax.broadcasted_iota(jnp.int32, (chunk, chunk), 1)
        diff = (li - mi).astype(F32)
        idx = lax.broadcasted_iota(jnp.int32, (chunk, 1), 0).astype(F32)
        for hh in range(heads):
            lg = log_gamma(hh)
            dm_ref[hh] = jnp.where(diff >= 0, jnp.exp(lg * jnp.maximum(diff, 0.0)), 0.0)
            rd_ref[hh] = jnp.exp(lg * (idx + 1.0))
            kw_ref[hh] = jnp.exp(lg * (chunk - 1.0 - idx))
        if has_init:
            s_ref[...] = s0_ref[...]
        else:
            s_ref[...] = jnp.zeros_like(s_ref)

    for hh in range(heads):
        base = hh * RET_HEAD_COLS
        q = p_ref[:, base:base + RET_DK]
        k = p_ref[:, base + RET_DK:base + 2 * RET_DK]
        v = p_ref[:, base + 2 * RET_DK:base + 2 * RET_DK + RET_DV]
        s_old = s_ref[0, hh]
        scores = lax.dot_general(q, k, (((1,), (1,)), ((), ())), preferred_element_type=F32) * dm_ref[hh]
        inner = jnp.dot(scores.astype(BF16), v, preferred_element_type=F32)
        cross = jnp.dot(q, s_old.astype(BF16), preferred_element_type=F32) * rd_ref[hh]
        raw_ref[hh] = inner + cross
        kd = (k.astype(F32) * kw_ref[hh]).astype(BF16)
        s_ref[0, hh] = jnp.exp(log_gamma(hh) * float(chunk)) * s_old + lax.dot_general(
            kd, v, (((0,), (0,)), ((), ())), preferred_element_type=F32)

    for hh in range(heads):
        o = raw_ref[hh]
        mu = jnp.mean(o, axis=-1, keepdims=True)
        cen = o - mu
        var = jnp.mean(cen * cen, axis=-1, keepdims=True)
        on = cen * lax.rsqrt(var + GN_EPS)
        gate = p_ref[:, (hh + 1) * RET_HEAD_COLS - RET_DV:(hh + 1) * RET_HEAD_COLS].astype(F32)
        o_ref[:, hh * RET_DV:(hh + 1) * RET_DV] = (on * gate).astype(BF16)


def _retention(proj, s0, *, batch, seq, chunk):
    nc = seq // chunk
    has_init = s0 is not None
    heads = RET_HEADS_PER_STEP
    v_w = heads * RET_DV
    in_specs = [pl.BlockSpec((chunk, heads * RET_HEAD_COLS), lambda b, h, c: (b * nc + c, h))]
    args = [proj]
    if has_init:
        in_specs.append(pl.BlockSpec((1, heads, RET_DK, RET_DV), lambda b, h, c: (b, h, 0, 0)))
        args.append(s0)
    return pl.pallas_call(
        functools.partial(_retention_kernel, chunk=chunk, has_init=has_init),
        grid=(batch, RET_HEADS // heads, nc),
        in_specs=in_specs,
        out_specs=[
            pl.BlockSpec((chunk, v_w), lambda b, h, c: (b * nc + c, h)),
            pl.BlockSpec((1, heads, RET_DK, RET_DV), lambda b, h, c: (b, h, 0, 0)),
        ],
        out_shape=[
            jax.ShapeDtypeStruct((batch * seq, RET_V), BF16),
            jax.ShapeDtypeStruct((batch, RET_HEADS, RET_DK, RET_DV), F32),
        ],
        scratch_shapes=[
            pltpu.VMEM((heads, chunk, chunk), F32),
            pltpu.VMEM((heads, chunk, 1), F32),
            pltpu.VMEM((heads, chunk, 1), F32),
            pltpu.VMEM((heads, chunk, RET_DV), F32),
        ],
        compiler_params=_params("parallel", "parallel", "arbitrary"),
        name="retention",
    )(*args)


def _mm_res_kernel(a_ref, w_ref, r_ref, o_ref):
    o_ref[...] = r_ref[...] + jnp.dot(a_ref[...], w_ref[...], preferred_element_type=F32)


def _mm_res(a, w, res, *, tm, tn):
    m, kdim = a.shape
    n = w.shape[1]
    return pl.pallas_call(
        _mm_res_kernel,
        grid=(m // tm, n // tn),
        in_specs=[
            pl.BlockSpec((tm, kdim), lambda i, j: (i, 0)),
            pl.BlockSpec((kdim, tn), lambda i, j: (0, j)),
            pl.BlockSpec((tm, tn), lambda i, j: (i, j)),
        ],
        out_specs=pl.BlockSpec((tm, tn), lambda i, j: (i, j)),
        out_shape=jax.ShapeDtypeStruct((m, n), F32),
        compiler_params=_params("parallel", "arbitrary"),
        name="matmul_residual",
    )(a, w, res)


def _ffn_kernel(*refs, final_norm):
    if final_norm:
        x_ref, gain_ref, wg_ref, wu_ref, wd_ref, fgain_ref, o_ref, xn_ref = refs
    else:
        x_ref, gain_ref, wg_ref, wu_ref, wd_ref, o_ref, xn_ref = refs
    j = pl.program_id(1)

    @pl.when(j == 0)
    def _():
        x = x_ref[...]
        xn_ref[...] = _rmsnorm(x, gain_ref[...]).astype(BF16)
        o_ref[...] = x

    xn = xn_ref[...]
    gate = jnp.dot(xn, wg_ref[...], preferred_element_type=F32)
    up = jnp.dot(xn, wu_ref[...], preferred_element_type=F32)
    hidden = (_silu(gate) * up).astype(BF16)
    o_ref[...] += jnp.dot(hidden, wd_ref[...], preferred_element_type=F32)

    if final_norm:
        @pl.when(j == pl.num_programs(1) - 1)
        def _():
            o_ref[...] = _rmsnorm(o_ref[...], fgain_ref[...])


def _ffn(x, gain, wg, wu, wd, final_gain=None, *, layer, tm):
    m, d = x.shape
    f = wg.shape[2]
    tf = FFN_COL_TILE if tm >= FFN_ROW_TILE else FFN_COL_TILE_SMALL_ROWS
    final_norm = final_gain is not None
    in_specs = [
        pl.BlockSpec((tm, d), lambda i, j: (i, 0)),
        pl.BlockSpec((1, d), lambda i, j: (0, 0)),
        pl.BlockSpec((None, d, tf), lambda i, j: (layer, 0, j)),
        pl.BlockSpec((None, d, tf), lambda i, j: (layer, 0, j)),
        pl.BlockSpec((None, tf, d), lambda i, j: (layer, j, 0)),
    ]
    args = [x, gain, wg, wu, wd]
    if final_norm:
        in_specs.append(pl.BlockSpec((1, d), lambda i, j: (0, 0)))
        args.append(final_gain)
    return pl.pallas_call(
        functools.partial(_ffn_kernel, final_norm=final_norm),
        grid=(m // tm, f // tf),
        in_specs=in_specs,
        out_specs=pl.BlockSpec((tm, d), lambda i, j: (i, 0)),
        out_shape=jax.ShapeDtypeStruct((m, d), F32),
        scratch_shapes=[pltpu.VMEM((tm, d), BF16)],
        compiler_params=_params("parallel", "arbitrary"),
        name="ffn",
    )(*args)


def _kv_kernel(x_ref, gain_ref, w_ref, bias_ref, k32_ref, v32_ref, kh_ref, vh_ref, lf_ref, lf16_ref, *, key_tile):
    rows = x_ref.shape[0] // ROW_GROUPS
    for r in range(ROW_GROUPS):
        rs = slice(r * rows, (r + 1) * rows)
        xn = _rmsnorm(x_ref[rs, :], gain_ref[...]).astype(BF16)
        y = jnp.dot(xn, w_ref[...], preferred_element_type=F32)
        k = y[:, :FOX_KV]
        v = y[:, FOX_KV:2 * FOX_KV]
        for h in range(FOX_KV_HEADS):
            hs = slice(h * FOX_HD, (h + 1) * FOX_HD)
            out_rows = pl.ds(r * rows * FOX_KV_HEADS + h, rows, stride=FOX_KV_HEADS)
            k32_ref[out_rows, :] = k[:, hs]
            v32_ref[out_rows, :] = v[:, hs]
            kh_ref[h, rs, :] = k[:, hs].astype(BF16)
            if key_tile is None:
                vh_ref[h, rs, :] = v[:, hs].astype(BF16)
            else:
                v_t = v[:, hs].T.astype(BF16)
                for t in range(rows // key_tile):
                    tile = r * (rows // key_tile) + t
                    vh_ref[h, tile, :FOX_HD, :] = v_t[:, t * key_tile:(t + 1) * key_tile]
                    vh_ref[h, tile, FOX_HD:, :] = jnp.ones((FOX_V_ROWS - FOX_HD, key_tile), BF16)
        z = y[:, 2 * FOX_KV:] + bias_ref[...]
        lf = jnp.minimum(z, 0.0) - jnp.log1p(jnp.exp(-jnp.abs(z)))
        lf_ref[rs, :] = lf
        lf16_ref[rs, :] = lf[:, :FOX_HEADS]


def _kv_proj(x, gain, w, bias, *, tm, key_tile):
    m, d = x.shape
    n = w.shape[1]
    if key_tile is None:
        v_spec = pl.BlockSpec((FOX_KV_HEADS, tm, FOX_HD), lambda i: (0, i, 0))
        v_shape = jax.ShapeDtypeStruct((FOX_KV_HEADS, m, FOX_HD), BF16)
    else:
        v_spec = pl.BlockSpec((FOX_KV_HEADS, tm // key_tile, FOX_V_ROWS, key_tile), lambda i: (0, i, 0, 0))
        v_shape = jax.ShapeDtypeStruct((FOX_KV_HEADS, m // key_tile, FOX_V_ROWS, key_tile), BF16)
    return pl.pallas_call(
        functools.partial(_kv_kernel, key_tile=key_tile),
        grid=(m // tm,),
        in_specs=[
            pl.BlockSpec((tm, d), lambda i: (i, 0)),
            pl.BlockSpec((1, d), lambda i: (0, 0)),
            pl.BlockSpec((d, n), lambda i: (0, 0)),
            pl.BlockSpec((1, LANES), lambda i: (0, 0)),
        ],
        out_specs=[
            pl.BlockSpec((tm * FOX_KV_HEADS, FOX_HD), lambda i: (i, 0)),
            pl.BlockSpec((tm * FOX_KV_HEADS, FOX_HD), lambda i: (i, 0)),
            pl.BlockSpec((FOX_KV_HEADS, tm, FOX_HD), lambda i: (0, i, 0)),
            v_spec,
            pl.BlockSpec((tm, LANES), lambda i: (i, 0)),
            pl.BlockSpec((tm, FOX_HEADS), lambda i: (i, 0)),
        ],
        out_shape=[
            jax.ShapeDtypeStruct((m * FOX_KV_HEADS, FOX_HD), F32),
            jax.ShapeDtypeStruct((m * FOX_KV_HEADS, FOX_HD), F32),
            jax.ShapeDtypeStruct((FOX_KV_HEADS, m, FOX_HD), BF16),
            v_shape,
            jax.ShapeDtypeStruct((m, LANES), F32),
            jax.ShapeDtypeStruct((m, FOX_HEADS), F32),
        ],
        compiler_params=_params("parallel"),
        name="kv_proj",
    )(x, gain, w, bias)


def _q_kernel(x_ref, gain_ref, w_ref, o_ref):
    rows = x_ref.shape[0] // ROW_GROUPS
    for r in range(ROW_GROUPS):
        rs = slice(r * rows, (r + 1) * rows)
        xn = _rmsnorm(x_ref[rs, :], gain_ref[...]).astype(BF16)
        y = jnp.dot(xn, w_ref[...], preferred_element_type=F32)
        o_ref[rs, :] = (y * (FOX_HD ** -0.5 * LOG2_E)).astype(BF16)


def _q_proj(x, gain, w, *, tm):
    m, d = x.shape
    n = w.shape[1]
    return pl.pallas_call(
        _q_kernel,
        grid=(m // tm,),
        in_specs=[
            pl.BlockSpec((tm, d), lambda i: (i, 0)),
            pl.BlockSpec((1, d), lambda i: (0, 0)),
            pl.BlockSpec((d, n), lambda i: (0, 0)),
        ],
        out_specs=pl.BlockSpec((tm, n), lambda i: (i, 0)),
        out_shape=jax.ShapeDtypeStruct((m, n), BF16),
        compiler_params=_params("parallel"),
        name="q_proj",
    )(x, gain, w)


def _cumsum_kernel(x_ref, o_ref, carry_ref):
    @pl.when(pl.program_id(1) == 0)
    def _():
        carry_ref[...] = jnp.zeros_like(carry_ref)

    x = x_ref[0]
    tl = x.shape[0]
    r = lax.broadcasted_iota(jnp.int32, (tl, tl), 0)
    c = lax.broadcasted_iota(jnp.int32, (tl, tl), 1)
    tri = jnp.where(c <= r, 1.0, 0.0).astype(BF16)
    x_hi = x.astype(BF16)
    x_rest = x - x_hi.astype(F32)
    x_mid = x_rest.astype(BF16)
    x_lo = (x_rest - x_mid.astype(F32)).astype(BF16)
    y = (jnp.dot(tri, x_lo, preferred_element_type=F32) + jnp.dot(tri, x_mid, preferred_element_type=F32)
         + jnp.dot(tri, x_hi, preferred_element_type=F32) + carry_ref[...])
    carry_ref[...] = y[tl - 1:tl, :]
    y2 = y * LOG2_E
    hi = y2.astype(BF16).astype(F32)
    rest = y2 - hi
    mid = rest.astype(BF16).astype(F32)
    lo = rest - mid
    lane = lax.broadcasted_iota(jnp.int32, y.shape, 1)
    packed = jnp.where(
        lane < FOX_HEADS, hi,
        jnp.where(lane < 2 * FOX_HEADS, pltpu.roll(mid, FOX_HEADS, 1),
                  jnp.where(lane < 3 * FOX_HEADS, pltpu.roll(lo, 2 * FOX_HEADS, 1), 0.0)))
    o_ref[0] = packed.astype(BF16)


def _cumsum_pieces(x, *, tl):
    b, length, lanes = x.shape
    return pl.pallas_call(
        _cumsum_kernel,
        grid=(b, length // tl),
        in_specs=[pl.BlockSpec((1, tl, lanes), lambda i, t: (i, t, 0))],
        out_specs=pl.BlockSpec((1, tl, lanes), lambda i, t: (i, t, 0)),
        out_shape=jax.ShapeDtypeStruct((b, length, lanes), BF16),
        scratch_shapes=[pltpu.VMEM((1, lanes), F32)],
        compiler_params=_params("parallel", "arbitrary"),
        name="logf_cumsum",
    )(x)


def _fox_kernel(q_ref, k_ref, f_ref, v_ref, o_ref, qa_ref, z0_ref, z1_ref, mb0_ref, mb1_ref, m_ref, acc_ref,
                *, tq, tk, q_start, blocks):
    step = pl.program_id(2)
    rows = FOX_GROUP * tq
    lane = lax.broadcasted_iota(jnp.int32, (tq, FOX_HD), 1)
    for x in range(blocks):
        for g in range(FOX_GROUP):
            head = pl.program_id(1) * FOX_GROUP + g
            pick = jnp.where(lane < FOX_F_PIECES * FOX_HEADS,
                             jnp.where((lane & (FOX_HEADS - 1)) == head, -1.0, 0.0), 0.0)
            qa_ref[x, g * tq:(g + 1) * tq, :FOX_HD] = q_ref[0, x * tq:(x + 1) * tq, g * FOX_HD:(g + 1) * FOX_HD]
            qa_ref[x, g * tq:(g + 1) * tq, FOX_HD:] = pick.astype(BF16)
    n_full = (q_start + step * blocks * tq) // tk

    m_ref[...] = jnp.full_like(m_ref, MASK_VALUE)
    acc_ref[...] = jnp.zeros_like(acc_ref)

    cw = min(rows, FOX_COL_CHUNK)
    chunks = [slice(c * cw, (c + 1) * cw) for c in range(rows // cw)]
    buffers = ((z0_ref, mb0_ref), (z1_ref, mb1_ref))

    def score(x, kb, masked, buf):
        z_ref, mb_ref = buffers[buf]
        start = pl.multiple_of(kb * tk, tk)
        ka = jnp.concatenate([k_ref[0, 0, pl.ds(start, tk), :], f_ref[0, pl.ds(start, tk), :]], axis=1)
        for c, cs in enumerate(chunks):
            z = lax.dot_general(ka, qa_ref[x, cs, :], (((1,), (1,)), ((), ())), preferred_element_type=F32)
            if masked:
                kpos = kb * tk + lax.broadcasted_iota(jnp.int32, (tk, cw), 0)
                col = c * cw + lax.broadcasted_iota(jnp.int32, (tk, cw), 1)
                qpos = q_start + (step * blocks + x) * tq + (col & (tq - 1))
                z = jnp.where(kpos <= qpos, z, MASK_VALUE)
            z_ref[:, cs] = z
            mb_ref[:, cs] = jnp.max(z, axis=0, keepdims=True)

    def accumulate(x, kb, buf):
        z_ref, mb_ref = buffers[buf]
        vt = v_ref[0, 0, kb]
        for cs in chunks:
            m_old = m_ref[x, :, cs]
            m_new = jnp.maximum(m_old, mb_ref[:, cs])
            alpha = jnp.exp2(m_old - m_new)
            p = jnp.exp2(z_ref[:, cs] - m_new)
            acc_ref[x, :, cs] = alpha * acc_ref[x, :, cs] + jnp.dot(vt, p.astype(BF16), preferred_element_type=F32)
            m_ref[x, :, cs] = m_new

    if blocks == 4:
        def run(stages, pending):
            for x, kb, masked in stages:
                buf = 1 - pending[2]
                score(x, kb, masked, buf)
                accumulate(*pending)
                pending = (x, kb, buf)
            return pending

        score(0, n_full, True, 0)
        pending = run([(1, n_full, True), (2, n_full + 1, True), (3, n_full + 1, True),
                       (2, n_full, False), (3, n_full, False)], (0, n_full, 0))
        last_block, _, last_buf = pending

        def trip(i, kb_pending):
            stages = [(x, FOX_TILES_PER_TRIP * i + t, False) for t in range(FOX_TILES_PER_TRIP) for x in range(blocks)]
            return run(stages, (last_block, kb_pending, last_buf))[1]

        kb_pending = lax.fori_loop(0, n_full // FOX_TILES_PER_TRIP, trip, n_full)
        accumulate(last_block, kb_pending, last_buf)
    else:
        score(0, n_full, True, 0)

        def pair(i, pending):
            score(0, 2 * i, False, 1)
            accumulate(0, pending, 0)
            score(0, 2 * i + 1, False, 0)
            accumulate(0, 2 * i, 1)
            return 2 * i + 1

        pending = lax.fori_loop(0, n_full // 2, pair, n_full)

        @pl.when(n_full % 2 == 1)
        def _():
            score(0, n_full - 1, False, 1)
            accumulate(0, pending, 0)
            accumulate(0, n_full - 1, 1)

        @pl.when(n_full % 2 == 0)
        def _():
            accumulate(0, pending, 0)

    for x in range(blocks):
        acc = acc_ref[x]
        out = (acc[:FOX_HD] / acc[FOX_HD:FOX_HD + 1]).T
        for g in range(FOX_GROUP):
            o_ref[0, x * tq:(x + 1) * tq, g * FOX_HD:(g + 1) * FOX_HD] = out[g * tq:(g + 1) * tq, :].astype(BF16)


def _fox_attention(q, k_heads, f_pieces, v_t, *, tq, tk, q_start):
    b, lq, _ = q.shape
    lk = k_heads.shape[2]
    nk = lk // tk
    rows = FOX_GROUP * tq
    span = FOX_TILES_PER_TRIP * tk
    blocks = FOX_BLOCKS_PER_STEP if (FOX_BLOCKS_PER_STEP * tq == span and lq % span == 0 and q_start % span == 0) else 1
    assert tq & (tq - 1) == 0 and tk % tq == 0 and q_start % tk == 0
    return pl.pallas_call(
        functools.partial(_fox_kernel, tq=tq, tk=tk, q_start=q_start, blocks=blocks),
        grid=(b, FOX_KV_HEADS, lq // (blocks * tq)),
        in_specs=[
            pl.BlockSpec((1, blocks * tq, FOX_GROUP * FOX_HD), lambda i, h, t: (i, t, h)),
            pl.BlockSpec((1, 1, lk, FOX_HD), lambda i, h, t: (h, i, 0, 0)),
            pl.BlockSpec((1, lk, LANES), lambda i, h, t: (i, 0, 0)),
            pl.BlockSpec((1, 1, nk, FOX_V_ROWS, tk), lambda i, h, t: (h, i, 0, 0, 0)),
        ],
        out_specs=pl.BlockSpec((1, blocks * tq, FOX_GROUP * FOX_HD), lambda i, h, t: (i, t, h)),
        out_shape=jax.ShapeDtypeStruct(q.shape, BF16),
        scratch_shapes=[
            pltpu.VMEM((blocks, rows, 2 * FOX_HD), BF16),
            pltpu.VMEM((tk, rows), F32),
            pltpu.VMEM((tk, rows), F32),
            pltpu.VMEM((1, rows), F32),
            pltpu.VMEM((1, rows), F32),
            pltpu.VMEM((blocks, 1, rows), F32),
            pltpu.VMEM((blocks, FOX_V_ROWS, rows), F32),
        ],
        compiler_params=_params("parallel", "parallel", "arbitrary"),
        name="fox_attention",
    )(q, k_heads, f_pieces, v_t)


def _rope_tables(pos):
    half = RET_DK // 2
    inv = ROPE_BASE ** (-jnp.arange(half, dtype=F32) / half)
    ang = pos.astype(F32)[:, None] * inv[None, :]
    return jnp.cos(ang), jnp.sin(ang)


def _with_cache(new_heads, cache, lk_pad):
    past = cache.transpose(2, 0, 1, 3).astype(BF16)
    both = jnp.concatenate([past, new_heads], axis=2)
    return jnp.pad(both, ((0, 0), (0, 0), (0, lk_pad - both.shape[2]), (0, 0)))


def _transpose_values(v_heads, tk):
    kvh, b, lk, _ = v_heads.shape
    v_t = v_heads.reshape(kvh, b, lk // tk, tk, FOX_HD).transpose(0, 1, 2, 4, 3)
    ones = jnp.ones((kvh, b, lk // tk, FOX_V_ROWS - FOX_HD, tk), BF16)
    return jnp.concatenate([v_t, ones], axis=3)


def _stream(x, *, pos, s0, cache, weights, later_weights, tm, ffn_tm, chunk, tq):
    norm_mix, norm_ffn, norm_kv, norm_final, w_ret_in, w_ret_o, w_kv, b_kv, w_fox_q, w_fox_o = weights
    b, length, d = x.shape
    m = b * length
    h = x.reshape(m, d)

    cos, sin = _rope_tables(pos)
    if cos.shape[0] < tm:
        cos = jnp.tile(cos, (tm // cos.shape[0], 1))
        sin = jnp.tile(sin, (tm // sin.shape[0], 1))
    if later_weights[0].dtype == BF16:
        proj, _ = _ret_in(h, norm_mix[0:1], cos, sin, w_ret_in, tm=tm)
    else:
        proj, later_weights = _ret_in(h, norm_mix[0:1], cos, sin, w_ret_in, later_weights, tm=tm)
    w_ffn_gate, w_ffn_up, w_ffn_down = later_weights
    o, s_fin = _retention(proj, s0, batch=b, seq=length, chunk=chunk)
    h = _mm_res(o, w_ret_o, h, tm=tm, tn=COL_TILE)
    h = _ffn(h, norm_ffn[0:1], w_ffn_gate, w_ffn_up, w_ffn_down, layer=0, tm=ffn_tm)

    tk = FOX_K_TILE
    if cache is None:
        q_start = 0
        k32, v32, k_heads, v_t, lf_pad, lf = _kv_proj(h, norm_kv[None, :], w_kv, b_kv, tm=tm, key_tile=tk)
        k_heads = k_heads.reshape(FOX_KV_HEADS, b, length, FOX_HD)
        v_t = v_t.reshape(FOX_KV_HEADS, b, length // tk, FOX_V_ROWS, tk)
        lf_all = lf_pad.reshape(b, length, LANES)
    else:
        cache_k, cache_v, cache_logf = cache
        q_start = cache_k.shape[1]
        lk_pad = -(-(q_start + length) // tk) * tk
        k32, v32, k_heads, v_heads, lf_pad, lf = _kv_proj(h, norm_kv[None, :], w_kv, b_kv, tm=tm, key_tile=None)
        k_heads = _with_cache(k_heads.reshape(FOX_KV_HEADS, b, length, FOX_HD), cache_k, lk_pad)
        v_t = _transpose_values(_with_cache(v_heads.reshape(FOX_KV_HEADS, b, length, FOX_HD), cache_v, lk_pad), tk)
        lf_all = jnp.concatenate([
            jnp.pad(cache_logf.astype(F32), ((0, 0), (0, 0), (0, LANES - FOX_HEADS))),
            lf_pad.reshape(b, length, LANES)], axis=1)
        lf_all = jnp.pad(lf_all, ((0, 0), (0, lk_pad - lf_all.shape[1]), (0, 0)))
    lf = lf.reshape(b, length, FOX_HEADS)
    f_pieces = _cumsum_pieces(lf_all, tl=CUMSUM_TILE)

    q = _q_proj(h, norm_mix[1:2], w_fox_q, tm=tm).reshape(b, length, d)
    att = _fox_attention(q, k_heads, f_pieces, v_t, tq=tq, tk=tk, q_start=q_start)
    h = _mm_res(att.reshape(m, d), w_fox_o, h, tm=min(tm, ROW_TILE // 2), tn=d)
    y = _ffn(h, norm_ffn[1:2], w_ffn_gate, w_ffn_up, w_ffn_down, norm_final[None, :], layer=1, tm=ffn_tm)

    outputs = (y.reshape(b, length, d), s_fin[None],
               k32.reshape(b, length, FOX_KV_HEADS, FOX_HD), v32.reshape(b, length, FOX_KV_HEADS, FOX_HD), lf)
    return outputs, later_weights


def kernel(x_prompt, x_sample, state_ret, cache_k, cache_v, cache_logf, norm_mix, norm_ffn, norm_kv, norm_final, w_ret_in, w_ret_o, w_kv_k, w_kv_v, w_kv_f, b_kv_f, w_fox_q, w_fox_o, w_ffn_gate, w_ffn_up, w_ffn_down):
    lp = x_prompt.shape[1]
    bs, ls, _ = x_sample.shape
    past = cache_k.shape[1]

    w_kv = jnp.concatenate(
        [w_kv_k, w_kv_v, jnp.pad(w_kv_f, ((0, 0), (0, LANES - FOX_HEADS)))], axis=1).astype(BF16)
    b_kv = jnp.pad(b_kv_f.astype(F32), (0, LANES - FOX_HEADS))[None, :]
    weights = (norm_mix.astype(F32), norm_ffn.astype(F32), norm_kv.astype(F32), norm_final.astype(F32),
               w_ret_in[0].astype(BF16), w_ret_o[0].astype(BF16), w_kv, b_kv,
               w_fox_q[0].astype(BF16), w_fox_o[0].astype(BF16))
    later_weights = tuple(w.astype(F32) for w in (w_ffn_gate, w_ffn_up, w_ffn_down))

    (y_p, s_p, k_p, v_p, lf_p), later_weights = _stream(
        x_prompt, pos=jnp.arange(lp), s0=None, cache=None, weights=weights, later_weights=later_weights,
        tm=ROW_TILE, ffn_tm=FFN_ROW_TILE, chunk=RET_CHUNK, tq=FOX_Q_TILE)
    (y_s, s_s, k_s, v_s, lf_s), _ = _stream(
        x_sample, pos=past + jnp.arange(ls), s0=state_ret[0], cache=(cache_k, cache_v, cache_logf),
        weights=weights, later_weights=later_weights, tm=bs * ls, ffn_tm=bs * ls, chunk=ls, tq=ls)
    return (y_p, y_s, s_p, k_p, v_p, lf_p, s_s, k_s, v_s, lf_s)
```

```python
import functools

import jax
import jax.numpy as jnp
from jax import lax
from jax.experimental import pallas as pl
from jax.experimental.pallas import tpu as pltpu

F32 = jnp.float32
BF16 = jnp.bfloat16

D_MODEL = 2048
RET_HEADS = 8
RET_DK = D_MODEL // RET_HEADS
RET_DV = 2 * D_MODEL // RET_HEADS
RET_QK = RET_HEADS * RET_DK
RET_V = RET_HEADS * RET_DV
RET_HEAD_COLS = 2 * RET_DK + 2 * RET_DV
ROPE_BASE = 10000.0
FOX_HEADS = 16
FOX_HD = D_MODEL // FOX_HEADS
FOX_KV_HEADS = 4
FOX_GROUP = FOX_HEADS // FOX_KV_HEADS
FOX_KV = FOX_KV_HEADS * FOX_HD
FOX_F_PIECES = 3
FOX_V_ROWS = FOX_HD + 16
LOG2_E = 1.4426950408889634
NORM_EPS = 1e-6
GN_EPS = 1e-5

LANES = 128
BF16_SUBLANES = 16
VMEM_LIMIT_BYTES = 60 * 1024 * 1024

ROW_TILE = 1024
COL_TILE = 1024
ROW_GROUPS = 2
FFN_ROW_TILE = 1024
FFN_COL_TILE = 512
FFN_COL_TILE_SMALL_ROWS = 512
RET_CHUNK = 256
RET_HEADS_PER_STEP = 8
FOX_Q_TILE = 256
FOX_K_TILE = 512
FOX_BLOCKS_PER_STEP = 4
FOX_TILES_PER_TRIP = 2
FOX_COL_CHUNK = 512
CUMSUM_TILE = 512
MASK_VALUE = -1e30


def _params(*semantics):
    return pltpu.CompilerParams(dimension_semantics=semantics, vmem_limit_bytes=VMEM_LIMIT_BYTES)


def _rmsnorm(x, gain):
    y = x * lax.rsqrt(jnp.mean(x * x, axis=-1, keepdims=True) + NORM_EPS)
    return y * gain


def _silu(x):
    return x * (1.0 / (1.0 + jnp.exp(-x)))


def _ret_in_kernel(x_ref, gain_ref, cos_ref, sin_ref, wq_ref, wk_ref, wv_ref, wg_ref, *rest, n_cast):
    cast_in = rest[:n_cast]
    o_ref = rest[n_cast]
    cast_out = rest[n_cast + 1:2 * n_cast + 1]
    xn_ref = rest[2 * n_cast + 1]

    @pl.when(pl.program_id(1) == 0)
    def _():
        xn_ref[...] = _rmsnorm(x_ref[...], gain_ref[...]).astype(BF16)

    for src, dst in zip(cast_in, cast_out):
        dst[...] = src[...].astype(BF16)

    w = jnp.concatenate([wg_ref[...], wq_ref[...], wk_ref[...], wv_ref[...]], axis=1)
    y = jnp.dot(xn_ref[...], w, preferred_element_type=F32)
    cos = cos_ref[...]
    sin = sin_ref[...]
    half = RET_DK // 2
    v_first = 2 * RET_DK
    g_first = v_first + RET_DV
    o_ref[:, g_first:] = _silu(y[:, :RET_DV]).astype(BF16)
    for src, dst, scale in ((RET_DV, 0, 1.0), (RET_DV + RET_DK, RET_DK, RET_DK ** -0.5)):
        x1 = y[:, src:src + half]
        x2 = y[:, src + half:src + RET_DK]
        o_ref[:, dst:dst + half] = ((x1 * cos - x2 * sin) * scale).astype(BF16)
        o_ref[:, dst + half:dst + RET_DK] = ((x1 * sin + x2 * cos) * scale).astype(BF16)
    o_ref[:, v_first:g_first] = y[:, RET_DV + 2 * RET_DK:].astype(BF16)


def _slab_spec(shape, steps):
    layers, rows, cols = shape
    slab = next(r for r in range(BF16_SUBLANES, rows + 1, BF16_SUBLANES)
                if rows % r == 0 and layers * rows // r <= steps)
    per_layer = rows // slab
    last = layers * per_layer - 1

    def index(i, j):
        b = jnp.minimum(i * RET_HEADS + j, last)
        return (b // per_layer, b % per_layer, 0)

    return pl.BlockSpec((None, slab, cols), index)


def _ret_in(x, gain, cos, sin, w, cast=(), *, tm):
    m, d = x.shape
    pos_tiles = cos.shape[0] // tm
    k_first = RET_QK // RET_DK
    v_first = 2 * RET_QK // RET_DV
    g_first = (2 * RET_QK + RET_V) // RET_DV
    grid = (m // tm, RET_HEADS)
    cast_specs = [_slab_spec(c.shape, grid[0] * grid[1]) for c in cast]
    outs = pl.pallas_call(
        functools.partial(_ret_in_kernel, n_cast=len(cast)),
        grid=grid,
        in_specs=[
            pl.BlockSpec((tm, d), lambda i, j: (i, 0)),
            pl.BlockSpec((1, d), lambda i, j: (0, 0)),
            pl.BlockSpec((tm, RET_DK // 2), lambda i, j: (i % pos_tiles, 0)),
            pl.BlockSpec((tm, RET_DK // 2), lambda i, j: (i % pos_tiles, 0)),
            pl.BlockSpec((d, RET_DK), lambda i, j: (0, j)),
            pl.BlockSpec((d, RET_DK), lambda i, j: (0, k_first + j)),
            pl.BlockSpec((d, RET_DV), lambda i, j: (0, v_first + j)),
            pl.BlockSpec((d, RET_DV), lambda i, j: (0, g_first + j)),
        ] + cast_specs,
        out_specs=[pl.BlockSpec((tm, RET_HEAD_COLS), lambda i, j: (i, j))] + cast_specs,
        out_shape=[jax.ShapeDtypeStruct((m, RET_HEADS * RET_HEAD_COLS), BF16)]
        + [jax.ShapeDtypeStruct(c.shape, BF16) for c in cast],
        scratch_shapes=[pltpu.VMEM((tm, d), BF16)],
        compiler_params=_params("arbitrary", "arbitrary"),
        name="ret_in_proj",
    )(x, gain, cos, sin, w, w, w, w, *cast)
    return outs[0], tuple(outs[1:])


def _retention_kernel(*refs, chunk, has_init):
    if has_init:
        p_ref, s0_ref, o_ref, s_ref, dm_ref, rd_ref, kw_ref, raw_ref = refs
    else:
        p_ref, o_ref, s_ref, dm_ref, rd_ref, kw_ref, raw_ref = refs
        s0_ref = None
    c = pl.program_id(2)
    heads = RET_HEADS_PER_STEP

    def log_gamma(hh):
        head = pl.program_id(1) * heads + hh
        return jnp.log(1.0 - jnp.exp2(-5.0 - jnp.full((1, 1), head, jnp.int32).astype(F32)))

    @pl.when(c == 0)
    def _():
        li = lax.broadcasted_iota(jnp.int32, (chunk, chunk), 0)
        mi = lax.broadcasted_iota(jnp.int32, (chunk, chunk), 1)
        diff = (li - mi).astype(F32)
        idx = lax.broadcasted_iota(jnp.int32, (chunk, 1), 0).astype(F32)
        for hh in range(heads):
            lg = log_gamma(hh)
            dm_ref[hh] = jnp.where(diff >= 0, jnp.exp(lg * jnp.maximum(diff, 0.0)), 0.0)
            rd_ref[hh] = jnp.exp(lg * (idx + 1.0))
            kw_ref[hh] = jnp.exp(lg * (chunk - 1.0 - idx))
        if has_init:
            s_ref[...] = s0_ref[...]
        else:
            s_ref[...] = jnp.zeros_like(s_ref)

    for hh in range(heads):
        base = hh * RET_HEAD_COLS
        q = p_ref[:, base:base + RET_DK]
        k = p_ref[:, base + RET_DK:base + 2 * RET_DK]
        v = p_ref[:, base + 2 * RET_DK:base + 2 * RET_DK + RET_DV]
        s_old = s_ref[0, hh]
        scores = lax.dot_general(q, k, (((1,), (1,)), ((), ())), preferred_element_type=F32) * dm_ref[hh]
        inner = jnp.dot(scores.astype(BF16), v, preferred_element_type=F32)
        cross = jnp.dot(q, s_old.astype(BF16), preferred_element_type=F32) * rd_ref[hh]
        raw_ref[hh] = inner + cross
        kd = (k.astype(F32) * kw_ref[hh]).astype(BF16)
        s_ref[0, hh] = jnp.exp(log_gamma(hh) * float(chunk)) * s_old + lax.dot_general(
            kd, v, (((0,), (0,)), ((), ())), preferred_element_type=F32)

    for hh in range(heads):
        o = raw_ref[hh]
        mu = jnp.mean(o, axis=-1, keepdims=True)
        cen = o - mu
        var = jnp.mean(cen * cen, axis=-1, keepdims=True)
        on = cen * lax.rsqrt(var + GN_EPS)
        gate = p_ref[:, (hh + 1) * RET_HEAD_COLS - RET_DV:(hh + 1) * RET_HEAD_COLS].astype(F32)
        o_ref[:, hh * RET_DV:(hh + 1) * RET_DV] = (on * gate).astype(BF16)


def _retention(proj, s0, *, batch, seq, chunk):
    nc = seq // chunk
    has_init = s0 is not None
    heads = RET_HEADS_PER_STEP
    v_w = heads * RET_DV
    in_specs = [pl.BlockSpec((chunk, heads * RET_HEAD_COLS), lambda b, h, c: (b * nc + c, h))]
    args = [proj]
    if has_init:
        in_specs.append(pl.BlockSpec((1, heads, RET_DK, RET_DV), lambda b, h, c: (b, h, 0, 0)))
        args.append(s0)
    return pl.pallas_call(
        functools.partial(_retention_kernel, chunk=chunk, has_init=has_init),
        grid=(batch, RET_HEADS // heads, nc),
        in_specs=in_specs,
        out_specs=[
            pl.BlockSpec((chunk, v_w), lambda b, h, c: (b * nc + c, h)),
            pl.BlockSpec((1, heads, RET_DK, RET_DV), lambda b, h, c: (b, h, 0, 0)),
        ],
        out_shape=[
            jax.ShapeDtypeStruct((batch * seq, RET_V), BF16),
            jax.ShapeDtypeStruct((batch, RET_HEADS, RET_DK, RET_DV), F32),
        ],
        scratch_shapes=[
            pltpu.VMEM((heads, chunk, chunk), F32),
            pltpu.VMEM((heads, chunk, 1), F32),
            pltpu.VMEM((heads, chunk, 1), F32),
            pltpu.VMEM((heads, chunk, RET_DV), F32),
        ],
        compiler_params=_params("parallel", "parallel", "arbitrary"),
        name="retention",
    )(*args)


def _ret_fused_kernel(x_ref, gain_ref, cos_ref, sin_ref, wq_ref, wk_ref, wv_ref, wg_ref, *rest,
                      n_cast, chunk, tiles_per_seq):
    cast_in = rest[:n_cast]
    o_ref, state_ref = rest[n_cast:n_cast + 2]
    cast_out = rest[n_cast + 2:2 * n_cast + 2]
    xn_ref, p_ref, dm_ref, rd_ref, kw_ref = rest[2 * n_cast + 2:]
    s_ref = state_ref.at[0]
    i = pl.program_id(0)
    head = pl.program_id(1)
    lg = jnp.log(1.0 - jnp.exp2(-5.0 - jnp.full((1, 1), head, jnp.int32).astype(F32)))

    @pl.when(head == 0)
    def _():
        xn_ref[...] = _rmsnorm(x_ref[...], gain_ref[...]).astype(BF16)

    @pl.when(i == 0)
    def _():
        li = lax.broadcasted_iota(jnp.int32, (chunk, chunk), 0)
        mi = lax.broadcasted_iota(jnp.int32, (chunk, chunk), 1)
        diff = (li - mi).astype(F32)
        idx = lax.broadcasted_iota(jnp.int32, (chunk, 1), 0).astype(F32)
        dm_ref[head] = jnp.where(diff >= 0, jnp.exp(lg * jnp.maximum(diff, 0.0)), 0.0)
        rd_ref[head] = jnp.exp(lg * (idx + 1.0))
        kw_ref[head] = jnp.exp(lg * (chunk - 1.0 - idx))

    @pl.when(i % tiles_per_seq == 0)
    def _():
        s_ref[head] = jnp.zeros((RET_DK, RET_DV), F32)

    for src, dst in zip(cast_in, cast_out):
        dst[...] = src[...].astype(BF16)

    w = jnp.concatenate([wg_ref[...], wq_ref[...], wk_ref[...], wv_ref[...]], axis=1)
    half = RET_DK // 2
    v_first = 2 * RET_DK
    g_first = v_first + RET_DV

    def project(rows):
        y = jnp.dot(xn_ref[rows, :], w, preferred_element_type=F32)
        cos = cos_ref[rows, :]
        sin = sin_ref[rows, :]
        p_ref[rows, g_first:] = _silu(y[:, :RET_DV]).astype(BF16)
        for src, dst, scale in ((RET_DV, 0, 1.0), (RET_DV + RET_DK, RET_DK, RET_DK ** -0.5)):
            x1 = y[:, src:src + half]
            x2 = y[:, src + half:src + RET_DK]
            p_ref[rows, dst:dst + half] = ((x1 * cos - x2 * sin) * scale).astype(BF16)
            p_ref[rows, dst + half:dst + RET_DK] = ((x1 * sin + x2 * cos) * scale).astype(BF16)
        p_ref[rows, v_first:g_first] = y[:, RET_DV + 2 * RET_DK:].astype(BF16)

    def recur(rows):
        q = p_ref[rows, :RET_DK]
        k = p_ref[rows, RET_DK:v_first]
        v = p_ref[rows, v_first:g_first]
        s_old = s_ref[head]
        scores = lax.dot_general(q, k, (((1,), (1,)), ((), ())), preferred_element_type=F32) * dm_ref[head]
        inner = jnp.dot(scores.astype(BF16), v, preferred_element_type=F32)
        cross = jnp.dot(q, s_old.astype(BF16), preferred_element_type=F32) * rd_ref[head]
        kd = (k.astype(F32) * kw_ref[head]).astype(BF16)
        s_ref[head] = jnp.exp(lg * float(chunk)) * s_old + lax.dot_general(
            kd, v, (((0,), (0,)), ((), ())), preferred_element_type=F32)
        o = inner + cross
        mu = jnp.mean(o, axis=-1, keepdims=True)
        cen = o - mu
        var = jnp.mean(cen * cen, axis=-1, keepdims=True)
        on = cen * lax.rsqrt(var + GN_EPS)
        o_ref[rows, :] = (on * p_ref[rows, g_first:].astype(F32)).astype(BF16)

    groups = [slice(g * chunk, (g + 1) * chunk) for g in range(x_ref.shape[0] // chunk)]
    project(groups[0])
    for g in range(1, len(groups)):
        project(groups[g])
        recur(groups[g - 1])
    recur(groups[-1])


def _ret_fused(x, gain, cos, sin, w, cast=(), *, batch, seq, tm, chunk):
    m, d = x.shape
    pos_tiles = cos.shape[0] // tm
    tiles_per_seq = seq // tm
    k_first = RET_QK // RET_DK
    v_first = 2 * RET_QK // RET_DV
    g_first = (2 * RET_QK + RET_V) // RET_DV
    grid = (m // tm, RET_HEADS)
    cast_specs = [_slab_spec(c.shape, grid[0] * grid[1]) for c in cast]
    outs = pl.pallas_call(
        functools.partial(_ret_fused_kernel, n_cast=len(cast), chunk=chunk, tiles_per_seq=tiles_per_seq),
        grid=grid,
        in_specs=[
            pl.BlockSpec((tm, d), lambda i, j: (i, 0)),
            pl.BlockSpec((1, d), lambda i, j: (0, 0)),
            pl.BlockSpec((tm, RET_DK // 2), lambda i, j: (i % pos_tiles, 0)),
            pl.BlockSpec((tm, RET_DK // 2), lambda i, j: (i % pos_tiles, 0)),
            pl.BlockSpec((d, RET_DK), lambda i, j: (0, j)),
            pl.BlockSpec((d, RET_DK), lambda i, j: (0, k_first + j)),
            pl.BlockSpec((d, RET_DV), lambda i, j: (0, v_first + j)),
            pl.BlockSpec((d, RET_DV), lambda i, j: (0, g_first + j)),
        ] + cast_specs,
        out_specs=[
            pl.BlockSpec((tm, RET_DV), lambda i, j: (i, j)),
            pl.BlockSpec((1, RET_HEADS, RET_DK, RET_DV), lambda i, j: (i // tiles_per_seq, 0, 0, 0)),
        ] + cast_specs,
        out_shape=[
            jax.ShapeDtypeStruct((m, RET_V), BF16),
            jax.ShapeDtypeStruct((batch, RET_HEADS, RET_DK, RET_DV), F32),
        ] + [jax.ShapeDtypeStruct(c.shape, BF16) for c in cast],
        scratch_shapes=[
            pltpu.VMEM((tm, d), BF16),
            pltpu.VMEM((tm, RET_HEAD_COLS), BF16),
            pltpu.VMEM((RET_HEADS, chunk, chunk), F32),
            pltpu.VMEM((RET_HEADS, chunk, 1), F32),
            pltpu.VMEM((RET_HEADS, chunk, 1), F32),
        ],
        compiler_params=_params("arbitrary", "arbitrary"),
        name="ret_fused",
    )(x, gain, cos, sin, w, w, w, w, *cast)
    return outs[0], outs[1], tuple(outs[2:])


def _mm_res_kernel(a_ref, w_ref, r_ref, o_ref):
    o_ref[...] = r_ref[...] + jnp.dot(a_ref[...], w_ref[...], preferred_element_type=F32)


def _mm_res(a, w, res, *, tm, tn):
    m, kdim = a.shape
    n = w.shape[1]
    return pl.pallas_call(
        _mm_res_kernel,
        grid=(m // tm, n // tn),
        in_specs=[
            pl.BlockSpec((tm, kdim), lambda i, j: (i, 0)),
            pl.BlockSpec((kdim, tn), lambda i, j: (0, j)),
            pl.BlockSpec((tm, tn), lambda i, j: (i, j)),
        ],
        out_specs=pl.BlockSpec((tm, tn), lambda i, j: (i, j)),
        out_shape=jax.ShapeDtypeStruct((m, n), F32),
        compiler_params=_params("parallel", "arbitrary"),
        name="matmul_residual",
    )(a, w, res)


def _ffn_kernel(*refs, final_norm):
    if final_norm:
        x_ref, gain_ref, wg_ref, wu_ref, wd_ref, fgain_ref, o_ref, xn_ref = refs
    else:
        x_ref, gain_ref, wg_ref, wu_ref, wd_ref, o_ref, xn_ref = refs
    j = pl.program_id(1)

    @pl.when(j == 0)
    def _():
        x = x_ref[...]
        xn_ref[...] = _rmsnorm(x, gain_ref[...]).astype(BF16)
        o_ref[...] = x

    xn = xn_ref[...]
    gate = jnp.dot(xn, wg_ref[...], preferred_element_type=F32)
    up = jnp.dot(xn, wu_ref[...], preferred_element_type=F32)
    hidden = (_silu(gate) * up).astype(BF16)
    o_ref[...] += jnp.dot(hidden, wd_ref[...], preferred_element_type=F32)

    if final_norm:
        @pl.when(j == pl.num_programs(1) - 1)
        def _():
            o_ref[...] = _rmsnorm(o_ref[...], fgain_ref[...])


def _ffn(x, gain, wg, wu, wd, final_gain=None, *, layer, tm):
    m, d = x.shape
    f = wg.shape[2]
    tf = FFN_COL_TILE if tm >= FFN_ROW_TILE else FFN_COL_TILE_SMALL_ROWS
    final_norm = final_gain is not None
    in_specs = [
        pl.BlockSpec((tm, d), lambda i, j: (i, 0)),
        pl.BlockSpec((1, d), lambda i, j: (0, 0)),
        pl.BlockSpec((None, d, tf), lambda i, j: (layer, 0, j)),
        pl.BlockSpec((None, d, tf), lambda i, j: (layer, 0, j)),
        pl.BlockSpec((None, tf, d), lambda i, j: (layer, j, 0)),
    ]
    args = [x, gain, wg, wu, wd]
    if final_norm:
        in_specs.append(pl.BlockSpec((1, d), lambda i, j: (0, 0)))
        args.append(final_gain)
    return pl.pallas_call(
        functools.partial(_ffn_kernel, final_norm=final_norm),
        grid=(m // tm, f // tf),
        in_specs=in_specs,
        out_specs=pl.BlockSpec((tm, d), lambda i, j: (i, 0)),
        out_shape=jax.ShapeDtypeStruct((m, d), F32),
        scratch_shapes=[pltpu.VMEM((tm, d), BF16)],
        compiler_params=_params("parallel", "arbitrary"),
        name="ffn",
    )(*args)


def _kv_kernel(x_ref, gain_ref, w_ref, bias_ref, k32_ref, v32_ref, kh_ref, vh_ref, lf_ref, lf16_ref, *, key_tile):
    rows = x_ref.shape[0] // ROW_GROUPS
    for r in range(ROW_GROUPS):
        rs = slice(r * rows, (r + 1) * rows)
        xn = _rmsnorm(x_ref[rs, :], gain_ref[...]).astype(BF16)
        y = jnp.dot(xn, w_ref[...], preferred_element_type=F32)
        k = y[:, :FOX_KV]
        v = y[:, FOX_KV:2 * FOX_KV]
        for h in range(FOX_KV_HEADS):
            hs = slice(h * FOX_HD, (h + 1) * FOX_HD)
            out_rows = pl.ds(r * rows * FOX_KV_HEADS + h, rows, stride=FOX_KV_HEADS)
            k32_ref[out_rows, :] = k[:, hs]
            v32_ref[out_rows, :] = v[:, hs]
            kh_ref[h, rs, :] = k[:, hs].astype(BF16)
            if key_tile is None:
                vh_ref[h, rs, :] = v[:, hs].astype(BF16)
            else:
                v_t = v[:, hs].T.astype(BF16)
                for t in range(rows // key_tile):
                    tile = r * (rows // key_tile) + t
                    vh_ref[h, tile, :FOX_HD, :] = v_t[:, t * key_tile:(t + 1) * key_tile]
                    vh_ref[h, tile, FOX_HD:, :] = jnp.ones((FOX_V_ROWS - FOX_HD, key_tile), BF16)
        z = y[:, 2 * FOX_KV:] + bias_ref[...]
        lf = jnp.minimum(z, 0.0) - jnp.log1p(jnp.exp(-jnp.abs(z)))
        lf_ref[rs, :] = lf
        lf16_ref[rs, :] = lf[:, :FOX_HEADS]


def _kv_proj(x, gain, w, bias, *, tm, key_tile):
    m, d = x.shape
    n = w.shape[1]
    if key_tile is None:
        v_spec = pl.BlockSpec((FOX_KV_HEADS, tm, FOX_HD), lambda i: (0, i, 0))
        v_shape = jax.ShapeDtypeStruct((FOX_KV_HEADS, m, FOX_HD), BF16)
    else:
        v_spec = pl.BlockSpec((FOX_KV_HEADS, tm // key_tile, FOX_V_ROWS, key_tile), lambda i: (0, i, 0, 0))
        v_shape = jax.ShapeDtypeStruct((FOX_KV_HEADS, m // key_tile, FOX_V_ROWS, key_tile), BF16)
    return pl.pallas_call(
        functools.partial(_kv_kernel, key_tile=key_tile),
        grid=(m // tm,),
        in_specs=[
            pl.BlockSpec((tm, d), lambda i: (i, 0)),
            pl.BlockSpec((1, d), lambda i: (0, 0)),
            pl.BlockSpec((d, n), lambda i: (0, 0)),
            pl.BlockSpec((1, LANES), lambda i: (0, 0)),
        ],
        out_specs=[
            pl.BlockSpec((tm * FOX_KV_HEADS, FOX_HD), lambda i: (i, 0)),
            pl.BlockSpec((tm * FOX_KV_HEADS, FOX_HD), lambda i: (i, 0)),
            pl.BlockSpec((FOX_KV_HEADS, tm, FOX_HD), lambda i: (0, i, 0)),
            v_spec,
            pl.BlockSpec((tm, LANES), lambda i: (i, 0)),
            pl.BlockSpec((tm, FOX_HEADS), lambda i: (i, 0)),
        ],
        out_shape=[
            jax.ShapeDtypeStruct((m * FOX_KV_HEADS, FOX_HD), F32),
            jax.ShapeDtypeStruct((m * FOX_KV_HEADS, FOX_HD), F32),
            jax.ShapeDtypeStruct((FOX_KV_HEADS, m, FOX_HD), BF16),
            v_shape,
            jax.ShapeDtypeStruct((m, LANES), F32),
            jax.ShapeDtypeStruct((m, FOX_HEADS), F32),
        ],
        compiler_params=_params("parallel"),
        name="kv_proj",
    )(x, gain, w, bias)


def _q_kernel(x_ref, gain_ref, w_ref, o_ref):
    rows = x_ref.shape[0] // ROW_GROUPS
    for r in range(ROW_GROUPS):
        rs = slice(r * rows, (r + 1) * rows)
        xn = _rmsnorm(x_ref[rs, :], gain_ref[...]).astype(BF16)
        y = jnp.dot(xn, w_ref[...], preferred_element_type=F32)
        o_ref[rs, :] = (y * (FOX_HD ** -0.5 * LOG2_E)).astype(BF16)


def _q_proj(x, gain, w, *, tm):
    m, d = x.shape
    n = w.shape[1]
    return pl.pallas_call(
        _q_kernel,
        grid=(m // tm,),
        in_specs=[
            pl.BlockSpec((tm, d), lambda i: (i, 0)),
            pl.BlockSpec((1, d), lambda i: (0, 0)),
            pl.BlockSpec((d, n), lambda i: (0, 0)),
        ],
        out_specs=pl.BlockSpec((tm, n), lambda i: (i, 0)),
        out_shape=jax.ShapeDtypeStruct((m, n), BF16),
        compiler_params=_params("parallel"),
        name="q_proj",
    )(x, gain, w)


def _cumsum_kernel(x_ref, o_ref, carry_ref):
    @pl.when(pl.program_id(1) == 0)
    def _():
        carry_ref[...] = jnp.zeros_like(carry_ref)

    x = x_ref[0]
    tl = x.shape[0]
    r = lax.broadcasted_iota(jnp.int32, (tl, tl), 0)
    c = lax.broadcasted_iota(jnp.int32, (tl, tl), 1)
    tri = jnp.where(c <= r, 1.0, 0.0).astype(BF16)
    x_hi = x.astype(BF16)
    x_rest = x - x_hi.astype(F32)
    x_mid = x_rest.astype(BF16)
    x_lo = (x_rest - x_mid.astype(F32)).astype(BF16)
    y = (jnp.dot(tri, x_lo, preferred_element_type=F32) + jnp.dot(tri, x_mid, preferred_element_type=F32)
         + jnp.dot(tri, x_hi, preferred_element_type=F32) + carry_ref[...])
    carry_ref[...] = y[tl - 1:tl, :]
    y2 = y * LOG2_E
    hi = y2.astype(BF16).astype(F32)
    rest = y2 - hi
    mid = rest.astype(BF16).astype(F32)
    lo = rest - mid
    lane = lax.broadcasted_iota(jnp.int32, y.shape, 1)
    packed = jnp.where(
        lane < FOX_HEADS, hi,
        jnp.where(lane < 2 * FOX_HEADS, pltpu.roll(mid, FOX_HEADS, 1),
                  jnp.where(lane < 3 * FOX_HEADS, pltpu.roll(lo, 2 * FOX_HEADS, 1), 0.0)))
    o_ref[0] = packed.astype(BF16)


def _cumsum_pieces(x, *, tl):
    b, length, lanes = x.shape
    return pl.pallas_call(
        _cumsum_kernel,
        grid=(b, length // tl),
        in_specs=[pl.BlockSpec((1, tl, lanes), lambda i, t: (i, t, 0))],
        out_specs=pl.BlockSpec((1, tl, lanes), lambda i, t: (i, t, 0)),
        out_shape=jax.ShapeDtypeStruct((b, length, lanes), BF16),
        scratch_shapes=[pltpu.VMEM((1, lanes), F32)],
        compiler_params=_params("parallel", "arbitrary"),
        name="logf_cumsum",
    )(x)


def _fox_kernel(q_ref, k_ref, f_ref, v_ref, o_ref, qa_ref, z0_ref, z1_ref, mb0_ref, mb1_ref, m_ref, acc_ref,
                *, tq, tk, q_start, blocks):
    step = pl.program_id(2)
    rows = FOX_GROUP * tq
    lane = lax.broadcasted_iota(jnp.int32, (tq, FOX_HD), 1)
    for x in range(blocks):
        for g in range(FOX_GROUP):
            head = pl.program_id(1) * FOX_GROUP + g
            pick = jnp.where(lane < FOX_F_PIECES * FOX_HEADS,
                             jnp.where((lane & (FOX_HEADS - 1)) == head, -1.0, 0.0), 0.0)
            qa_ref[x, g * tq:(g + 1) * tq, :FOX_HD] = q_ref[0, x * tq:(x + 1) * tq, g * FOX_HD:(g + 1) * FOX_HD]
            qa_ref[x, g * tq:(g + 1) * tq, FOX_HD:] = pick.astype(BF16)
    n_full = (q_start + step * blocks * tq) // tk

    m_ref[...] = jnp.full_like(m_ref, MASK_VALUE)
    acc_ref[...] = jnp.zeros_like(acc_ref)

    cw = min(rows, FOX_COL_CHUNK)
    chunks = [slice(c * cw, (c + 1) * cw) for c in range(rows // cw)]
    buffers = ((z0_ref, mb0_ref), (z1_ref, mb1_ref))

    def score(x, kb, masked, buf):
        z_ref, mb_ref = buffers[buf]
        start = pl.multiple_of(kb * tk, tk)
        ka = jnp.concatenate([k_ref[0, 0, pl.ds(start, tk), :], f_ref[0, pl.ds(start, tk), :]], axis=1)
        for c, cs in enumerate(chunks):
            z = lax.dot_general(ka, qa_ref[x, cs, :], (((1,), (1,)), ((), ())), preferred_element_type=F32)
            if masked:
                kpos = kb * tk + lax.broadcasted_iota(jnp.int32, (tk, cw), 0)
                col = c * cw + lax.broadcasted_iota(jnp.int32, (tk, cw), 1)
                qpos = q_start + (step * blocks + x) * tq + (col & (tq - 1))
                z = jnp.where(kpos <= qpos, z, MASK_VALUE)
            z_ref[:, cs] = z
            mb_ref[:, cs] = jnp.max(z, axis=0, keepdims=True)

    def accumulate(x, kb, buf):
        z_ref, mb_ref = buffers[buf]
        vt = v_ref[0, 0, kb]
        for cs in chunks:
            m_old = m_ref[x, :, cs]
            m_new = jnp.maximum(m_old, mb_ref[:, cs])
            alpha = jnp.exp2(m_old - m_new)
            p = jnp.exp2(z_ref[:, cs] - m_new)
            acc_ref[x, :, cs] = alpha * acc_ref[x, :, cs] + jnp.dot(vt, p.astype(BF16), preferred_element_type=F32)
            m_ref[x, :, cs] = m_new

    if blocks == 4:
        def run(stages, pending):
            for x, kb, masked in stages:
                buf = 1 - pending[2]
                score(x, kb, masked, buf)
                accumulate(*pending)
                pending = (x, kb, buf)
            return pending

        score(0, n_full, True, 0)
        pending = run([(1, n_full, True), (2, n_full + 1, True), (3, n_full + 1, True),
                       (2, n_full, False), (3, n_full, False)], (0, n_full, 0))
        last_block, _, last_buf = pending

        def trip(i, kb_pending):
            stages = [(x, FOX_TILES_PER_TRIP * i + t, False) for t in range(FOX_TILES_PER_TRIP) for x in range(blocks)]
            return run(stages, (last_block, kb_pending, last_buf))[1]

        kb_pending = lax.fori_loop(0, n_full // FOX_TILES_PER_TRIP, trip, n_full)
        accumulate(last_block, kb_pending, last_buf)
    else:
        score(0, n_full, True, 0)

        def pair(i, pending):
            score(0, 2 * i, False, 1)
            accumulate(0, pending, 0)
            score(0, 2 * i + 1, False, 0)
            accumulate(0, 2 * i, 1)
            return 2 * i + 1

        pending = lax.fori_loop(0, n_full // 2, pair, n_full)

        @pl.when(n_full % 2 == 1)
        def _():
            score(0, n_full - 1, False, 1)
            accumulate(0, pending, 0)
            accumulate(0, n_full - 1, 1)

        @pl.when(n_full % 2 == 0)
        def _():
            accumulate(0, pending, 0)

    for x in range(blocks):
        acc = acc_ref[x]
        out = (acc[:FOX_HD] / acc[FOX_HD:FOX_HD + 1]).T
        for g in range(FOX_GROUP):
            o_ref[0, x * tq:(x + 1) * tq, g * FOX_HD:(g + 1) * FOX_HD] = out[g * tq:(g + 1) * tq, :].astype(BF16)


def _fox_attention(q, k_heads, f_pieces, v_t, *, tq, tk, q_start):
    b, lq, _ = q.shape
    lk = k_heads.shape[2]
    nk = lk // tk
    rows = FOX_GROUP * tq
    span = FOX_TILES_PER_TRIP * tk
    blocks = FOX_BLOCKS_PER_STEP if (FOX_BLOCKS_PER_STEP * tq == span and lq % span == 0 and q_start % span == 0) else 1
    assert tq & (tq - 1) == 0 and tk % tq == 0 and q_start % tk == 0
    return pl.pallas_call(
        functools.partial(_fox_kernel, tq=tq, tk=tk, q_start=q_start, blocks=blocks),
        grid=(b, FOX_KV_HEADS, lq // (blocks * tq)),
        in_specs=[
            pl.BlockSpec((1, blocks * tq, FOX_GROUP * FOX_HD), lambda i, h, t: (i, t, h)),
            pl.BlockSpec((1, 1, lk, FOX_HD), lambda i, h, t: (h, i, 0, 0)),
            pl.BlockSpec((1, lk, LANES), lambda i, h, t: (i, 0, 0)),
            pl.BlockSpec((1, 1, nk, FOX_V_ROWS, tk), lambda i, h, t: (h, i, 0, 0, 0)),
        ],
        out_specs=pl.BlockSpec((1, blocks * tq, FOX_GROUP * FOX_HD), lambda i, h, t: (i, t, h)),
        out_shape=jax.ShapeDtypeStruct(q.shape, BF16),
        scratch_shapes=[
            pltpu.VMEM((blocks, rows, 2 * FOX_HD), BF16),
            pltpu.VMEM((tk, rows), F32),
            pltpu.VMEM((tk, rows), F32),
            pltpu.VMEM((1, rows), F32),
            pltpu.VMEM((1, rows), F32),
            pltpu.VMEM((blocks, 1, rows), F32),
            pltpu.VMEM((blocks, FOX_V_ROWS, rows), F32),
        ],
        compiler_params=_params("parallel", "parallel", "arbitrary"),
        name="fox_attention",
    )(q, k_heads, f_pieces, v_t)


def _rope_tables(pos):
    half = RET_DK // 2
    inv = ROPE_BASE ** (-jnp.arange(half, dtype=F32) / half)
    ang = pos.astype(F32)[:, None] * inv[None, :]
    return jnp.cos(ang), jnp.sin(ang)


def _with_cache(new_heads, cache, lk_pad):
    past = cache.transpose(2, 0, 1, 3).astype(BF16)
    both = jnp.concatenate([past, new_heads], axis=2)
    return jnp.pad(both, ((0, 0), (0, 0), (0, lk_pad - both.shape[2]), (0, 0)))


def _transpose_values(v_heads, tk):
    kvh, b, lk, _ = v_heads.shape
    v_t = v_heads.reshape(kvh, b, lk // tk, tk, FOX_HD).transpose(0, 1, 2, 4, 3)
    ones = jnp.ones((kvh, b, lk // tk, FOX_V_ROWS - FOX_HD, tk), BF16)
    return jnp.concatenate([v_t, ones], axis=3)


def _stream(x, *, pos, s0, cache, weights, later_weights, tm, ffn_tm, chunk, tq):
    norm_mix, norm_ffn, norm_kv, norm_final, w_ret_in, w_ret_o, w_kv, b_kv, w_fox_q, w_fox_o = weights
    b, length, d = x.shape
    m = b * length
    h = x.reshape(m, d)

    cos, sin = _rope_tables(pos)
    if cos.shape[0] < tm:
        cos = jnp.tile(cos, (tm // cos.shape[0], 1))
        sin = jnp.tile(sin, (tm // sin.shape[0], 1))
    cast = () if later_weights[0].dtype == BF16 else later_weights
    if s0 is None:
        o, s_fin, narrowed = _ret_fused(h, norm_mix[0:1], cos, sin, w_ret_in, cast,
                                        batch=b, seq=length, tm=tm, chunk=chunk)
    else:
        proj, narrowed = _ret_in(h, norm_mix[0:1], cos, sin, w_ret_in, cast, tm=tm)
        o, s_fin = _retention(proj, s0, batch=b, seq=length, chunk=chunk)
    later_weights = narrowed if cast else later_weights
    w_ffn_gate, w_ffn_up, w_ffn_down = later_weights
    h = _mm_res(o, w_ret_o, h, tm=tm, tn=COL_TILE)
    h = _ffn(h, norm_ffn[0:1], w_ffn_gate, w_ffn_up, w_ffn_down, layer=0, tm=ffn_tm)

    tk = FOX_K_TILE
    if cache is None:
        q_start = 0
        k32, v32, k_heads, v_t, lf_pad, lf = _kv_proj(h, norm_kv[None, :], w_kv, b_kv, tm=tm, key_tile=tk)
        k_heads = k_heads.reshape(FOX_KV_HEADS, b, length, FOX_HD)
        v_t = v_t.reshape(FOX_KV_HEADS, b, length // tk, FOX_V_ROWS, tk)
        lf_all = lf_pad.reshape(b, length, LANES)
    else:
        cache_k, cache_v, cache_logf = cache
        q_start = cache_k.shape[1]
        lk_pad = -(-(q_start + length) // tk) * tk
        k32, v32, k_heads, v_heads, lf_pad, lf = _kv_proj(h, norm_kv[None, :], w_kv, b_kv, tm=tm, key_tile=None)
        k_heads = _with_cache(k_heads.reshape(FOX_KV_HEADS, b, length, FOX_HD), cache_k, lk_pad)
        v_t = _transpose_values(_with_cache(v_heads.reshape(FOX_KV_HEADS, b, length, FOX_HD), cache_v, lk_pad), tk)
        lf_all = jnp.concatenate([
            jnp.pad(cache_logf.astype(F32), ((0, 0), (0, 0), (0, LANES - FOX_HEADS))),
            lf_pad.reshape(b, length, LANES)], axis=1)
        lf_all = jnp.pad(lf_all, ((0, 0), (0, lk_pad - lf_all.shape[1]), (0, 0)))
    lf = lf.reshape(b, length, FOX_HEADS)
    f_pieces = _cumsum_pieces(lf_all, tl=CUMSUM_TILE)

    q = _q_proj(h, norm_mix[1:2], w_fox_q, tm=tm).reshape(b, length, d)
    att = _fox_attention(q, k_heads, f_pieces, v_t, tq=tq, tk=tk, q_start=q_start)
    h = _mm_res(att.reshape(m, d), w_fox_o, h, tm=min(tm, ROW_TILE // 2), tn=d)
    y = _ffn(h, norm_ffn[1:2], w_ffn_gate, w_ffn_up, w_ffn_down, norm_final[None, :], layer=1, tm=ffn_tm)

    outputs = (y.reshape(b, length, d), s_fin[None],
               k32.reshape(b, length, FOX_KV_HEADS, FOX_HD), v32.reshape(b, length, FOX_KV_HEADS, FOX_HD), lf)
    return outputs, later_weights


def kernel(x_prompt, x_sample, state_ret, cache_k, cache_v, cache_logf, norm_mix, norm_ffn, norm_kv, norm_final, w_ret_in, w_ret_o, w_kv_k, w_kv_v, w_kv_f, b_kv_f, w_fox_q, w_fox_o, w_ffn_gate, w_ffn_up, w_ffn_down):
    lp = x_prompt.shape[1]
    bs, ls, _ = x_sample.shape
    past = cache_k.shape[1]

    w_kv = jnp.concatenate(
        [w_kv_k, w_kv_v, jnp.pad(w_kv_f, ((0, 0), (0, LANES - FOX_HEADS)))], axis=1).astype(BF16)
    b_kv = jnp.pad(b_kv_f.astype(F32), (0, LANES - FOX_HEADS))[None, :]
    weights = (norm_mix.astype(F32), norm_ffn.astype(F32), norm_kv.astype(F32), norm_final.astype(F32),
               w_ret_in[0].astype(BF16), w_ret_o[0].astype(BF16), w_kv, b_kv,
               w_fox_q[0].astype(BF16), w_fox_o[0].astype(BF16))
    later_weights = tuple(w.astype(F32) for w in (w_ffn_gate, w_ffn_up, w_ffn_down))

    (y_p, s_p, k_p, v_p, lf_p), later_weights = _stream(
        x_prompt, pos=jnp.arange(lp), s0=None, cache=None, weights=weights, later_weights=later_weights,
        tm=ROW_TILE, ffn_tm=FFN_ROW_TILE, chunk=RET_CHUNK, tq=FOX_Q_TILE)
    (y_s, s_s, k_s, v_s, lf_s), _ = _stream(
        x_sample, pos=past + jnp.arange(ls), s0=state_ret[0], cache=(cache_k, cache_v, cache_logf),
        weights=weights, later_weights=later_weights, tm=bs * ls, ffn_tm=bs * ls, chunk=ls, tq=ls)
    return (y_p, y_s, s_p, k_p, v_p, lf_p, s_s, k_s, v_s, lf_s)
```

```python
import functools

import jax
import jax.numpy as jnp
from jax import lax
from jax.experimental import pallas as pl
from jax.experimental.pallas import tpu as pltpu

F32 = jnp.float32
BF16 = jnp.bfloat16

D_MODEL = 2048
RET_HEADS = 8
RET_DK = D_MODEL // RET_HEADS
RET_DV = 2 * D_MODEL // RET_HEADS
RET_QK = RET_HEADS * RET_DK
RET_V = RET_HEADS * RET_DV
RET_HEAD_COLS = 2 * RET_DK + 2 * RET_DV
ROPE_BASE = 10000.0
FOX_HEADS = 16
FOX_HD = D_MODEL // FOX_HEADS
FOX_KV_HEADS = 4
FOX_GROUP = FOX_HEADS // FOX_KV_HEADS
FOX_KV = FOX_KV_HEADS * FOX_HD
FOX_F_PIECES = 3
FOX_V_ROWS = FOX_HD + 16
LOG2_E = 1.4426950408889634
NORM_EPS = 1e-6
GN_EPS = 1e-5

LANES = 128
BF16_SUBLANES = 16
VMEM_LIMIT_BYTES = 60 * 1024 * 1024

ROW_TILE = 1024
COL_TILE = 1024
ROW_GROUPS = 4
FFN_ROW_TILE = 1024
FFN_COL_TILE = 512
FFN_COL_TILE_SMALL_ROWS = 1408
RET_CHUNK = 256
RET_HEADS_PER_STEP = 8
FOX_Q_TILE = 256
FOX_K_TILE = 512
FOX_BLOCKS_PER_STEP = 4
FOX_TILES_PER_TRIP = 2
FOX_COL_CHUNK = 512
CUMSUM_TILE = 512
MASK_VALUE = -1e30


def _params(*semantics):
    return pltpu.CompilerParams(dimension_semantics=semantics, vmem_limit_bytes=VMEM_LIMIT_BYTES)


def _rmsnorm(x, gain):
    y = x * lax.rsqrt(jnp.mean(x * x, axis=-1, keepdims=True) + NORM_EPS)
    return y * gain


def _silu(x):
    return x * (1.0 / (1.0 + jnp.exp(-x)))


def _ret_in_kernel(x_ref, gain_ref, cos_ref, sin_ref, wq_ref, wk_ref, wv_ref, wg_ref, *rest, n_cast):
    cast_in = rest[:n_cast]
    o_ref = rest[n_cast]
    cast_out = rest[n_cast + 1:2 * n_cast + 1]
    xn_ref = rest[2 * n_cast + 1]

    @pl.when(pl.program_id(1) == 0)
    def _():
        xn_ref[...] = _rmsnorm(x_ref[...], gain_ref[...]).astype(BF16)

    for src, dst in zip(cast_in, cast_out):
        dst[...] = src[...].astype(BF16)

    w = jnp.concatenate([wg_ref[...], wq_ref[...], wk_ref[...], wv_ref[...]], axis=1)
    y = jnp.dot(xn_ref[...], w, preferred_element_type=F32)
    cos = cos_ref[...]
    sin = sin_ref[...]
    half = RET_DK // 2
    v_first = 2 * RET_DK
    g_first = v_first + RET_DV
    o_ref[:, g_first:] = _silu(y[:, :RET_DV]).astype(BF16)
    for src, dst, scale in ((RET_DV, 0, 1.0), (RET_DV + RET_DK, RET_DK, RET_DK ** -0.5)):
        x1 = y[:, src:src + half]
        x2 = y[:, src + half:src + RET_DK]
        o_ref[:, dst:dst + half] = ((x1 * cos - x2 * sin) * scale).astype(BF16)
        o_ref[:, dst + half:dst + RET_DK] = ((x1 * sin + x2 * cos) * scale).astype(BF16)
    o_ref[:, v_first:g_first] = y[:, RET_DV + 2 * RET_DK:].astype(BF16)


def _slab_spec(shape, steps):
    layers, rows, cols = shape
    slab = next(r for r in range(BF16_SUBLANES, rows + 1, BF16_SUBLANES)
                if rows % r == 0 and layers * rows // r <= steps)
    per_layer = rows // slab
    last = layers * per_layer - 1

    def index(i, j):
        b = jnp.minimum(i * RET_HEADS + j, last)
        return (b // per_layer, b % per_layer, 0)

    return pl.BlockSpec((None, slab, cols), index)


def _ret_in(x, gain, cos, sin, w, cast=(), *, tm):
    m, d = x.shape
    pos_tiles = cos.shape[0] // tm
    k_first = RET_QK // RET_DK
    v_first = 2 * RET_QK // RET_DV
    g_first = (2 * RET_QK + RET_V) // RET_DV
    grid = (m // tm, RET_HEADS)
    cast_specs = [_slab_spec(c.shape, grid[0] * grid[1]) for c in cast]
    outs = pl.pallas_call(
        functools.partial(_ret_in_kernel, n_cast=len(cast)),
        grid=grid,
        in_specs=[
            pl.BlockSpec((tm, d), lambda i, j: (i, 0)),
            pl.BlockSpec((1, d), lambda i, j: (0, 0)),
            pl.BlockSpec((tm, RET_DK // 2), lambda i, j: (i % pos_tiles, 0)),
            pl.BlockSpec((tm, RET_DK // 2), lambda i, j: (i % pos_tiles, 0)),
            pl.BlockSpec((d, RET_DK), lambda i, j: (0, j)),
            pl.BlockSpec((d, RET_DK), lambda i, j: (0, k_first + j)),
            pl.BlockSpec((d, RET_DV), lambda i, j: (0, v_first + j)),
            pl.BlockSpec((d, RET_DV), lambda i, j: (0, g_first + j)),
        ] + cast_specs,
        out_specs=[pl.BlockSpec((tm, RET_HEAD_COLS), lambda i, j: (i, j))] + cast_specs,
        out_shape=[jax.ShapeDtypeStruct((m, RET_HEADS * RET_HEAD_COLS), BF16)]
        + [jax.ShapeDtypeStruct(c.shape, BF16) for c in cast],
        scratch_shapes=[pltpu.VMEM((tm, d), BF16)],
        compiler_params=_params("arbitrary", "arbitrary"),
        name="ret_in_proj",
    )(x, gain, cos, sin, w, w, w, w, *cast)
    return outs[0], tuple(outs[1:])


def _retention_kernel(*refs, chunk, has_init):
    if has_init:
        p_ref, s0_ref, o_ref, s_ref, dm_ref, rd_ref, kw_ref, raw_ref = refs
    else:
        p_ref, o_ref, s_ref, dm_ref, rd_ref, kw_ref, raw_ref = refs
        s0_ref = None
    c = pl.program_id(2)
    heads = RET_HEADS_PER_STEP

    def log_gamma(hh):
        head = pl.program_id(1) * heads + hh
        return jnp.log(1.0 - jnp.exp2(-5.0 - jnp.full((1, 1), head, jnp.int32).astype(F32)))

    @pl.when(c == 0)
    def _():
        li = lax.broadcasted_iota(jnp.int32, (chunk, chunk), 0)
        mi = lax.broadcasted_iota(jnp.int32, (chunk, chunk), 1)
        diff = (li - mi).astype(F32)
        idx = lax.broadcasted_iota(jnp.int32, (chunk, 1), 0).astype(F32)
        for hh in range(heads):
            lg = log_gamma(hh)
            dm_ref[hh] = jnp.where(diff >= 0, jnp.exp(lg * jnp.maximum(diff, 0.0)), 0.0)
            rd_ref[hh] = jnp.exp(lg * (idx + 1.0))
            kw_ref[hh] = jnp.exp(lg * (chunk - 1.0 - idx))
        if has_init:
            s_ref[...] = s0_ref[...]
        else:
            s_ref[...] = jnp.zeros_like(s_ref)

    for hh in range(heads):
        base = hh * RET_HEAD_COLS
        q = p_ref[:, base:base + RET_DK]
        k = p_ref[:, base + RET_DK:base + 2 * RET_DK]
        v = p_ref[:, base + 2 * RET_DK:base + 2 * RET_DK + RET_DV]
        s_old = s_ref[0, hh]
        scores = lax.dot_general(q, k, (((1,), (1,)), ((), ())), preferred_element_type=F32) * dm_ref[hh]
        inner = jnp.dot(scores.astype(BF16), v, preferred_element_type=F32)
        cross = jnp.dot(q, s_old.astype(BF16), preferred_element_type=F32) * rd_ref[hh]
        raw_ref[hh] = inner + cross
        kd = (k.astype(F32) * kw_ref[hh]).astype(BF16)
        s_ref[0, hh] = jnp.exp(log_gamma(hh) * float(chunk)) * s_old + lax.dot_general(
            kd, v, (((0,), (0,)), ((), ())), preferred_element_type=F32)

    for hh in range(heads):
        o = raw_ref[hh]
        mu = jnp.mean(o, axis=-1, keepdims=True)
        cen = o - mu
        var = jnp.mean(cen * cen, axis=-1, keepdims=True)
        on = cen * lax.rsqrt(var + GN_EPS)
        gate = p_ref[:, (hh + 1) * RET_HEAD_COLS - RET_DV:(hh + 1) * RET_HEAD_COLS].astype(F32)
        o_ref[:, hh * RET_DV:(hh + 1) * RET_DV] = (on * gate).astype(BF16)


def _retention(proj, s0, *, batch, seq, chunk):
    nc = seq // chunk
    has_init = s0 is not None
    heads = RET_HEADS_PER_STEP
    v_w = heads * RET_DV
    in_specs = [pl.BlockSpec((chunk, heads * RET_HEAD_COLS), lambda b, h, c: (b * nc + c, h))]
    args = [proj]
    if has_init:
        in_specs.append(pl.BlockSpec((1, heads, RET_DK, RET_DV), lambda b, h, c: (b, h, 0, 0)))
        args.append(s0)
    return pl.pallas_call(
        functools.partial(_retention_kernel, chunk=chunk, has_init=has_init),
        grid=(batch, RET_HEADS // heads, nc),
        in_specs=in_specs,
        out_specs=[
            pl.BlockSpec((chunk, v_w), lambda b, h, c: (b * nc + c, h)),
            pl.BlockSpec((1, heads, RET_DK, RET_DV), lambda b, h, c: (b, h, 0, 0)),
        ],
        out_shape=[
            jax.ShapeDtypeStruct((batch * seq, RET_V), BF16),
            jax.ShapeDtypeStruct((batch, RET_HEADS, RET_DK, RET_DV), F32),
        ],
        scratch_shapes=[
            pltpu.VMEM((heads, chunk, chunk), F32),
            pltpu.VMEM((heads, chunk, 1), F32),
            pltpu.VMEM((heads, chunk, 1), F32),
            pltpu.VMEM((heads, chunk, RET_DV), F32),
        ],
        compiler_params=_params("parallel", "parallel", "arbitrary"),
        name="retention",
    )(*args)


def _ret_fused_kernel(x_ref, gain_ref, cos_ref, sin_ref, wq_ref, wk_ref, wv_ref, wg_ref, *rest,
                      n_cast, chunk, tiles_per_seq):
    cast_in = rest[:n_cast]
    o_ref, state_ref = rest[n_cast:n_cast + 2]
    cast_out = rest[n_cast + 2:2 * n_cast + 2]
    xn_ref, p_ref, dm_ref, rd_ref, kw_ref = rest[2 * n_cast + 2:]
    s_ref = state_ref.at[0]
    i = pl.program_id(0)
    head = pl.program_id(1)
    lg = jnp.log(1.0 - jnp.exp2(-5.0 - jnp.full((1, 1), head, jnp.int32).astype(F32)))

    @pl.when(head == 0)
    def _():
        xn_ref[...] = _rmsnorm(x_ref[...], gain_ref[...]).astype(BF16)

    @pl.when(i == 0)
    def _():
        li = lax.broadcasted_iota(jnp.int32, (chunk, chunk), 0)
        mi = lax.broadcasted_iota(jnp.int32, (chunk, chunk), 1)
        diff = (li - mi).astype(F32)
        idx = lax.broadcasted_iota(jnp.int32, (chunk, 1), 0).astype(F32)
        dm_ref[head] = jnp.where(diff >= 0, jnp.exp(lg * jnp.maximum(diff, 0.0)), 0.0)
        rd_ref[head] = jnp.exp(lg * (idx + 1.0))
        kw_ref[head] = jnp.exp(lg * (chunk - 1.0 - idx))

    @pl.when(i % tiles_per_seq == 0)
    def _():
        s_ref[head] = jnp.zeros((RET_DK, RET_DV), F32)

    for src, dst in zip(cast_in, cast_out):
        dst[...] = src[...].astype(BF16)

    w = jnp.concatenate([wg_ref[...], wq_ref[...], wk_ref[...], wv_ref[...]], axis=1)
    half = RET_DK // 2
    v_first = 2 * RET_DK
    g_first = v_first + RET_DV

    def project(rows):
        y = jnp.dot(xn_ref[rows, :], w, preferred_element_type=F32)
        cos = cos_ref[rows, :]
        sin = sin_ref[rows, :]
        p_ref[rows, g_first:] = _silu(y[:, :RET_DV]).astype(BF16)
        for src, dst, scale in ((RET_DV, 0, 1.0), (RET_DV + RET_DK, RET_DK, RET_DK ** -0.5)):
            x1 = y[:, src:src + half]
            x2 = y[:, src + half:src + RET_DK]
            p_ref[rows, dst:dst + half] = ((x1 * cos - x2 * sin) * scale).astype(BF16)
            p_ref[rows, dst + half:dst + RET_DK] = ((x1 * sin + x2 * cos) * scale).astype(BF16)
        p_ref[rows, v_first:g_first] = y[:, RET_DV + 2 * RET_DK:].astype(BF16)

    def recur(rows):
        q = p_ref[rows, :RET_DK]
        k = p_ref[rows, RET_DK:v_first]
        v = p_ref[rows, v_first:g_first]
        s_old = s_ref[head]
        scores = lax.dot_general(q, k, (((1,), (1,)), ((), ())), preferred_element_type=F32) * dm_ref[head]
        inner = jnp.dot(scores.astype(BF16), v, preferred_element_type=F32)
        cross = jnp.dot(q, s_old.astype(BF16), preferred_element_type=F32) * rd_ref[head]
        kd = (k.astype(F32) * kw_ref[head]).astype(BF16)
        s_ref[head] = jnp.exp(lg * float(chunk)) * s_old + lax.dot_general(
            kd, v, (((0,), (0,)), ((), ())), preferred_element_type=F32)
        o = inner + cross
        mu = jnp.mean(o, axis=-1, keepdims=True)
        cen = o - mu
        var = jnp.mean(cen * cen, axis=-1, keepdims=True)
        on = cen * lax.rsqrt(var + GN_EPS)
        o_ref[rows, :] = (on * p_ref[rows, g_first:].astype(F32)).astype(BF16)

    groups = [slice(g * chunk, (g + 1) * chunk) for g in range(x_ref.shape[0] // chunk)]
    project(groups[0])
    for g in range(1, len(groups)):
        project(groups[g])
        recur(groups[g - 1])
    recur(groups[-1])


def _ret_fused(x, gain, cos, sin, w, cast=(), *, batch, seq, tm, chunk):
    m, d = x.shape
    pos_tiles = cos.shape[0] // tm
    tiles_per_seq = seq // tm
    k_first = RET_QK // RET_DK
    v_first = 2 * RET_QK // RET_DV
    g_first = (2 * RET_QK + RET_V) // RET_DV
    grid = (m // tm, RET_HEADS)
    cast_specs = [_slab_spec(c.shape, grid[0] * grid[1]) for c in cast]
    outs = pl.pallas_call(
        functools.partial(_ret_fused_kernel, n_cast=len(cast), chunk=chunk, tiles_per_seq=tiles_per_seq),
        grid=grid,
        in_specs=[
            pl.BlockSpec((tm, d), lambda i, j: (i, 0)),
            pl.BlockSpec((1, d), lambda i, j: (0, 0)),
            pl.BlockSpec((tm, RET_DK // 2), lambda i, j: (i % pos_tiles, 0)),
            pl.BlockSpec((tm, RET_DK // 2), lambda i, j: (i % pos_tiles, 0)),
            pl.BlockSpec((d, RET_DK), lambda i, j: (0, j)),
            pl.BlockSpec((d, RET_DK), lambda i, j: (0, k_first + j)),
            pl.BlockSpec((d, RET_DV), lambda i, j: (0, v_first + j)),
            pl.BlockSpec((d, RET_DV), lambda i, j: (0, g_first + j)),
        ] + cast_specs,
        out_specs=[
            pl.BlockSpec((tm, RET_DV), lambda i, j: (i, j)),
            pl.BlockSpec((1, RET_HEADS, RET_DK, RET_DV), lambda i, j: (i // tiles_per_seq, 0, 0, 0)),
        ] + cast_specs,
        out_shape=[
            jax.ShapeDtypeStruct((m, RET_V), BF16),
            jax.ShapeDtypeStruct((batch, RET_HEADS, RET_DK, RET_DV), F32),
        ] + [jax.ShapeDtypeStruct(c.shape, BF16) for c in cast],
        scratch_shapes=[
            pltpu.VMEM((tm, d), BF16),
            pltpu.VMEM((tm, RET_HEAD_COLS), BF16),
            pltpu.VMEM((RET_HEADS, chunk, chunk), F32),
            pltpu.VMEM((RET_HEADS, chunk, 1), F32),
            pltpu.VMEM((RET_HEADS, chunk, 1), F32),
        ],
        compiler_params=_params("arbitrary", "arbitrary"),
        name="ret_fused",
    )(x, gain, cos, sin, w, w, w, w, *cast)
    return outs[0], outs[1], tuple(outs[2:])


def _mm_res_kernel(a_ref, w_ref, r_ref, o_ref):
    o_ref[...] = r_ref[...] + jnp.dot(a_ref[...], w_ref[...], preferred_element_type=F32)


def _mm_res(a, w, res, *, tm, tn):
    m, kdim = a.shape
    n = w.shape[1]
    return pl.pallas_call(
        _mm_res_kernel,
        grid=(m // tm, n // tn),
        in_specs=[
            pl.BlockSpec((tm, kdim), lambda i, j: (i, 0)),
            pl.BlockSpec((kdim, tn), lambda i, j: (0, j)),
            pl.BlockSpec((tm, tn), lambda i, j: (i, j)),
        ],
        out_specs=pl.BlockSpec((tm, tn), lambda i, j: (i, j)),
        out_shape=jax.ShapeDtypeStruct((m, n), F32),
        compiler_params=_params("parallel", "arbitrary"),
        name="matmul_residual",
    )(a, w, res)


def _ffn_kernel(*refs, final_norm):
    if final_norm:
        x_ref, gain_ref, wg_ref, wu_ref, wd_ref, fgain_ref, o_ref, xn_ref = refs
    else:
        x_ref, gain_ref, wg_ref, wu_ref, wd_ref, o_ref, xn_ref = refs
    j = pl.program_id(1)

    @pl.when(j == 0)
    def _():
        x = x_ref[...]
        xn_ref[...] = _rmsnorm(x, gain_ref[...]).astype(BF16)
        o_ref[...] = x

    xn = xn_ref[...]
    gate = jnp.dot(xn, wg_ref[...], preferred_element_type=F32)
    up = jnp.dot(xn, wu_ref[...], preferred_element_type=F32)
    hidden = (_silu(gate) * up).astype(BF16)
    o_ref[...] += jnp.dot(hidden, wd_ref[...], preferred_element_type=F32)

    if final_norm:
        @pl.when(j == pl.num_programs(1) - 1)
        def _():
            o_ref[...] = _rmsnorm(o_ref[...], fgain_ref[...])


def _ffn(x, gain, wg, wu, wd, final_gain=None, *, layer, tm):
    m, d = x.shape
    f = wg.shape[2]
    tf = FFN_COL_TILE if tm >= FFN_ROW_TILE else FFN_COL_TILE_SMALL_ROWS
    final_norm = final_gain is not None
    in_specs = [
        pl.BlockSpec((tm, d), lambda i, j: (i, 0)),
        pl.BlockSpec((1, d), lambda i, j: (0, 0)),
        pl.BlockSpec((None, d, tf), lambda i, j: (layer, 0, j)),
        pl.BlockSpec((None, d, tf), lambda i, j: (layer, 0, j)),
        pl.BlockSpec((None, tf, d), lambda i, j: (layer, j, 0)),
    ]
    args = [x, gain, wg, wu, wd]
    if final_norm:
        in_specs.append(pl.BlockSpec((1, d), lambda i, j: (0, 0)))
        args.append(final_gain)
    return pl.pallas_call(
        functools.partial(_ffn_kernel, final_norm=final_norm),
        grid=(m // tm, f // tf),
        in_specs=in_specs,
        out_specs=pl.BlockSpec((tm, d), lambda i, j: (i, 0)),
        out_shape=jax.ShapeDtypeStruct((m, d), F32),
        scratch_shapes=[pltpu.VMEM((tm, d), BF16)],
        compiler_params=_params("parallel", "arbitrary"),
        name="ffn",
    )(*args)


def _kv_kernel(x_ref, gain_ref, w_ref, bias_ref, k32_ref, v32_ref, kh_ref, vh_ref, lf_ref, lf16_ref, *scratch,
               key_tile, tiles_per_seq):
    if tiles_per_seq is not None:
        carry_ref, = scratch

        @pl.when(pl.program_id(0) % tiles_per_seq == 0)
        def _():
            carry_ref[...] = jnp.zeros_like(carry_ref)

    rows = x_ref.shape[0] // ROW_GROUPS
    for r in range(ROW_GROUPS):
        rs = slice(r * rows, (r + 1) * rows)
        xn = _rmsnorm(x_ref[rs, :], gain_ref[...]).astype(BF16)
        y = jnp.dot(xn, w_ref[...], preferred_element_type=F32)
        z = y[:, :LANES] + bias_ref[...]
        lf = jnp.minimum(z, 0.0) - jnp.log1p(jnp.exp(-jnp.abs(z)))
        lf16_ref[rs, :] = lf[:, :FOX_HEADS]
        if tiles_per_seq is None:
            lf_ref[rs, :] = lf
        else:
            lf_ref[rs, :], carry_ref[...] = _prefix_pieces(lf, carry_ref[...])
        v = y[:, LANES:LANES + FOX_KV]
        k = y[:, LANES + FOX_KV:]
        for h in range(FOX_KV_HEADS):
            hs = slice(h * FOX_HD, (h + 1) * FOX_HD)
            out_rows = pl.ds(r * rows * FOX_KV_HEADS + h, rows, stride=FOX_KV_HEADS)
            v32_ref[out_rows, :] = v[:, hs]
            if key_tile is None:
                vh_ref[h, rs, :] = v[:, hs].astype(BF16)
            else:
                v_t = v[:, hs].T.astype(BF16)
                width = min(rows, key_tile)
                for t in range(rows // width):
                    first = r * rows + t * width
                    tile, lanes = first // key_tile, slice(first % key_tile, first % key_tile + width)
                    vh_ref[h, tile, :FOX_HD, lanes] = v_t[:, t * width:(t + 1) * width]
                    vh_ref[h, tile, FOX_HD:, lanes] = jnp.ones((FOX_V_ROWS - FOX_HD, width), BF16)
        for h in range(FOX_KV_HEADS):
            hs = slice(h * FOX_HD, (h + 1) * FOX_HD)
            out_rows = pl.ds(r * rows * FOX_KV_HEADS + h, rows, stride=FOX_KV_HEADS)
            k32_ref[out_rows, :] = k[:, hs]
            kh_ref[h, rs, :] = k[:, hs].astype(BF16)


def _kv_proj(x, gain, w, bias, *, tm, key_tile, seq=None):
    m, d = x.shape
    n = w.shape[1]
    tiles_per_seq = None if seq is None else seq // tm
    if key_tile is None:
        v_spec = pl.BlockSpec((FOX_KV_HEADS, tm, FOX_HD), lambda i: (0, i, 0))
        v_shape = jax.ShapeDtypeStruct((FOX_KV_HEADS, m, FOX_HD), BF16)
    else:
        v_spec = pl.BlockSpec((FOX_KV_HEADS, tm // key_tile, FOX_V_ROWS, key_tile), lambda i: (0, i, 0, 0))
        v_shape = jax.ShapeDtypeStruct((FOX_KV_HEADS, m // key_tile, FOX_V_ROWS, key_tile), BF16)
    return pl.pallas_call(
        functools.partial(_kv_kernel, key_tile=key_tile, tiles_per_seq=tiles_per_seq),
        grid=(m // tm,),
        in_specs=[
            pl.BlockSpec((tm, d), lambda i: (i, 0)),
            pl.BlockSpec((1, d), lambda i: (0, 0)),
            pl.BlockSpec((d, n), lambda i: (0, 0)),
            pl.BlockSpec((1, LANES), lambda i: (0, 0)),
        ],
        out_specs=[
            pl.BlockSpec((tm * FOX_KV_HEADS, FOX_HD), lambda i: (i, 0)),
            pl.BlockSpec((tm * FOX_KV_HEADS, FOX_HD), lambda i: (i, 0)),
            pl.BlockSpec((FOX_KV_HEADS, tm, FOX_HD), lambda i: (0, i, 0)),
            v_spec,
            pl.BlockSpec((tm, LANES), lambda i: (i, 0)),
            pl.BlockSpec((tm, FOX_HEADS), lambda i: (i, 0)),
        ],
        out_shape=[
            jax.ShapeDtypeStruct((m * FOX_KV_HEADS, FOX_HD), F32),
            jax.ShapeDtypeStruct((m * FOX_KV_HEADS, FOX_HD), F32),
            jax.ShapeDtypeStruct((FOX_KV_HEADS, m, FOX_HD), BF16),
            v_shape,
            jax.ShapeDtypeStruct((m, LANES), F32 if seq is None else BF16),
            jax.ShapeDtypeStruct((m, FOX_HEADS), F32),
        ],
        scratch_shapes=[] if seq is None else [pltpu.VMEM((1, LANES), F32)],
        compiler_params=_params("arbitrary"),
        name="kv_proj",
    )(x, gain, w, bias)


def _q_kernel(x_ref, gain_ref, w_ref, o_ref):
    rows = x_ref.shape[0] // ROW_GROUPS
    for r in range(ROW_GROUPS):
        rs = slice(r * rows, (r + 1) * rows)
        xn = _rmsnorm(x_ref[rs, :], gain_ref[...]).astype(BF16)
        y = jnp.dot(xn, w_ref[...], preferred_element_type=F32)
        o_ref[rs, :] = (y * (FOX_HD ** -0.5 * LOG2_E)).astype(BF16)


def _q_proj(x, gain, w, *, tm):
    m, d = x.shape
    n = w.shape[1]
    return pl.pallas_call(
        _q_kernel,
        grid=(m // tm,),
        in_specs=[
            pl.BlockSpec((tm, d), lambda i: (i, 0)),
            pl.BlockSpec((1, d), lambda i: (0, 0)),
            pl.BlockSpec((d, n), lambda i: (0, 0)),
        ],
        out_specs=pl.BlockSpec((tm, n), lambda i: (i, 0)),
        out_shape=jax.ShapeDtypeStruct((m, n), BF16),
        compiler_params=_params("parallel"),
        name="q_proj",
    )(x, gain, w)


def _prefix_pieces(x, carry):
    tl = x.shape[0]
    r = lax.broadcasted_iota(jnp.int32, (tl, tl), 0)
    c = lax.broadcasted_iota(jnp.int32, (tl, tl), 1)
    tri = jnp.where(c <= r, 1.0, 0.0).astype(BF16)
    x_hi = x.astype(BF16)
    x_rest = x - x_hi.astype(F32)
    x_mid = x_rest.astype(BF16)
    x_lo = (x_rest - x_mid.astype(F32)).astype(BF16)
    y = (jnp.dot(tri, x_lo, preferred_element_type=F32) + jnp.dot(tri, x_mid, preferred_element_type=F32)
         + jnp.dot(tri, x_hi, preferred_element_type=F32) + carry)
    y2 = y * LOG2_E
    hi = y2.astype(BF16).astype(F32)
    rest = y2 - hi
    mid = rest.astype(BF16).astype(F32)
    lo = rest - mid
    lane = lax.broadcasted_iota(jnp.int32, y.shape, 1)
    packed = jnp.where(
        lane < FOX_HEADS, hi,
        jnp.where(lane < 2 * FOX_HEADS, pltpu.roll(mid, FOX_HEADS, 1),
                  jnp.where(lane < 3 * FOX_HEADS, pltpu.roll(lo, 2 * FOX_HEADS, 1), 0.0)))
    return packed.astype(BF16), y[tl - 1:tl, :]


def _cumsum_kernel(x_ref, o_ref, carry_ref):
    @pl.when(pl.program_id(1) == 0)
    def _():
        carry_ref[...] = jnp.zeros_like(carry_ref)

    o_ref[0], carry_ref[...] = _prefix_pieces(x_ref[0], carry_ref[...])


def _cumsum_pieces(x, *, tl):
    b, length, lanes = x.shape
    return pl.pallas_call(
        _cumsum_kernel,
        grid=(b, length // tl),
        in_specs=[pl.BlockSpec((1, tl, lanes), lambda i, t: (i, t, 0))],
        out_specs=pl.BlockSpec((1, tl, lanes), lambda i, t: (i, t, 0)),
        out_shape=jax.ShapeDtypeStruct((b, length, lanes), BF16),
        scratch_shapes=[pltpu.VMEM((1, lanes), F32)],
        compiler_params=_params("parallel", "arbitrary"),
        name="logf_cumsum",
    )(x)


def _fox_kernel(q_ref, k_ref, f_ref, v_ref, o_ref, qa_ref, z0_ref, z1_ref, mb0_ref, mb1_ref, m_ref, acc_ref,
                *, tq, tk, q_start, blocks):
    step = pl.program_id(2)
    rows = FOX_GROUP * tq
    lane = lax.broadcasted_iota(jnp.int32, (tq, FOX_HD), 1)
    for x in range(blocks):
        for g in range(FOX_GROUP):
            head = pl.program_id(1) * FOX_GROUP + g
            pick = jnp.where(lane < FOX_F_PIECES * FOX_HEADS,
                             jnp.where((lane & (FOX_HEADS - 1)) == head, -1.0, 0.0), 0.0)
            qa_ref[x, g * tq:(g + 1) * tq, :FOX_HD] = q_ref[0, x * tq:(x + 1) * tq, g * FOX_HD:(g + 1) * FOX_HD]
            qa_ref[x, g * tq:(g + 1) * tq, FOX_HD:] = pick.astype(BF16)
    n_full = (q_start + step * blocks * tq) // tk

    m_ref[...] = jnp.full_like(m_ref, MASK_VALUE)
    acc_ref[...] = jnp.zeros_like(acc_ref)

    cw = min(rows, FOX_COL_CHUNK)
    chunks = [slice(c * cw, (c + 1) * cw) for c in range(rows // cw)]
    buffers = ((z0_ref, mb0_ref), (z1_ref, mb1_ref))

    def score(x, kb, masked, buf):
        z_ref, mb_ref = buffers[buf]
        start = pl.multiple_of(kb * tk, tk)
        ka = jnp.concatenate([k_ref[0, 0, pl.ds(start, tk), :], f_ref[0, pl.ds(start, tk), :]], axis=1)
        for c, cs in enumerate(chunks):
            z = lax.dot_general(ka, qa_ref[x, cs, :], (((1,), (1,)), ((), ())), preferred_element_type=F32)
            if masked:
                kpos = kb * tk + lax.broadcasted_iota(jnp.int32, (tk, cw), 0)
                col = c * cw + lax.broadcasted_iota(jnp.int32, (tk, cw), 1)
                qpos = q_start + (step * blocks + x) * tq + (col & (tq - 1))
                z = jnp.where(kpos <= qpos, z, MASK_VALUE)
            z_ref[:, cs] = z
            mb_ref[:, cs] = jnp.max(z, axis=0, keepdims=True)

    def accumulate(x, kb, buf):
        z_ref, mb_ref = buffers[buf]
        vt = v_ref[0, 0, kb]
        for cs in chunks:
            m_old = m_ref[x, :, cs]
            m_new = jnp.maximum(m_old, mb_ref[:, cs])
            alpha = jnp.exp2(m_old - m_new)
            p = jnp.exp2(z_ref[:, cs] - m_new)
            acc_ref[x, :, cs] = alpha * acc_ref[x, :, cs] + jnp.dot(vt, p.astype(BF16), preferred_element_type=F32)
            m_ref[x, :, cs] = m_new

    if blocks == 4:
        def run(stages, pending):
            for x, kb, masked in stages:
                buf = 1 - pending[2]
                score(x, kb, masked, buf)
                accumulate(*pending)
                pending = (x, kb, buf)
            return pending

        score(0, n_full, True, 0)
        pending = run([(1, n_full, True), (2, n_full + 1, True), (3, n_full + 1, True),
                       (2, n_full, False), (3, n_full, False)], (0, n_full, 0))
        last_block, _, last_buf = pending

        def trip(i, kb_pending):
            stages = [(x, FOX_TILES_PER_TRIP * i + t, False) for t in range(FOX_TILES_PER_TRIP) for x in range(blocks)]
            return run(stages, (last_block, kb_pending, last_buf))[1]

        kb_pending = lax.fori_loop(0, n_full // FOX_TILES_PER_TRIP, trip, n_full)
        accumulate(last_block, kb_pending, last_buf)
    else:
        score(0, n_full, True, 0)

        def pair(i, pending):
            score(0, 2 * i, False, 1)
            accumulate(0, pending, 0)
            score(0, 2 * i + 1, False, 0)
            accumulate(0, 2 * i, 1)
            return 2 * i + 1

        pending = lax.fori_loop(0, n_full // 2, pair, n_full)

        @pl.when(n_full % 2 == 1)
        def _():
            score(0, n_full - 1, False, 1)
            accumulate(0, pending, 0)
            accumulate(0, n_full - 1, 1)

        @pl.when(n_full % 2 == 0)
        def _():
            accumulate(0, pending, 0)

    for x in range(blocks):
        acc = acc_ref[x]
        out = (acc[:FOX_HD] / acc[FOX_HD:FOX_HD + 1]).T
        for g in range(FOX_GROUP):
            o_ref[0, x * tq:(x + 1) * tq, g * FOX_HD:(g + 1) * FOX_HD] = out[g * tq:(g + 1) * tq, :].astype(BF16)


def _fox_attention(q, k_heads, f_pieces, v_t, *, tq, tk, q_start):
    b, lq, _ = q.shape
    lk = k_heads.shape[2]
    nk = lk // tk
    rows = FOX_GROUP * tq
    span = FOX_TILES_PER_TRIP * tk
    blocks = FOX_BLOCKS_PER_STEP if (FOX_BLOCKS_PER_STEP * tq == span and lq % span == 0 and q_start % span == 0) else 1
    assert tq & (tq - 1) == 0 and tk % tq == 0 and q_start % tk == 0
    return pl.pallas_call(
        functools.partial(_fox_kernel, tq=tq, tk=tk, q_start=q_start, blocks=blocks),
        grid=(b, FOX_KV_HEADS, lq // (blocks * tq)),
        in_specs=[
            pl.BlockSpec((1, blocks * tq, FOX_GROUP * FOX_HD), lambda i, h, t: (i, t, h)),
            pl.BlockSpec((1, 1, lk, FOX_HD), lambda i, h, t: (h, i, 0, 0)),
            pl.BlockSpec((1, lk, LANES), lambda i, h, t: (i, 0, 0)),
            pl.BlockSpec((1, 1, nk, FOX_V_ROWS, tk), lambda i, h, t: (h, i, 0, 0, 0)),
        ],
        out_specs=pl.BlockSpec((1, blocks * tq, FOX_GROUP * FOX_HD), lambda i, h, t: (i, t, h)),
        out_shape=jax.ShapeDtypeStruct(q.shape, BF16),
        scratch_shapes=[
            pltpu.VMEM((blocks, rows, 2 * FOX_HD), BF16),
            pltpu.VMEM((tk, rows), F32),
            pltpu.VMEM((tk, rows), F32),
            pltpu.VMEM((1, rows), F32),
            pltpu.VMEM((1, rows), F32),
            pltpu.VMEM((blocks, 1, rows), F32),
            pltpu.VMEM((blocks, FOX_V_ROWS, rows), F32),
        ],
        compiler_params=_params("parallel", "parallel", "arbitrary"),
        name="fox_attention",
    )(q, k_heads, f_pieces, v_t)


def _rope_tables(pos):
    half = RET_DK // 2
    inv = ROPE_BASE ** (-jnp.arange(half, dtype=F32) / half)
    ang = pos.astype(F32)[:, None] * inv[None, :]
    return jnp.cos(ang), jnp.sin(ang)


def _with_cache(new_heads, cache, lk_pad):
    past = cache.transpose(2, 0, 1, 3).astype(BF16)
    both = jnp.concatenate([past, new_heads], axis=2)
    return jnp.pad(both, ((0, 0), (0, 0), (0, lk_pad - both.shape[2]), (0, 0)))


def _transpose_values(v_heads, tk):
    kvh, b, lk, _ = v_heads.shape
    v_t = v_heads.reshape(kvh, b, lk // tk, tk, FOX_HD).transpose(0, 1, 2, 4, 3)
    ones = jnp.ones((kvh, b, lk // tk, FOX_V_ROWS - FOX_HD, tk), BF16)
    return jnp.concatenate([v_t, ones], axis=3)


def _stream(x, *, pos, s0, cache, weights, later_weights, tm, ffn_tm, chunk, tq):
    norm_mix, norm_ffn, norm_kv, norm_final, w_ret_in, w_ret_o, w_kv, b_kv, w_fox_q, w_fox_o = weights
    b, length, d = x.shape
    m = b * length
    h = x.reshape(m, d)

    cos, sin = _rope_tables(pos)
    if cos.shape[0] < tm:
        cos = jnp.tile(cos, (tm // cos.shape[0], 1))
        sin = jnp.tile(sin, (tm // sin.shape[0], 1))
    cast = () if later_weights[0].dtype == BF16 else later_weights
    if s0 is None:
        o, s_fin, narrowed = _ret_fused(h, norm_mix[0:1], cos, sin, w_ret_in, cast,
                                        batch=b, seq=length, tm=tm, chunk=chunk)
    else:
        proj, narrowed = _ret_in(h, norm_mix[0:1], cos, sin, w_ret_in, cast, tm=tm)
        o, s_fin = _retention(proj, s0, batch=b, seq=length, chunk=chunk)
    later_weights = narrowed if cast else later_weights
    w_ffn_gate, w_ffn_up, w_ffn_down = later_weights
    h = _mm_res(o, w_ret_o, h, tm=tm, tn=COL_TILE)
    h = _ffn(h, norm_ffn[0:1], w_ffn_gate, w_ffn_up, w_ffn_down, layer=0, tm=ffn_tm)

    tk = FOX_K_TILE
    if cache is None:
        q_start = 0
        k32, v32, k_heads, v_t, f_pieces, lf = _kv_proj(
            h, norm_kv[None, :], w_kv, b_kv, tm=tm, key_tile=tk, seq=length)
        k_heads = k_heads.reshape(FOX_KV_HEADS, b, length, FOX_HD)
        v_t = v_t.reshape(FOX_KV_HEADS, b, length // tk, FOX_V_ROWS, tk)
        f_pieces = f_pieces.reshape(b, length, LANES)
    else:
        cache_k, cache_v, cache_logf = cache
        q_start = cache_k.shape[1]
        lk_pad = -(-(q_start + length) // tk) * tk
        k32, v32, k_heads, v_heads, lf_pad, lf = _kv_proj(h, norm_kv[None, :], w_kv, b_kv, tm=tm, key_tile=None)
        k_heads = _with_cache(k_heads.reshape(FOX_KV_HEADS, b, length, FOX_HD), cache_k, lk_pad)
        v_t = _transpose_values(_with_cache(v_heads.reshape(FOX_KV_HEADS, b, length, FOX_HD), cache_v, lk_pad), tk)
        lf_all = jnp.concatenate([
            jnp.pad(cache_logf.astype(F32), ((0, 0), (0, 0), (0, LANES - FOX_HEADS))),
            lf_pad.reshape(b, length, LANES)], axis=1)
        lf_all = jnp.pad(lf_all, ((0, 0), (0, lk_pad - lf_all.shape[1]), (0, 0)))
        f_pieces = _cumsum_pieces(lf_all, tl=CUMSUM_TILE)
    lf = lf.reshape(b, length, FOX_HEADS)

    q = _q_proj(h, norm_mix[1:2], w_fox_q, tm=tm).reshape(b, length, d)
    att = _fox_attention(q, k_heads, f_pieces, v_t, tq=tq, tk=tk, q_start=q_start)
    h = _mm_res(att.reshape(m, d), w_fox_o, h, tm=min(tm, ROW_TILE // 2), tn=d)
    y = _ffn(h, norm_ffn[1:2], w_ffn_gate, w_ffn_up, w_ffn_down, norm_final[None, :], layer=1, tm=ffn_tm)

    outputs = (y.reshape(b, length, d), s_fin[None],
               k32.reshape(b, length, FOX_KV_HEADS, FOX_HD), v32.reshape(b, length, FOX_KV_HEADS, FOX_HD), lf)
    return outputs, later_weights


def kernel(x_prompt, x_sample, state_ret, cache_k, cache_v, cache_logf, norm_mix, norm_ffn, norm_kv, norm_final, w_ret_in, w_ret_o, w_kv_k, w_kv_v, w_kv_f, b_kv_f, w_fox_q, w_fox_o, w_ffn_gate, w_ffn_up, w_ffn_down):
    lp = x_prompt.shape[1]
    bs, ls, _ = x_sample.shape
    past = cache_k.shape[1]

    w_kv = jnp.concatenate(
        [jnp.pad(w_kv_f, ((0, 0), (0, LANES - FOX_HEADS))), w_kv_v, w_kv_k], axis=1).astype(BF16)
    b_kv = jnp.pad(b_kv_f.astype(F32), (0, LANES - FOX_HEADS))[None, :]
    weights = (norm_mix.astype(F32), norm_ffn.astype(F32), norm_kv.astype(F32), norm_final.astype(F32),
               w_ret_in[0].astype(BF16), w_ret_o[0].astype(BF16), w_kv, b_kv,
               w_fox_q[0].astype(BF16), w_fox_o[0].astype(BF16))
    later_weights = tuple(w.astype(F32) for w in (w_ffn_gate, w_ffn_up, w_ffn_down))

    (y_p, s_p, k_p, v_p, lf_p), later_weights = _stream(
        x_prompt, pos=jnp.arange(lp), s0=None, cache=None, weights=weights, later_weights=later_weights,
        tm=ROW_TILE, ffn_tm=FFN_ROW_TILE, chunk=RET_CHUNK, tq=FOX_Q_TILE)
    (y_s, s_s, k_s, v_s, lf_s), _ = _stream(
        x_sample, pos=past + jnp.arange(ls), s0=state_ret[0], cache=(cache_k, cache_v, cache_logf),
        weights=weights, later_weights=later_weights, tm=bs * ls, ffn_tm=bs * ls, chunk=ls, tq=ls)
    return (y_p, y_s, s_p, k_p, v_p, lf_p, s_s, k_s, v_s, lf_s)
```

```python
import functools

import jax
import jax.numpy as jnp
from jax import lax
from jax.experimental import pallas as pl
from jax.experimental.pallas import tpu as pltpu

F32 = jnp.float32
BF16 = jnp.bfloat16

D_MODEL = 2048
RET_HEADS = 8
RET_DK = D_MODEL // RET_HEADS
RET_DV = 2 * D_MODEL // RET_HEADS
RET_QK = RET_HEADS * RET_DK
RET_V = RET_HEADS * RET_DV
RET_HEAD_COLS = 2 * RET_DK + 2 * RET_DV
ROPE_BASE = 10000.0
FOX_HEADS = 16
FOX_HD = D_MODEL // FOX_HEADS
FOX_KV_HEADS = 4
FOX_GROUP = FOX_HEADS // FOX_KV_HEADS
FOX_KV = FOX_KV_HEADS * FOX_HD
FOX_F_PIECES = 3
FOX_V_ROWS = FOX_HD + 16
LOG2_E = 1.4426950408889634
NORM_EPS = 1e-6
GN_EPS = 1e-5

LANES = 128
BF16_SUBLANES = 16
VMEM_LIMIT_BYTES = 60 * 1024 * 1024

ROW_TILE = 1024
COL_TILE = 1024
ROW_GROUPS = 4
MIN_GROUP_ROWS = 128
FFN_ROW_TILE = 1024
FFN_COL_TILE = 512
FFN_COL_TILE_SMALL_ROWS = 1408
RET_CHUNK = 256
RET_HEADS_PER_STEP = 8
FOX_Q_TILE = 256
FOX_K_TILE = 512
FOX_BLOCKS_PER_STEP = 4
FOX_TILES_PER_TRIP = 2
FOX_COL_CHUNK = 512
CUMSUM_TILE = 512
MASK_VALUE = -1e30


def _params(*semantics):
    return pltpu.CompilerParams(dimension_semantics=semantics, vmem_limit_bytes=VMEM_LIMIT_BYTES)


def _rmsnorm(x, gain):
    y = x * lax.rsqrt(jnp.mean(x * x, axis=-1, keepdims=True) + NORM_EPS)
    return y * gain


def _silu(x):
    return x * (1.0 / (1.0 + jnp.exp(-x)))


def _group_rows(tm):
    return tm // min(ROW_GROUPS, max(1, tm // MIN_GROUP_ROWS))


def _ret_in_kernel(x_ref, gain_ref, cos_ref, sin_ref, wq_ref, wk_ref, wv_ref, wg_ref, *rest, n_cast):
    cast_in = rest[:n_cast]
    o_ref = rest[n_cast]
    cast_out = rest[n_cast + 1:2 * n_cast + 1]
    xn_ref = rest[2 * n_cast + 1]

    @pl.when(pl.program_id(1) == 0)
    def _():
        xn_ref[...] = _rmsnorm(x_ref[...], gain_ref[...]).astype(BF16)

    for src, dst in zip(cast_in, cast_out):
        dst[...] = src[...].astype(BF16)

    w = jnp.concatenate([wg_ref[...], wq_ref[...], wk_ref[...], wv_ref[...]], axis=1)
    y = jnp.dot(xn_ref[...], w, preferred_element_type=F32)
    cos = cos_ref[...]
    sin = sin_ref[...]
    half = RET_DK // 2
    v_first = 2 * RET_DK
    g_first = v_first + RET_DV
    o_ref[:, g_first:] = _silu(y[:, :RET_DV]).astype(BF16)
    for src, dst, scale in ((RET_DV, 0, 1.0), (RET_DV + RET_DK, RET_DK, RET_DK ** -0.5)):
        x1 = y[:, src:src + half]
        x2 = y[:, src + half:src + RET_DK]
        o_ref[:, dst:dst + half] = ((x1 * cos - x2 * sin) * scale).astype(BF16)
        o_ref[:, dst + half:dst + RET_DK] = ((x1 * sin + x2 * cos) * scale).astype(BF16)
    o_ref[:, v_first:g_first] = y[:, RET_DV + 2 * RET_DK:].astype(BF16)


def _slab_spec(shape, steps):
    layers, rows, cols = shape
    slab = next(r for r in range(BF16_SUBLANES, rows + 1, BF16_SUBLANES)
                if rows % r == 0 and layers * rows // r <= steps)
    per_layer = rows // slab
    last = layers * per_layer - 1

    def index(i, j):
        b = jnp.minimum(i * RET_HEADS + j, last)
        return (b // per_layer, b % per_layer, 0)

    return pl.BlockSpec((None, slab, cols), index)


def _ret_in(x, gain, cos, sin, w, cast=(), *, tm):
    m, d = x.shape
    pos_tiles = cos.shape[0] // tm
    k_first = RET_QK // RET_DK
    v_first = 2 * RET_QK // RET_DV
    g_first = (2 * RET_QK + RET_V) // RET_DV
    grid = (m // tm, RET_HEADS)
    cast_specs = [_slab_spec(c.shape, grid[0] * grid[1]) for c in cast]
    outs = pl.pallas_call(
        functools.partial(_ret_in_kernel, n_cast=len(cast)),
        grid=grid,
        in_specs=[
            pl.BlockSpec((tm, d), lambda i, j: (i, 0)),
            pl.BlockSpec((1, d), lambda i, j: (0, 0)),
            pl.BlockSpec((tm, RET_DK // 2), lambda i, j: (i % pos_tiles, 0)),
            pl.BlockSpec((tm, RET_DK // 2), lambda i, j: (i % pos_tiles, 0)),
            pl.BlockSpec((d, RET_DK), lambda i, j: (0, j)),
            pl.BlockSpec((d, RET_DK), lambda i, j: (0, k_first + j)),
            pl.BlockSpec((d, RET_DV), lambda i, j: (0, v_first + j)),
            pl.BlockSpec((d, RET_DV), lambda i, j: (0, g_first + j)),
        ] + cast_specs,
        out_specs=[pl.BlockSpec((tm, RET_HEAD_COLS), lambda i, j: (i, j))] + cast_specs,
        out_shape=[jax.ShapeDtypeStruct((m, RET_HEADS * RET_HEAD_COLS), BF16)]
        + [jax.ShapeDtypeStruct(c.shape, BF16) for c in cast],
        scratch_shapes=[pltpu.VMEM((tm, d), BF16)],
        compiler_params=_params("arbitrary", "arbitrary"),
        name="ret_in_proj",
    )(x, gain, cos, sin, w, w, w, w, *cast)
    return outs[0], tuple(outs[1:])


def _retention_kernel(*refs, chunk, has_init):
    if has_init:
        p_ref, s0_ref, o_ref, s_ref, dm_ref, rd_ref, kw_ref, raw_ref = refs
    else:
        p_ref, o_ref, s_ref, dm_ref, rd_ref, kw_ref, raw_ref = refs
        s0_ref = None
    c = pl.program_id(2)
    heads = RET_HEADS_PER_STEP

    def log_gamma(hh):
        head = pl.program_id(1) * heads + hh
        return jnp.log(1.0 - jnp.exp2(-5.0 - jnp.full((1, 1), head, jnp.int32).astype(F32)))

    @pl.when(c == 0)
    def _():
        li = lax.broadcasted_iota(jnp.int32, (chunk, chunk), 0)
        mi = lax.broadcasted_iota(jnp.int32, (chunk, chunk), 1)
        diff = (li - mi).astype(F32)
        idx = lax.broadcasted_iota(jnp.int32, (chunk, 1), 0).astype(F32)
        for hh in range(heads):
            lg = log_gamma(hh)
            dm_ref[hh] = jnp.where(diff >= 0, jnp.exp(lg * jnp.maximum(diff, 0.0)), 0.0)
            rd_ref[hh] = jnp.exp(lg * (idx + 1.0))
            kw_ref[hh] = jnp.exp(lg * (chunk - 1.0 - idx))
        if has_init:
            s_ref[...] = s0_ref[...]
        else:
            s_ref[...] = jnp.zeros_like(s_ref)

    for hh in range(heads):
        base = hh * RET_HEAD_COLS
        q = p_ref[:, base:base + RET_DK]
        k = p_ref[:, base + RET_DK:base + 2 * RET_DK]
        v = p_ref[:, base + 2 * RET_DK:base + 2 * RET_DK + RET_DV]
        s_old = s_ref[0, hh]
        scores = lax.dot_general(q, k, (((1,), (1,)), ((), ())), preferred_element_type=F32) * dm_ref[hh]
        inner = jnp.dot(scores.astype(BF16), v, preferred_element_type=F32)
        cross = jnp.dot(q, s_old.astype(BF16), preferred_element_type=F32) * rd_ref[hh]
        raw_ref[hh] = inner + cross
        kd = (k.astype(F32) * kw_ref[hh]).astype(BF16)
        s_ref[0, hh] = jnp.exp(log_gamma(hh) * float(chunk)) * s_old + lax.dot_general(
            kd, v, (((0,), (0,)), ((), ())), preferred_element_type=F32)

    for hh in range(heads):
        o = raw_ref[hh]
        mu = jnp.mean(o, axis=-1, keepdims=True)
        cen = o - mu
        var = jnp.mean(cen * cen, axis=-1, keepdims=True)
        on = cen * lax.rsqrt(var + GN_EPS)
        gate = p_ref[:, (hh + 1) * RET_HEAD_COLS - RET_DV:(hh + 1) * RET_HEAD_COLS].astype(F32)
        o_ref[:, hh * RET_DV:(hh + 1) * RET_DV] = (on * gate).astype(BF16)


def _retention(proj, s0, *, batch, seq, chunk):
    nc = seq // chunk
    has_init = s0 is not None
    heads = RET_HEADS_PER_STEP
    v_w = heads * RET_DV
    in_specs = [pl.BlockSpec((chunk, heads * RET_HEAD_COLS), lambda b, h, c: (b * nc + c, h))]
    args = [proj]
    if has_init:
        in_specs.append(pl.BlockSpec((1, heads, RET_DK, RET_DV), lambda b, h, c: (b, h, 0, 0)))
        args.append(s0)
    return pl.pallas_call(
        functools.partial(_retention_kernel, chunk=chunk, has_init=has_init),
        grid=(batch, RET_HEADS // heads, nc),
        in_specs=in_specs,
        out_specs=[
            pl.BlockSpec((chunk, v_w), lambda b, h, c: (b * nc + c, h)),
            pl.BlockSpec((1, heads, RET_DK, RET_DV), lambda b, h, c: (b, h, 0, 0)),
        ],
        out_shape=[
            jax.ShapeDtypeStruct((batch * seq, RET_V), BF16),
            jax.ShapeDtypeStruct((batch, RET_HEADS, RET_DK, RET_DV), F32),
        ],
        scratch_shapes=[
            pltpu.VMEM((heads, chunk, chunk), F32),
            pltpu.VMEM((heads, chunk, 1), F32),
            pltpu.VMEM((heads, chunk, 1), F32),
            pltpu.VMEM((heads, chunk, RET_DV), F32),
        ],
        compiler_params=_params("parallel", "parallel", "arbitrary"),
        name="retention",
    )(*args)


def _ret_fused_kernel(x_ref, gain_ref, cos_ref, sin_ref, wq_ref, wk_ref, wv_ref, wg_ref, *rest,
                      n_cast, chunk, tiles_per_seq):
    cast_in = rest[:n_cast]
    o_ref, state_ref = rest[n_cast:n_cast + 2]
    cast_out = rest[n_cast + 2:2 * n_cast + 2]
    xn_ref, p_ref, dm_ref, rd_ref, kw_ref = rest[2 * n_cast + 2:]
    s_ref = state_ref.at[0]
    i = pl.program_id(0)
    head = pl.program_id(1)
    lg = jnp.log(1.0 - jnp.exp2(-5.0 - jnp.full((1, 1), head, jnp.int32).astype(F32)))

    @pl.when(head == 0)
    def _():
        xn_ref[...] = _rmsnorm(x_ref[...], gain_ref[...]).astype(BF16)

    @pl.when(i == 0)
    def _():
        li = lax.broadcasted_iota(jnp.int32, (chunk, chunk), 0)
        mi = lax.broadcasted_iota(jnp.int32, (chunk, chunk), 1)
        diff = (li - mi).astype(F32)
        idx = lax.broadcasted_iota(jnp.int32, (chunk, 1), 0).astype(F32)
        dm_ref[head] = jnp.where(diff >= 0, jnp.exp(lg * jnp.maximum(diff, 0.0)), 0.0)
        rd_ref[head] = jnp.exp(lg * (idx + 1.0))
        kw_ref[head] = jnp.exp(lg * (chunk - 1.0 - idx))

    @pl.when(i % tiles_per_seq == 0)
    def _():
        s_ref[head] = jnp.zeros((RET_DK, RET_DV), F32)

    for src, dst in zip(cast_in, cast_out):
        dst[...] = src[...].astype(BF16)

    w = jnp.concatenate([wg_ref[...], wq_ref[...], wk_ref[...], wv_ref[...]], axis=1)
    half = RET_DK // 2
    v_first = 2 * RET_DK
    g_first = v_first + RET_DV

    def project(rows):
        y = jnp.dot(xn_ref[rows, :], w, preferred_element_type=F32)
        cos = cos_ref[rows, :]
        sin = sin_ref[rows, :]
        p_ref[rows, g_first:] = _silu(y[:, :RET_DV]).astype(BF16)
        for src, dst, scale in ((RET_DV, 0, 1.0), (RET_DV + RET_DK, RET_DK, RET_DK ** -0.5)):
            x1 = y[:, src:src + half]
            x2 = y[:, src + half:src + RET_DK]
            p_ref[rows, dst:dst + half] = ((x1 * cos - x2 * sin) * scale).astype(BF16)
            p_ref[rows, dst + half:dst + RET_DK] = ((x1 * sin + x2 * cos) * scale).astype(BF16)
        p_ref[rows, v_first:g_first] = y[:, RET_DV + 2 * RET_DK:].astype(BF16)

    def recur(rows):
        q = p_ref[rows, :RET_DK]
        k = p_ref[rows, RET_DK:v_first]
        v = p_ref[rows, v_first:g_first]
        s_old = s_ref[head]
        scores = lax.dot_general(q, k, (((1,), (1,)), ((), ())), preferred_element_type=F32) * dm_ref[head]
        inner = jnp.dot(scores.astype(BF16), v, preferred_element_type=F32)
        cross = jnp.dot(q, s_old.astype(BF16), preferred_element_type=F32) * rd_ref[head]
        kd = (k.astype(F32) * kw_ref[head]).astype(BF16)
        s_ref[head] = jnp.exp(lg * float(chunk)) * s_old + lax.dot_general(
            kd, v, (((0,), (0,)), ((), ())), preferred_element_type=F32)
        o = inner + cross
        mu = jnp.mean(o, axis=-1, keepdims=True)
        cen = o - mu
        var = jnp.mean(cen * cen, axis=-1, keepdims=True)
        on = cen * lax.rsqrt(var + GN_EPS)
        o_ref[rows, :] = (on * p_ref[rows, g_first:].astype(F32)).astype(BF16)

    groups = [slice(g * chunk, (g + 1) * chunk) for g in range(x_ref.shape[0] // chunk)]
    project(groups[0])
    for g in range(1, len(groups)):
        project(groups[g])
        recur(groups[g - 1])
    recur(groups[-1])


def _ret_fused(x, gain, cos, sin, w, cast=(), *, batch, seq, tm, chunk):
    m, d = x.shape
    pos_tiles = cos.shape[0] // tm
    tiles_per_seq = seq // tm
    k_first = RET_QK // RET_DK
    v_first = 2 * RET_QK // RET_DV
    g_first = (2 * RET_QK + RET_V) // RET_DV
    grid = (m // tm, RET_HEADS)
    cast_specs = [_slab_spec(c.shape, grid[0] * grid[1]) for c in cast]
    outs = pl.pallas_call(
        functools.partial(_ret_fused_kernel, n_cast=len(cast), chunk=chunk, tiles_per_seq=tiles_per_seq),
        grid=grid,
        in_specs=[
            pl.BlockSpec((tm, d), lambda i, j: (i, 0)),
            pl.BlockSpec((1, d), lambda i, j: (0, 0)),
            pl.BlockSpec((tm, RET_DK // 2), lambda i, j: (i % pos_tiles, 0)),
            pl.BlockSpec((tm, RET_DK // 2), lambda i, j: (i % pos_tiles, 0)),
            pl.BlockSpec((d, RET_DK), lambda i, j: (0, j)),
            pl.BlockSpec((d, RET_DK), lambda i, j: (0, k_first + j)),
            pl.BlockSpec((d, RET_DV), lambda i, j: (0, v_first + j)),
            pl.BlockSpec((d, RET_DV), lambda i, j: (0, g_first + j)),
        ] + cast_specs,
        out_specs=[
            pl.BlockSpec((tm, RET_DV), lambda i, j: (i, j)),
            pl.BlockSpec((1, RET_HEADS, RET_DK, RET_DV), lambda i, j: (i // tiles_per_seq, 0, 0, 0)),
        ] + cast_specs,
        out_shape=[
            jax.ShapeDtypeStruct((m, RET_V), BF16),
            jax.ShapeDtypeStruct((batch, RET_HEADS, RET_DK, RET_DV), F32),
        ] + [jax.ShapeDtypeStruct(c.shape, BF16) for c in cast],
        scratch_shapes=[
            pltpu.VMEM((tm, d), BF16),
            pltpu.VMEM((tm, RET_HEAD_COLS), BF16),
            pltpu.VMEM((RET_HEADS, chunk, chunk), F32),
            pltpu.VMEM((RET_HEADS, chunk, 1), F32),
            pltpu.VMEM((RET_HEADS, chunk, 1), F32),
        ],
        compiler_params=_params("arbitrary", "arbitrary"),
        name="ret_fused",
    )(x, gain, cos, sin, w, w, w, w, *cast)
    return outs[0], outs[1], tuple(outs[2:])


def _mm_res_kernel(a_ref, w_ref, r_ref, o_ref):
    o_ref[...] = r_ref[...] + jnp.dot(a_ref[...], w_ref[...], preferred_element_type=F32)


def _mm_res(a, w, res, *, tm, tn):
    m, kdim = a.shape
    n = w.shape[1]
    return pl.pallas_call(
        _mm_res_kernel,
        grid=(m // tm, n // tn),
        in_specs=[
            pl.BlockSpec((tm, kdim), lambda i, j: (i, 0)),
            pl.BlockSpec((kdim, tn), lambda i, j: (0, j)),
            pl.BlockSpec((tm, tn), lambda i, j: (i, j)),
        ],
        out_specs=pl.BlockSpec((tm, tn), lambda i, j: (i, j)),
        out_shape=jax.ShapeDtypeStruct((m, n), F32),
        compiler_params=_params("parallel", "arbitrary"),
        name="matmul_residual",
    )(a, w, res)


def _ffn_kernel(*refs, final_norm):
    if final_norm:
        x_ref, gain_ref, wg_ref, wu_ref, wd_ref, fgain_ref, o_ref, xn_ref = refs
    else:
        x_ref, gain_ref, wg_ref, wu_ref, wd_ref, o_ref, xn_ref = refs
    j = pl.program_id(1)
    last = pl.num_programs(1) - 1
    tm = x_ref.shape[0]
    rows = _group_rows(tm)
    groups = [slice(r, r + rows) for r in range(0, tm, rows)]

    def contribution(xn):
        gate = jnp.dot(xn, wg_ref[...], preferred_element_type=F32)
        up = jnp.dot(xn, wu_ref[...], preferred_element_type=F32)
        hidden = (_silu(gate) * up).astype(BF16)
        return jnp.dot(hidden, wd_ref[...], preferred_element_type=F32)

    @pl.when(j == 0)
    def _():
        for rs in groups:
            x = x_ref[rs, :]
            xn = _rmsnorm(x, gain_ref[...]).astype(BF16)
            xn_ref[rs, :] = xn
            o_ref[rs, :] = x + contribution(xn)

    if final_norm:
        @pl.when(jnp.logical_and(j > 0, j < last))
        def _():
            o_ref[...] += contribution(xn_ref[...])

        @pl.when(j == last)
        def _():
            for rs in groups:
                y = o_ref[rs, :] + contribution(xn_ref[rs, :])
                o_ref[rs, :] = _rmsnorm(y, fgain_ref[...])
    else:
        @pl.when(j > 0)
        def _():
            o_ref[...] += contribution(xn_ref[...])


def _ffn(x, gain, wg, wu, wd, final_gain=None, *, layer, tm):
    m, d = x.shape
    f = wg.shape[2]
    tf = FFN_COL_TILE if tm >= FFN_ROW_TILE else FFN_COL_TILE_SMALL_ROWS
    final_norm = final_gain is not None
    in_specs = [
        pl.BlockSpec((tm, d), lambda i, j: (i, 0)),
        pl.BlockSpec((1, d), lambda i, j: (0, 0)),
        pl.BlockSpec((None, d, tf), lambda i, j: (layer, 0, j)),
        pl.BlockSpec((None, d, tf), lambda i, j: (layer, 0, j)),
        pl.BlockSpec((None, tf, d), lambda i, j: (layer, j, 0)),
    ]
    args = [x, gain, wg, wu, wd]
    if final_norm:
        in_specs.append(pl.BlockSpec((1, d), lambda i, j: (0, 0)))
        args.append(final_gain)
    return pl.pallas_call(
        functools.partial(_ffn_kernel, final_norm=final_norm),
        grid=(m // tm, f // tf),
        in_specs=in_specs,
        out_specs=pl.BlockSpec((tm, d), lambda i, j: (i, 0)),
        out_shape=jax.ShapeDtypeStruct((m, d), F32),
        scratch_shapes=[pltpu.VMEM((tm, d), BF16)],
        compiler_params=_params("parallel", "arbitrary"),
        name="ffn",
    )(*args)


def _kv_kernel(x_ref, gain_ref, w_ref, bias_ref, k32_ref, v32_ref, kh_ref, vh_ref, lf_ref, lf16_ref, *scratch,
               key_tile, tiles_per_seq):
    if tiles_per_seq is not None:
        carry_ref, = scratch

        @pl.when(pl.program_id(0) % tiles_per_seq == 0)
        def _():
            carry_ref[...] = jnp.zeros_like(carry_ref)

    rows = _group_rows(x_ref.shape[0])
    for r in range(x_ref.shape[0] // rows):
        rs = slice(r * rows, (r + 1) * rows)
        xn = _rmsnorm(x_ref[rs, :], gain_ref[...]).astype(BF16)
        y = jnp.dot(xn, w_ref[...], preferred_element_type=F32)
        z = y[:, :LANES] + bias_ref[...]
        lf = jnp.minimum(z, 0.0) - jnp.log1p(jnp.exp(-jnp.abs(z)))
        lf16_ref[rs, :] = lf[:, :FOX_HEADS]
        if tiles_per_seq is None:
            lf_ref[rs, :] = lf
        else:
            lf_ref[rs, :], carry_ref[...] = _prefix_pieces(lf, carry_ref[...])
        v = y[:, LANES:LANES + FOX_KV]
        k = y[:, LANES + FOX_KV:]
        for h in range(FOX_KV_HEADS):
            hs = slice(h * FOX_HD, (h + 1) * FOX_HD)
            out_rows = pl.ds(r * rows * FOX_KV_HEADS + h, rows, stride=FOX_KV_HEADS)
            v32_ref[out_rows, :] = v[:, hs]
            if key_tile is None:
                vh_ref[h, rs, :] = v[:, hs].astype(BF16)
            else:
                v_t = v[:, hs].T.astype(BF16)
                width = min(rows, key_tile)
                for t in range(rows // width):
                    first = r * rows + t * width
                    tile, lanes = first // key_tile, slice(first % key_tile, first % key_tile + width)
                    vh_ref[h, tile, :FOX_HD, lanes] = v_t[:, t * width:(t + 1) * width]
                    vh_ref[h, tile, FOX_HD:, lanes] = jnp.ones((FOX_V_ROWS - FOX_HD, width), BF16)
        for h in range(FOX_KV_HEADS):
            hs = slice(h * FOX_HD, (h + 1) * FOX_HD)
            out_rows = pl.ds(r * rows * FOX_KV_HEADS + h, rows, stride=FOX_KV_HEADS)
            k32_ref[out_rows, :] = k[:, hs]
            kh_ref[h, rs, :] = k[:, hs].astype(BF16)


def _kv_proj(x, gain, w, bias, *, tm, key_tile, seq=None):
    m, d = x.shape
    n = w.shape[1]
    tiles_per_seq = None if seq is None else seq // tm
    if key_tile is None:
        v_spec = pl.BlockSpec((FOX_KV_HEADS, tm, FOX_HD), lambda i: (0, i, 0))
        v_shape = jax.ShapeDtypeStruct((FOX_KV_HEADS, m, FOX_HD), BF16)
    else:
        v_spec = pl.BlockSpec((FOX_KV_HEADS, tm // key_tile, FOX_V_ROWS, key_tile), lambda i: (0, i, 0, 0))
        v_shape = jax.ShapeDtypeStruct((FOX_KV_HEADS, m // key_tile, FOX_V_ROWS, key_tile), BF16)
    return pl.pallas_call(
        functools.partial(_kv_kernel, key_tile=key_tile, tiles_per_seq=tiles_per_seq),
        grid=(m // tm,),
        in_specs=[
            pl.BlockSpec((tm, d), lambda i: (i, 0)),
            pl.BlockSpec((1, d), lambda i: (0, 0)),
            pl.BlockSpec((d, n), lambda i: (0, 0)),
            pl.BlockSpec((1, LANES), lambda i: (0, 0)),
        ],
        out_specs=[
            pl.BlockSpec((tm * FOX_KV_HEADS, FOX_HD), lambda i: (i, 0)),
            pl.BlockSpec((tm * FOX_KV_HEADS, FOX_HD), lambda i: (i, 0)),
            pl.BlockSpec((FOX_KV_HEADS, tm, FOX_HD), lambda i: (0, i, 0)),
            v_spec,
            pl.BlockSpec((tm, LANES), lambda i: (i, 0)),
            pl.BlockSpec((tm, FOX_HEADS), lambda i: (i, 0)),
        ],
        out_shape=[
            jax.ShapeDtypeStruct((m * FOX_KV_HEADS, FOX_HD), F32),
            jax.ShapeDtypeStruct((m * FOX_KV_HEADS, FOX_HD), F32),
            jax.ShapeDtypeStruct((FOX_KV_HEADS, m, FOX_HD), BF16),
            v_shape,
            jax.ShapeDtypeStruct((m, LANES), F32 if seq is None else BF16),
            jax.ShapeDtypeStruct((m, FOX_HEADS), F32),
        ],
        scratch_shapes=[] if seq is None else [pltpu.VMEM((1, LANES), F32)],
        compiler_params=_params("arbitrary"),
        name="kv_proj",
    )(x, gain, w, bias)


def _q_kernel(x_ref, gain_ref, w_ref, o_ref):
    rows = _group_rows(x_ref.shape[0])
    for r in range(x_ref.shape[0] // rows):
        rs = slice(r * rows, (r + 1) * rows)
        xn = _rmsnorm(x_ref[rs, :], gain_ref[...]).astype(BF16)
        y = jnp.dot(xn, w_ref[...], preferred_element_type=F32)
        o_ref[rs, :] = (y * (FOX_HD ** -0.5 * LOG2_E)).astype(BF16)


def _q_proj(x, gain, w, *, tm):
    m, d = x.shape
    n = w.shape[1]
    return pl.pallas_call(
        _q_kernel,
        grid=(m // tm,),
        in_specs=[
            pl.BlockSpec((tm, d), lambda i: (i, 0)),
            pl.BlockSpec((1, d), lambda i: (0, 0)),
            pl.BlockSpec((d, n), lambda i: (0, 0)),
        ],
        out_specs=pl.BlockSpec((tm, n), lambda i: (i, 0)),
        out_shape=jax.ShapeDtypeStruct((m, n), BF16),
        compiler_params=_params("parallel"),
        name="q_proj",
    )(x, gain, w)


def _prefix_pieces(x, carry):
    tl = x.shape[0]
    r = lax.broadcasted_iota(jnp.int32, (tl, tl), 0)
    c = lax.broadcasted_iota(jnp.int32, (tl, tl), 1)
    tri = jnp.where(c <= r, 1.0, 0.0).astype(BF16)
    x_hi = x.astype(BF16)
    x_rest = x - x_hi.astype(F32)
    x_mid = x_rest.astype(BF16)
    x_lo = (x_rest - x_mid.astype(F32)).astype(BF16)
    y = (jnp.dot(tri, x_lo, preferred_element_type=F32) + jnp.dot(tri, x_mid, preferred_element_type=F32)
         + jnp.dot(tri, x_hi, preferred_element_type=F32) + carry)
    y2 = y * LOG2_E
    hi = y2.astype(BF16).astype(F32)
    rest = y2 - hi
    mid = rest.astype(BF16).astype(F32)
    lo = rest - mid
    lane = lax.broadcasted_iota(jnp.int32, y.shape, 1)
    packed = jnp.where(
        lane < FOX_HEADS, hi,
        jnp.where(lane < 2 * FOX_HEADS, pltpu.roll(mid, FOX_HEADS, 1),
                  jnp.where(lane < 3 * FOX_HEADS, pltpu.roll(lo, 2 * FOX_HEADS, 1), 0.0)))
    return packed.astype(BF16), y[tl - 1:tl, :]


def _cumsum_kernel(x_ref, o_ref, carry_ref):
    @pl.when(pl.program_id(1) == 0)
    def _():
        carry_ref[...] = jnp.zeros_like(carry_ref)

    o_ref[0], carry_ref[...] = _prefix_pieces(x_ref[0], carry_ref[...])


def _cumsum_pieces(x, *, tl):
    b, length, lanes = x.shape
    return pl.pallas_call(
        _cumsum_kernel,
        grid=(b, length // tl),
        in_specs=[pl.BlockSpec((1, tl, lanes), lambda i, t: (i, t, 0))],
        out_specs=pl.BlockSpec((1, tl, lanes), lambda i, t: (i, t, 0)),
        out_shape=jax.ShapeDtypeStruct((b, length, lanes), BF16),
        scratch_shapes=[pltpu.VMEM((1, lanes), F32)],
        compiler_params=_params("parallel", "arbitrary"),
        name="logf_cumsum",
    )(x)


def _fox_kernel(q_ref, k_ref, f_ref, v_ref, o_ref, qa_ref, z0_ref, z1_ref, mb0_ref, mb1_ref, m_ref, acc_ref,
                *, tq, tk, q_start, blocks):
    step = pl.program_id(2)
    rows = FOX_GROUP * tq
    lane = lax.broadcasted_iota(jnp.int32, (tq, FOX_HD), 1)
    for x in range(blocks):
        for g in range(FOX_GROUP):
            head = pl.program_id(1) * FOX_GROUP + g
            pick = jnp.where(lane < FOX_F_PIECES * FOX_HEADS,
                             jnp.where((lane & (FOX_HEADS - 1)) == head, -1.0, 0.0), 0.0)
            qa_ref[x, g * tq:(g + 1) * tq, :FOX_HD] = q_ref[0, x * tq:(x + 1) * tq, g * FOX_HD:(g + 1) * FOX_HD]
            qa_ref[x, g * tq:(g + 1) * tq, FOX_HD:] = pick.astype(BF16)
    n_full = (q_start + step * blocks * tq) // tk

    m_ref[...] = jnp.full_like(m_ref, MASK_VALUE)
    acc_ref[...] = jnp.zeros_like(acc_ref)

    cw = min(rows, FOX_COL_CHUNK)
    chunks = [slice(c * cw, (c + 1) * cw) for c in range(rows // cw)]
    buffers = ((z0_ref, mb0_ref), (z1_ref, mb1_ref))

    def score(x, kb, masked, buf):
        z_ref, mb_ref = buffers[buf]
        start = pl.multiple_of(kb * tk, tk)
        ka = jnp.concatenate([k_ref[0, 0, pl.ds(start, tk), :], f_ref[0, pl.ds(start, tk), :]], axis=1)
        for c, cs in enumerate(chunks):
            z = lax.dot_general(ka, qa_ref[x, cs, :], (((1,), (1,)), ((), ())), preferred_element_type=F32)
            if masked:
                kpos = kb * tk + lax.broadcasted_iota(jnp.int32, (tk, cw), 0)
                col = c * cw + lax.broadcasted_iota(jnp.int32, (tk, cw), 1)
                qpos = q_start + (step * blocks + x) * tq + (col & (tq - 1))
                z = jnp.where(kpos <= qpos, z, MASK_VALUE)
            z_ref[:, cs] = z
            mb_ref[:, cs] = jnp.max(z, axis=0, keepdims=True)

    def accumulate(x, kb, buf):
        z_ref, mb_ref = buffers[buf]
        vt = v_ref[0, 0, kb]
        for cs in chunks:
            m_old = m_ref[x, :, cs]
            m_new = jnp.maximum(m_old, mb_ref[:, cs])
            alpha = jnp.exp2(m_old - m_new)
            p = jnp.exp2(z_ref[:, cs] - m_new)
            acc_ref[x, :, cs] = alpha * acc_ref[x, :, cs] + jnp.dot(vt, p.astype(BF16), preferred_element_type=F32)
            m_ref[x, :, cs] = m_new

    if blocks == 4:
        def run(stages, pending):
            for x, kb, masked in stages:
                buf = 1 - pending[2]
                score(x, kb, masked, buf)
                accumulate(*pending)
                pending = (x, kb, buf)
            return pending

        score(0, n_full, True, 0)
        pending = run([(1, n_full, True), (2, n_full + 1, True), (3, n_full + 1, True),
                       (2, n_full, False), (3, n_full, False)], (0, n_full, 0))
        last_block, _, last_buf = pending

        def trip(i, kb_pending):
            stages = [(x, FOX_TILES_PER_TRIP * i + t, False) for t in range(FOX_TILES_PER_TRIP) for x in range(blocks)]
            return run(stages, (last_block, kb_pending, last_buf))[1]

        kb_pending = lax.fori_loop(0, n_full // FOX_TILES_PER_TRIP, trip, n_full)
        accumulate(last_block, kb_pending, last_buf)
    else:
        score(0, n_full, True, 0)

        def pair(i, pending):
            score(0, 2 * i, False, 1)
            accumulate(0, pending, 0)
            score(0, 2 * i + 1, False, 0)
            accumulate(0, 2 * i, 1)
            return 2 * i + 1

        pending = lax.fori_loop(0, n_full // 2, pair, n_full)

        @pl.when(n_full % 2 == 1)
        def _():
            score(0, n_full - 1, False, 1)
            accumulate(0, pending, 0)
            accumulate(0, n_full - 1, 1)

        @pl.when(n_full % 2 == 0)
        def _():
            accumulate(0, pending, 0)

    for x in range(blocks):
        acc = acc_ref[x]
        out = (acc[:FOX_HD] / acc[FOX_HD:FOX_HD + 1]).T
        for g in range(FOX_GROUP):
            o_ref[0, x * tq:(x + 1) * tq, g * FOX_HD:(g + 1) * FOX_HD] = out[g * tq:(g + 1) * tq, :].astype(BF16)


def _fox_attention(q, k_heads, f_pieces, v_t, *, tq, tk, q_start):
    b, lq, _ = q.shape
    lk = k_heads.shape[2]
    nk = lk // tk
    rows = FOX_GROUP * tq
    span = FOX_TILES_PER_TRIP * tk
    blocks = FOX_BLOCKS_PER_STEP if (FOX_BLOCKS_PER_STEP * tq == span and lq % span == 0 and q_start % span == 0) else 1
    assert tq & (tq - 1) == 0 and tk % tq == 0 and q_start % tk == 0
    return pl.pallas_call(
        functools.partial(_fox_kernel, tq=tq, tk=tk, q_start=q_start, blocks=blocks),
        grid=(b, FOX_KV_HEADS, lq // (blocks * tq)),
        in_specs=[
            pl.BlockSpec((1, blocks * tq, FOX_GROUP * FOX_HD), lambda i, h, t: (i, t, h)),
            pl.BlockSpec((1, 1, lk, FOX_HD), lambda i, h, t: (h, i, 0, 0)),
            pl.BlockSpec((1, lk, LANES), lambda i, h, t: (i, 0, 0)),
            pl.BlockSpec((1, 1, nk, FOX_V_ROWS, tk), lambda i, h, t: (h, i, 0, 0, 0)),
        ],
        out_specs=pl.BlockSpec((1, blocks * tq, FOX_GROUP * FOX_HD), lambda i, h, t: (i, t, h)),
        out_shape=jax.ShapeDtypeStruct(q.shape, BF16),
        scratch_shapes=[
            pltpu.VMEM((blocks, rows, 2 * FOX_HD), BF16),
            pltpu.VMEM((tk, rows), F32),
            pltpu.VMEM((tk, rows), F32),
            pltpu.VMEM((1, rows), F32),
            pltpu.VMEM((1, rows), F32),
            pltpu.VMEM((blocks, 1, rows), F32),
            pltpu.VMEM((blocks, FOX_V_ROWS, rows), F32),
        ],
        compiler_params=_params("parallel", "parallel", "arbitrary"),
        name="fox_attention",
    )(q, k_heads, f_pieces, v_t)


def _rope_tables(pos):
    half = RET_DK // 2
    inv = ROPE_BASE ** (-jnp.arange(half, dtype=F32) / half)
    ang = pos.astype(F32)[:, None] * inv[None, :]
    return jnp.cos(ang), jnp.sin(ang)


def _with_cache(new_heads, cache, lk_pad):
    past = cache.transpose(2, 0, 1, 3).astype(BF16)
    both = jnp.concatenate([past, new_heads], axis=2)
    return jnp.pad(both, ((0, 0), (0, 0), (0, lk_pad - both.shape[2]), (0, 0)))


def _transpose_values(v_heads, tk):
    kvh, b, lk, _ = v_heads.shape
    v_t = v_heads.reshape(kvh, b, lk // tk, tk, FOX_HD).transpose(0, 1, 2, 4, 3)
    ones = jnp.ones((kvh, b, lk // tk, FOX_V_ROWS - FOX_HD, tk), BF16)
    return jnp.concatenate([v_t, ones], axis=3)


def _stream(x, *, pos, s0, cache, weights, later_weights, tm, ffn_tm, chunk, tq):
    norm_mix, norm_ffn, norm_kv, norm_final, w_ret_in, w_ret_o, w_kv, b_kv, w_fox_q, w_fox_o = weights
    b, length, d = x.shape
    m = b * length
    h = x.reshape(m, d)

    cos, sin = _rope_tables(pos)
    if cos.shape[0] < tm:
        cos = jnp.tile(cos, (tm // cos.shape[0], 1))
        sin = jnp.tile(sin, (tm // sin.shape[0], 1))
    cast = () if later_weights[0].dtype == BF16 else later_weights
    if s0 is None:
        o, s_fin, narrowed = _ret_fused(h, norm_mix[0:1], cos, sin, w_ret_in, cast,
                                        batch=b, seq=length, tm=tm, chunk=chunk)
    else:
        proj, narrowed = _ret_in(h, norm_mix[0:1], cos, sin, w_ret_in, cast, tm=tm)
        o, s_fin = _retention(proj, s0, batch=b, seq=length, chunk=chunk)
    later_weights = narrowed if cast else later_weights
    w_ffn_gate, w_ffn_up, w_ffn_down = later_weights
    h = _mm_res(o, w_ret_o, h, tm=tm, tn=COL_TILE)
    h = _ffn(h, norm_ffn[0:1], w_ffn_gate, w_ffn_up, w_ffn_down, layer=0, tm=ffn_tm)

    tk = FOX_K_TILE
    if cache is None:
        q_start = 0
        k32, v32, k_heads, v_t, f_pieces, lf = _kv_proj(
            h, norm_kv[None, :], w_kv, b_kv, tm=tm, key_tile=tk, seq=length)
        k_heads = k_heads.reshape(FOX_KV_HEADS, b, length, FOX_HD)
        v_t = v_t.reshape(FOX_KV_HEADS, b, length // tk, FOX_V_ROWS, tk)
        f_pieces = f_pieces.reshape(b, length, LANES)
    else:
        cache_k, cache_v, cache_logf = cache
        q_start = cache_k.shape[1]
        lk_pad = -(-(q_start + length) // tk) * tk
        k32, v32, k_heads, v_heads, lf_pad, lf = _kv_proj(h, norm_kv[None, :], w_kv, b_kv, tm=tm, key_tile=None)
        k_heads = _with_cache(k_heads.reshape(FOX_KV_HEADS, b, length, FOX_HD), cache_k, lk_pad)
        v_t = _transpose_values(_with_cache(v_heads.reshape(FOX_KV_HEADS, b, length, FOX_HD), cache_v, lk_pad), tk)
        lf_all = jnp.concatenate([
            jnp.pad(cache_logf.astype(F32), ((0, 0), (0, 0), (0, LANES - FOX_HEADS))),
            lf_pad.reshape(b, length, LANES)], axis=1)
        lf_all = jnp.pad(lf_all, ((0, 0), (0, lk_pad - lf_all.shape[1]), (0, 0)))
        f_pieces = _cumsum_pieces(lf_all, tl=CUMSUM_TILE)
    lf = lf.reshape(b, length, FOX_HEADS)

    q = _q_proj(h, norm_mix[1:2], w_fox_q, tm=tm).reshape(b, length, d)
    att = _fox_attention(q, k_heads, f_pieces, v_t, tq=tq, tk=tk, q_start=q_start)
    h = _mm_res(att.reshape(m, d), w_fox_o, h, tm=min(tm, ROW_TILE // 2), tn=d)
    y = _ffn(h, norm_ffn[1:2], w_ffn_gate, w_ffn_up, w_ffn_down, norm_final[None, :], layer=1, tm=ffn_tm)

    outputs = (y.reshape(b, length, d), s_fin[None],
               k32.reshape(b, length, FOX_KV_HEADS, FOX_HD), v32.reshape(b, length, FOX_KV_HEADS, FOX_HD), lf)
    return outputs, later_weights


def kernel(x_prompt, x_sample, state_ret, cache_k, cache_v, cache_logf, norm_mix, norm_ffn, norm_kv, norm_final, w_ret_in, w_ret_o, w_kv_k, w_kv_v, w_kv_f, b_kv_f, w_fox_q, w_fox_o, w_ffn_gate, w_ffn_up, w_ffn_down):
    lp = x_prompt.shape[1]
    bs, ls, _ = x_sample.shape
    past = cache_k.shape[1]

    w_kv = jnp.concatenate(
        [jnp.pad(w_kv_f, ((0, 0), (0, LANES - FOX_HEADS))), w_kv_v, w_kv_k], axis=1).astype(BF16)
    b_kv = jnp.pad(b_kv_f.astype(F32), (0, LANES - FOX_HEADS))[None, :]
    weights = (norm_mix.astype(F32), norm_ffn.astype(F32), norm_kv.astype(F32), norm_final.astype(F32),
               w_ret_in[0].astype(BF16), w_ret_o[0].astype(BF16), w_kv, b_kv,
               w_fox_q[0].astype(BF16), w_fox_o[0].astype(BF16))
    later_weights = tuple(w.astype(F32) for w in (w_ffn_gate, w_ffn_up, w_ffn_down))

    (y_p, s_p, k_p, v_p, lf_p), later_weights = _stream(
        x_prompt, pos=jnp.arange(lp), s0=None, cache=None, weights=weights, later_weights=later_weights,
        tm=ROW_TILE, ffn_tm=FFN_ROW_TILE, chunk=RET_CHUNK, tq=FOX_Q_TILE)
    (y_s, s_s, k_s, v_s, lf_s), _ = _stream(
        x_sample, pos=past + jnp.arange(ls), s0=state_ret[0], cache=(cache_k, cache_v, cache_logf),
        weights=weights, later_weights=later_weights, tm=bs * ls, ffn_tm=bs * ls, chunk=ls, tq=ls)
    return (y_p, y_s, s_p, k_p, v_p, lf_p, s_s, k_s, v_s, lf_s)
```

```python
import functools

import jax
import jax.numpy as jnp
from jax import lax
from jax.experimental import pallas as pl
from jax.experimental.pallas import tpu as pltpu

F32 = jnp.float32
BF16 = jnp.bfloat16

D_MODEL = 2048
RET_HEADS = 8
RET_DK = D_MODEL // RET_HEADS
RET_DV = 2 * D_MODEL // RET_HEADS
RET_QK = RET_HEADS * RET_DK
RET_V = RET_HEADS * RET_DV
RET_HEAD_COLS = 2 * RET_DK + 2 * RET_DV
ROPE_BASE = 10000.0
FOX_HEADS = 16
FOX_HD = D_MODEL // FOX_HEADS
FOX_KV_HEADS = 4
FOX_GROUP = FOX_HEADS // FOX_KV_HEADS
FOX_KV = FOX_KV_HEADS * FOX_HD
FOX_F_PIECES = 3
FOX_V_ROWS = FOX_HD + 16
LOG2_E = 1.4426950408889634
NORM_EPS = 1e-6
GN_EPS = 1e-5

LANES = 128
BF16_SUBLANES = 16
VMEM_LIMIT_BYTES = 60 * 1024 * 1024

ROW_TILE = 1024
COL_TILE = 1024
ROW_GROUPS = 4
FFN_ROW_TILE = 1024
FFN_COL_TILE = 512
FFN_COL_TILE_SMALL_ROWS = 1408
RET_CHUNK = 256
RET_HEADS_PER_STEP = 8
FOX_Q_TILE = 256
FOX_K_TILE = 512
FOX_BLOCKS_PER_STEP = 4
FOX_TILES_PER_TRIP = 2
FOX_COL_CHUNK = 512
CUMSUM_TILE = 512
MASK_VALUE = -1e30


def _params(*semantics):
    return pltpu.CompilerParams(dimension_semantics=semantics, vmem_limit_bytes=VMEM_LIMIT_BYTES)


def _rmsnorm(x, gain):
    y = x * lax.rsqrt(jnp.mean(x * x, axis=-1, keepdims=True) + NORM_EPS)
    return y * gain


def _silu(x):
    return x * (1.0 / (1.0 + jnp.exp(-x)))


def _ret_in_kernel(x_ref, gain_ref, cos_ref, sin_ref, wq_ref, wk_ref, wv_ref, wg_ref, *rest, n_cast):
    cast_in = rest[:n_cast]
    o_ref = rest[n_cast]
    cast_out = rest[n_cast + 1:2 * n_cast + 1]
    xn_ref = rest[2 * n_cast + 1]

    @pl.when(pl.program_id(1) == 0)
    def _():
        xn_ref[...] = _rmsnorm(x_ref[...], gain_ref[...]).astype(BF16)

    for src, dst in zip(cast_in, cast_out):
        dst[...] = src[...].astype(BF16)

    w = jnp.concatenate([wg_ref[...], wq_ref[...], wk_ref[...], wv_ref[...]], axis=1)
    y = jnp.dot(xn_ref[...], w, preferred_element_type=F32)
    cos = cos_ref[...]
    sin = sin_ref[...]
    half = RET_DK // 2
    v_first = 2 * RET_DK
    g_first = v_first + RET_DV
    o_ref[:, g_first:] = _silu(y[:, :RET_DV]).astype(BF16)
    for src, dst, scale in ((RET_DV, 0, 1.0), (RET_DV + RET_DK, RET_DK, RET_DK ** -0.5)):
        x1 = y[:, src:src + half]
        x2 = y[:, src + half:src + RET_DK]
        o_ref[:, dst:dst + half] = ((x1 * cos - x2 * sin) * scale).astype(BF16)
        o_ref[:, dst + half:dst + RET_DK] = ((x1 * sin + x2 * cos) * scale).astype(BF16)
    o_ref[:, v_first:g_first] = y[:, RET_DV + 2 * RET_DK:].astype(BF16)


def _slab_spec(shape, steps):
    layers, rows, cols = shape
    slab = next(r for r in range(BF16_SUBLANES, rows + 1, BF16_SUBLANES)
                if rows % r == 0 and layers * rows // r <= steps)
    per_layer = rows // slab
    last = layers * per_layer - 1

    def index(i, j):
        b = jnp.minimum(i * RET_HEADS + j, last)
        return (b // per_layer, b % per_layer, 0)

    return pl.BlockSpec((None, slab, cols), index)


def _ret_in(x, gain, cos, sin, w, cast=(), *, tm):
    m, d = x.shape
    pos_tiles = cos.shape[0] // tm
    k_first = RET_QK // RET_DK
    v_first = 2 * RET_QK // RET_DV
    g_first = (2 * RET_QK + RET_V) // RET_DV
    grid = (m // tm, RET_HEADS)
    cast_specs = [_slab_spec(c.shape, grid[0] * grid[1]) for c in cast]
    outs = pl.pallas_call(
        functools.partial(_ret_in_kernel, n_cast=len(cast)),
        grid=grid,
        in_specs=[
            pl.BlockSpec((tm, d), lambda i, j: (i, 0)),
            pl.BlockSpec((1, d), lambda i, j: (0, 0)),
            pl.BlockSpec((tm, RET_DK // 2), lambda i, j: (i % pos_tiles, 0)),
            pl.BlockSpec((tm, RET_DK // 2), lambda i, j: (i % pos_tiles, 0)),
            pl.BlockSpec((d, RET_DK), lambda i, j: (0, j)),
            pl.BlockSpec((d, RET_DK), lambda i, j: (0, k_first + j)),
            pl.BlockSpec((d, RET_DV), lambda i, j: (0, v_first + j)),
            pl.BlockSpec((d, RET_DV), lambda i, j: (0, g_first + j)),
        ] + cast_specs,
        out_specs=[pl.BlockSpec((tm, RET_HEAD_COLS), lambda i, j: (i, j))] + cast_specs,
        out_shape=[jax.ShapeDtypeStruct((m, RET_HEADS * RET_HEAD_COLS), BF16)]
        + [jax.ShapeDtypeStruct(c.shape, BF16) for c in cast],
        scratch_shapes=[pltpu.VMEM((tm, d), BF16)],
        compiler_params=_params("arbitrary", "arbitrary"),
        name="ret_in_proj",
    )(x, gain, cos, sin, w, w, w, w, *cast)
    return outs[0], tuple(outs[1:])


def _retention_kernel(*refs, chunk, has_init):
    if has_init:
        p_ref, s0_ref, o_ref, s_ref, dm_ref, rd_ref, kw_ref, raw_ref = refs
    else:
        p_ref, o_ref, s_ref, dm_ref, rd_ref, kw_ref, raw_ref = refs
        s0_ref = None
    c = pl.program_id(2)
    heads = RET_HEADS_PER_STEP

    def log_gamma(hh):
        head = pl.program_id(1) * heads + hh
        return jnp.log(1.0 - jnp.exp2(-5.0 - jnp.full((1, 1), head, jnp.int32).astype(F32)))

    @pl.when(c == 0)
    def _():
        li = lax.broadcasted_iota(jnp.int32, (chunk, chunk), 0)
        mi = lax.broadcasted_iota(jnp.int32, (chunk, chunk), 1)
        diff = (li - mi).astype(F32)
        idx = lax.broadcasted_iota(jnp.int32, (chunk, 1), 0).astype(F32)
        for hh in range(heads):
            lg = log_gamma(hh)
            dm_ref[hh] = jnp.where(diff >= 0, jnp.exp(lg * jnp.maximum(diff, 0.0)), 0.0)
            rd_ref[hh] = jnp.exp(lg * (idx + 1.0))
            kw_ref[hh] = jnp.exp(lg * (chunk - 1.0 - idx))
        if has_init:
            s_ref[...] = s0_ref[...]
        else:
            s_ref[...] = jnp.zeros_like(s_ref)

    for hh in range(heads):
        base = hh * RET_HEAD_COLS
        q = p_ref[:, base:base + RET_DK]
        k = p_ref[:, base + RET_DK:base + 2 * RET_DK]
        v = p_ref[:, base + 2 * RET_DK:base + 2 * RET_DK + RET_DV]
        s_old = s_ref[0, hh]
        scores = lax.dot_general(q, k, (((1,), (1,)), ((), ())), preferred_element_type=F32) * dm_ref[hh]
        inner = jnp.dot(scores.astype(BF16), v, preferred_element_type=F32)
        cross = jnp.dot(q, s_old.astype(BF16), preferred_element_type=F32) * rd_ref[hh]
        raw_ref[hh] = inner + cross
        kd = (k.astype(F32) * kw_ref[hh]).astype(BF16)
        s_ref[0, hh] = jnp.exp(log_gamma(hh) * float(chunk)) * s_old + lax.dot_general(
            kd, v, (((0,), (0,)), ((), ())), preferred_element_type=F32)

    for hh in range(heads):
        o = raw_ref[hh]
        mu = jnp.mean(o, axis=-1, keepdims=True)
        cen = o - mu
        var = jnp.mean(cen * cen, axis=-1, keepdims=True)
        on = cen * lax.rsqrt(var + GN_EPS)
        gate = p_ref[:, (hh + 1) * RET_HEAD_COLS - RET_DV:(hh + 1) * RET_HEAD_COLS].astype(F32)
        o_ref[:, hh * RET_DV:(hh + 1) * RET_DV] = (on * gate).astype(BF16)


def _retention(proj, s0, *, batch, seq, chunk):
    nc = seq // chunk
    has_init = s0 is not None
    heads = RET_HEADS_PER_STEP
    v_w = heads * RET_DV
    in_specs = [pl.BlockSpec((chunk, heads * RET_HEAD_COLS), lambda b, h, c: (b * nc + c, h))]
    args = [proj]
    if has_init:
        in_specs.append(pl.BlockSpec((1, heads, RET_DK, RET_DV), lambda b, h, c: (b, h, 0, 0)))
        args.append(s0)
    return pl.pallas_call(
        functools.partial(_retention_kernel, chunk=chunk, has_init=has_init),
        grid=(batch, RET_HEADS // heads, nc),
        in_specs=in_specs,
        out_specs=[
            pl.BlockSpec((chunk, v_w), lambda b, h, c: (b * nc + c, h)),
            pl.BlockSpec((1, heads, RET_DK, RET_DV), lambda b, h, c: (b, h, 0, 0)),
        ],
        out_shape=[
            jax.ShapeDtypeStruct((batch * seq, RET_V), BF16),
            jax.ShapeDtypeStruct((batch, RET_HEADS, RET_DK, RET_DV), F32),
        ],
        scratch_shapes=[
            pltpu.VMEM((heads, chunk, chunk), F32),
            pltpu.VMEM((heads, chunk, 1), F32),
            pltpu.VMEM((heads, chunk, 1), F32),
            pltpu.VMEM((heads, chunk, RET_DV), F32),
        ],
        compiler_params=_params("parallel", "parallel", "arbitrary"),
        name="retention",
    )(*args)


def _ret_fused_kernel(x_ref, gain_ref, cos_ref, sin_ref, wq_ref, wk_ref, wv_ref, wg_ref, *rest,
                      n_cast, chunk, tiles_per_seq):
    cast_in = rest[:n_cast]
    o_ref, state_ref = rest[n_cast:n_cast + 2]
    cast_out = rest[n_cast + 2:2 * n_cast + 2]
    xn_ref, p_ref, dm_ref, rd_ref, kw_ref = rest[2 * n_cast + 2:]
    s_ref = state_ref.at[0]
    i = pl.program_id(0)
    head = pl.program_id(1)
    lg = jnp.log(1.0 - jnp.exp2(-5.0 - jnp.full((1, 1), head, jnp.int32).astype(F32)))

    @pl.when(head == 0)
    def _():
        xn_ref[...] = _rmsnorm(x_ref[...], gain_ref[...]).astype(BF16)

    @pl.when(i == 0)
    def _():
        li = lax.broadcasted_iota(jnp.int32, (chunk, chunk), 0)
        mi = lax.broadcasted_iota(jnp.int32, (chunk, chunk), 1)
        diff = (li - mi).astype(F32)
        idx = lax.broadcasted_iota(jnp.int32, (chunk, 1), 0).astype(F32)
        dm_ref[head] = jnp.where(diff >= 0, jnp.exp(lg * jnp.maximum(diff, 0.0)), 0.0)
        rd_ref[head] = jnp.exp(lg * (idx + 1.0))
        kw_ref[head] = jnp.exp(lg * (chunk - 1.0 - idx))

    @pl.when(i % tiles_per_seq == 0)
    def _():
        s_ref[head] = jnp.zeros((RET_DK, RET_DV), F32)

    for src, dst in zip(cast_in, cast_out):
        dst[...] = src[...].astype(BF16)

    w = jnp.concatenate([wg_ref[...], wq_ref[...], wk_ref[...], wv_ref[...]], axis=1)
    half = RET_DK // 2
    v_first = 2 * RET_DK
    g_first = v_first + RET_DV

    def project(rows):
        y = jnp.dot(xn_ref[rows, :], w, preferred_element_type=F32)
        cos = cos_ref[rows, :]
        sin = sin_ref[rows, :]
        p_ref[rows, g_first:] = _silu(y[:, :RET_DV]).astype(BF16)
        for src, dst, scale in ((RET_DV, 0, 1.0), (RET_DV + RET_DK, RET_DK, RET_DK ** -0.5)):
            x1 = y[:, src:src + half]
            x2 = y[:, src + half:src + RET_DK]
            p_ref[rows, dst:dst + half] = ((x1 * cos - x2 * sin) * scale).astype(BF16)
            p_ref[rows, dst + half:dst + RET_DK] = ((x1 * sin + x2 * cos) * scale).astype(BF16)
        p_ref[rows, v_first:g_first] = y[:, RET_DV + 2 * RET_DK:].astype(BF16)

    def recur(rows):
        q = p_ref[rows, :RET_DK]
        k = p_ref[rows, RET_DK:v_first]
        v = p_ref[rows, v_first:g_first]
        s_old = s_ref[head]
        scores = lax.dot_general(q, k, (((1,), (1,)), ((), ())), preferred_element_type=F32) * dm_ref[head]
        inner = jnp.dot(scores.astype(BF16), v, preferred_element_type=F32)
        cross = jnp.dot(q, s_old.astype(BF16), preferred_element_type=F32) * rd_ref[head]
        kd = (k.astype(F32) * kw_ref[head]).astype(BF16)
        s_ref[head] = jnp.exp(lg * float(chunk)) * s_old + lax.dot_general(
            kd, v, (((0,), (0,)), ((), ())), preferred_element_type=F32)
        o = inner + cross
        mu = jnp.mean(o, axis=-1, keepdims=True)
        cen = o - mu
        var = jnp.mean(cen * cen, axis=-1, keepdims=True)
        on = cen * lax.rsqrt(var + GN_EPS)
        o_ref[rows, :] = (on * p_ref[rows, g_first:].astype(F32)).astype(BF16)

    groups = [slice(g * chunk, (g + 1) * chunk) for g in range(x_ref.shape[0] // chunk)]
    project(groups[0])
    for g in range(1, len(groups)):
        project(groups[g])
        recur(groups[g - 1])
    recur(groups[-1])


def _ret_fused(x, gain, cos, sin, w, cast=(), *, batch, seq, tm, chunk):
    m, d = x.shape
    pos_tiles = cos.shape[0] // tm
    tiles_per_seq = seq // tm
    k_first = RET_QK // RET_DK
    v_first = 2 * RET_QK // RET_DV
    g_first = (2 * RET_QK + RET_V) // RET_DV
    grid = (m // tm, RET_HEADS)
    cast_specs = [_slab_spec(c.shape, grid[0] * grid[1]) for c in cast]
    outs = pl.pallas_call(
        functools.partial(_ret_fused_kernel, n_cast=len(cast), chunk=chunk, tiles_per_seq=tiles_per_seq),
        grid=grid,
        in_specs=[
            pl.BlockSpec((tm, d), lambda i, j: (i, 0)),
            pl.BlockSpec((1, d), lambda i, j: (0, 0)),
            pl.BlockSpec((tm, RET_DK // 2), lambda i, j: (i % pos_tiles, 0)),
            pl.BlockSpec((tm, RET_DK // 2), lambda i, j: (i % pos_tiles, 0)),
            pl.BlockSpec((d, RET_DK), lambda i, j: (0, j)),
            pl.BlockSpec((d, RET_DK), lambda i, j: (0, k_first + j)),
            pl.BlockSpec((d, RET_DV), lambda i, j: (0, v_first + j)),
            pl.BlockSpec((d, RET_DV), lambda i, j: (0, g_first + j)),
        ] + cast_specs,
        out_specs=[
            pl.BlockSpec((tm, RET_DV), lambda i, j: (i, j)),
            pl.BlockSpec((1, RET_HEADS, RET_DK, RET_DV), lambda i, j: (i // tiles_per_seq, 0, 0, 0)),
        ] + cast_specs,
        out_shape=[
            jax.ShapeDtypeStruct((m, RET_V), BF16),
            jax.ShapeDtypeStruct((batch, RET_HEADS, RET_DK, RET_DV), F32),
        ] + [jax.ShapeDtypeStruct(c.shape, BF16) for c in cast],
        scratch_shapes=[
            pltpu.VMEM((tm, d), BF16),
            pltpu.VMEM((tm, RET_HEAD_COLS), BF16),
            pltpu.VMEM((RET_HEADS, chunk, chunk), F32),
            pltpu.VMEM((RET_HEADS, chunk, 1), F32),
            pltpu.VMEM((RET_HEADS, chunk, 1), F32),
        ],
        compiler_params=_params("arbitrary", "arbitrary"),
        name="ret_fused",
    )(x, gain, cos, sin, w, w, w, w, *cast)
    return outs[0], outs[1], tuple(outs[2:])


def _mm_res_kernel(a_ref, w_ref, r_ref, o_ref):
    o_ref[...] = r_ref[...] + jnp.dot(a_ref[...], w_ref[...], preferred_element_type=F32)


def _mm_res(a, w, res, *, tm, tn):
    m, kdim = a.shape
    n = w.shape[1]
    return pl.pallas_call(
        _mm_res_kernel,
        grid=(m // tm, n // tn),
        in_specs=[
            pl.BlockSpec((tm, kdim), lambda i, j: (i, 0)),
            pl.BlockSpec((kdim, tn), lambda i, j: (0, j)),
            pl.BlockSpec((tm, tn), lambda i, j: (i, j)),
        ],
        out_specs=pl.BlockSpec((tm, tn), lambda i, j: (i, j)),
        out_shape=jax.ShapeDtypeStruct((m, n), F32),
        compiler_params=_params("parallel", "arbitrary"),
        name="matmul_residual",
    )(a, w, res)


def _ffn_kernel(*refs, final_norm):
    if final_norm:
        x_ref, gain_ref, wg_ref, wu_ref, wd_ref, fgain_ref, o_ref, xn_ref = refs
    else:
        x_ref, gain_ref, wg_ref, wu_ref, wd_ref, o_ref, xn_ref = refs
    j = pl.program_id(1)

    @pl.when(j == 0)
    def _():
        x = x_ref[...]
        xn_ref[...] = _rmsnorm(x, gain_ref[...]).astype(BF16)
        o_ref[...] = x

    xn = xn_ref[...]
    gate = jnp.dot(xn, wg_ref[...], preferred_element_type=F32)
    up = jnp.dot(xn, wu_ref[...], preferred_element_type=F32)
    hidden = (_silu(gate) * up).astype(BF16)
    o_ref[...] += jnp.dot(hidden, wd_ref[...], preferred_element_type=F32)

    if final_norm:
        @pl.when(j == pl.num_programs(1) - 1)
        def _():
            o_ref[...] = _rmsnorm(o_ref[...], fgain_ref[...])


def _ffn(x, gain, wg, wu, wd, final_gain=None, *, layer, tm):
    m, d = x.shape
    f = wg.shape[2]
    tf = FFN_COL_TILE if tm >= FFN_ROW_TILE else FFN_COL_TILE_SMALL_ROWS
    final_norm = final_gain is not None
    in_specs = [
        pl.BlockSpec((tm, d), lambda i, j: (i, 0)),
        pl.BlockSpec((1, d), lambda i, j: (0, 0)),
        pl.BlockSpec((None, d, tf), lambda i, j: (layer, 0, j)),
        pl.BlockSpec((None, d, tf), lambda i, j: (layer, 0, j)),
        pl.BlockSpec((None, tf, d), lambda i, j: (layer, j, 0)),
    ]
    args = [x, gain, wg, wu, wd]
    if final_norm:
        in_specs.append(pl.BlockSpec((1, d), lambda i, j: (0, 0)))
        args.append(final_gain)
    return pl.pallas_call(
        functools.partial(_ffn_kernel, final_norm=final_norm),
        grid=(m // tm, f // tf),
        in_specs=in_specs,
        out_specs=pl.BlockSpec((tm, d), lambda i, j: (i, 0)),
        out_shape=jax.ShapeDtypeStruct((m, d), F32),
        scratch_shapes=[pltpu.VMEM((tm, d), BF16)],
        compiler_params=_params("parallel", "arbitrary"),
        name="ffn",
    )(*args)


def _kv_kernel(x_ref, gain_ref, w_ref, bias_ref, k32_ref, v32_ref, kh_ref, vh_ref, lf_ref, lf16_ref, *scratch,
               key_tile, tiles_per_seq):
    if tiles_per_seq is not None:
        carry_ref, = scratch

        @pl.when(pl.program_id(0) % tiles_per_seq == 0)
        def _():
            carry_ref[...] = jnp.zeros_like(carry_ref)

    rows = x_ref.shape[0] // ROW_GROUPS
    for r in range(ROW_GROUPS):
        rs = slice(r * rows, (r + 1) * rows)
        xn = _rmsnorm(x_ref[rs, :], gain_ref[...]).astype(BF16)
        y = jnp.dot(xn, w_ref[...], preferred_element_type=F32)
        z = y[:, :LANES] + bias_ref[...]
        lf = jnp.minimum(z, 0.0) - jnp.log1p(jnp.exp(-jnp.abs(z)))
        lf16_ref[rs, :] = lf[:, :FOX_HEADS]
        if tiles_per_seq is None:
            lf_ref[rs, :] = lf
        else:
            lf_ref[rs, :], carry_ref[...] = _prefix_pieces(lf, carry_ref[...])
        v = y[:, LANES:LANES + FOX_KV]
        k = y[:, LANES + FOX_KV:]
        for h in range(FOX_KV_HEADS):
            hs = slice(h * FOX_HD, (h + 1) * FOX_HD)
            out_rows = pl.ds(r * rows * FOX_KV_HEADS + h, rows, stride=FOX_KV_HEADS)
            v32_ref[out_rows, :] = v[:, hs]
            if key_tile is None:
                vh_ref[h, rs, :] = v[:, hs].astype(BF16)
            else:
                v_t = v[:, hs].T.astype(BF16)
                width = min(rows, key_tile)
                for t in range(rows // width):
                    first = r * rows + t * width
                    tile, lanes = first // key_tile, slice(first % key_tile, first % key_tile + width)
                    vh_ref[h, tile, :FOX_HD, lanes] = v_t[:, t * width:(t + 1) * width]
                    vh_ref[h, tile, FOX_HD:, lanes] = jnp.ones((FOX_V_ROWS - FOX_HD, width), BF16)
        for h in range(FOX_KV_HEADS):
            hs = slice(h * FOX_HD, (h + 1) * FOX_HD)
            out_rows = pl.ds(r * rows * FOX_KV_HEADS + h, rows, stride=FOX_KV_HEADS)
            k32_ref[out_rows, :] = k[:, hs]
            kh_ref[h, rs, :] = k[:, hs].astype(BF16)


def _kv_proj(x, gain, w, bias, *, tm, key_tile, seq=None):
    m, d = x.shape
    n = w.shape[1]
    tiles_per_seq = None if seq is None else seq // tm
    if key_tile is None:
        v_spec = pl.BlockSpec((FOX_KV_HEADS, tm, FOX_HD), lambda i: (0, i, 0))
        v_shape = jax.ShapeDtypeStruct((FOX_KV_HEADS, m, FOX_HD), BF16)
    else:
        v_spec = pl.BlockSpec((FOX_KV_HEADS, tm // key_tile, FOX_V_ROWS, key_tile), lambda i: (0, i, 0, 0))
        v_shape = jax.ShapeDtypeStruct((FOX_KV_HEADS, m // key_tile, FOX_V_ROWS, key_tile), BF16)
    return pl.pallas_call(
        functools.partial(_kv_kernel, key_tile=key_tile, tiles_per_seq=tiles_per_seq),
        grid=(m // tm,),
        in_specs=[
            pl.BlockSpec((tm, d), lambda i: (i, 0)),
            pl.BlockSpec((1, d), lambda i: (0, 0)),
            pl.BlockSpec((d, n), lambda i: (0, 0)),
            pl.BlockSpec((1, LANES), lambda i: (0, 0)),
        ],
        out_specs=[
            pl.BlockSpec((tm * FOX_KV_HEADS, FOX_HD), lambda i: (i, 0)),
            pl.BlockSpec((tm * FOX_KV_HEADS, FOX_HD), lambda i: (i, 0)),
            pl.BlockSpec((FOX_KV_HEADS, tm, FOX_HD), lambda i: (0, i, 0)),
            v_spec,
            pl.BlockSpec((tm, LANES), lambda i: (i, 0)),
            pl.BlockSpec((tm, FOX_HEADS), lambda i: (i, 0)),
        ],
        out_shape=[
            jax.ShapeDtypeStruct((m * FOX_KV_HEADS, FOX_HD), F32),
            jax.ShapeDtypeStruct((m * FOX_KV_HEADS, FOX_HD), F32),
            jax.ShapeDtypeStruct((FOX_KV_HEADS, m, FOX_HD), BF16),
            v_shape,
            jax.ShapeDtypeStruct((m, LANES), F32 if seq is None else BF16),
            jax.ShapeDtypeStruct((m, FOX_HEADS), F32),
        ],
        scratch_shapes=[] if seq is None else [pltpu.VMEM((1, LANES), F32)],
        compiler_params=_params("arbitrary"),
        name="kv_proj",
    )(x, gain, w, bias)


def _q_kernel(x_ref, gain_ref, w_ref, o_ref):
    rows = x_ref.shape[0] // ROW_GROUPS
    for r in range(ROW_GROUPS):
        rs = slice(r * rows, (r + 1) * rows)
        xn = _rmsnorm(x_ref[rs, :], gain_ref[...]).astype(BF16)
        y = jnp.dot(xn, w_ref[...], preferred_element_type=F32)
        o_ref[rs, :] = (y * (FOX_HD ** -0.5 * LOG2_E)).astype(BF16)


def _q_proj(x, gain, w, *, tm):
    m, d = x.shape
    n = w.shape[1]
    return pl.pallas_call(
        _q_kernel,
        grid=(m // tm,),
        in_specs=[
            pl.BlockSpec((tm, d), lambda i: (i, 0)),
            pl.BlockSpec((1, d), lambda i: (0, 0)),
            pl.BlockSpec((d, n), lambda i: (0, 0)),
        ],
        out_specs=pl.BlockSpec((tm, n), lambda i: (i, 0)),
        out_shape=jax.ShapeDtypeStruct((m, n), BF16),
        compiler_params=_params("parallel"),
        name="q_proj",
    )(x, gain, w)


def _prefix_pieces(x, carry):
    tl = x.shape[0]
    r = lax.broadcasted_iota(jnp.int32, (tl, tl), 0)
    c = lax.broadcasted_iota(jnp.int32, (tl, tl), 1)
    tri = jnp.where(c <= r, 1.0, 0.0).astype(BF16)
    x_hi = x.astype(BF16)
    x_rest = x - x_hi.astype(F32)
    x_mid = x_rest.astype(BF16)
    x_lo = (x_rest - x_mid.astype(F32)).astype(BF16)
    y = (jnp.dot(tri, x_lo, preferred_element_type=F32) + jnp.dot(tri, x_mid, preferred_element_type=F32)
         + jnp.dot(tri, x_hi, preferred_element_type=F32) + carry)
    y2 = y * LOG2_E
    hi = y2.astype(BF16).astype(F32)
    rest = y2 - hi
    mid = rest.astype(BF16).astype(F32)
    lo = rest - mid
    lane = lax.broadcasted_iota(jnp.int32, y.shape, 1)
    packed = jnp.where(
        lane < FOX_HEADS, hi,
        jnp.where(lane < 2 * FOX_HEADS, pltpu.roll(mid, FOX_HEADS, 1),
                  jnp.where(lane < 3 * FOX_HEADS, pltpu.roll(lo, 2 * FOX_HEADS, 1), 0.0)))
    return packed.astype(BF16), y[tl - 1:tl, :]


def _cumsum_kernel(x_ref, o_ref, carry_ref):
    @pl.when(pl.program_id(1) == 0)
    def _():
        carry_ref[...] = jnp.zeros_like(carry_ref)

    o_ref[0], carry_ref[...] = _prefix_pieces(x_ref[0], carry_ref[...])


def _cumsum_pieces(x, *, tl):
    b, length, lanes = x.shape
    return pl.pallas_call(
        _cumsum_kernel,
        grid=(b, length // tl),
        in_specs=[pl.BlockSpec((1, tl, lanes), lambda i, t: (i, t, 0))],
        out_specs=pl.BlockSpec((1, tl, lanes), lambda i, t: (i, t, 0)),
        out_shape=jax.ShapeDtypeStruct((b, length, lanes), BF16),
        scratch_shapes=[pltpu.VMEM((1, lanes), F32)],
        compiler_params=_params("parallel", "arbitrary"),
        name="logf_cumsum",
    )(x)


def _fox_kernel(q_ref, k_ref, f_ref, v_ref, o_ref, qa_ref, z0_ref, z1_ref, mb0_ref, mb1_ref, m_ref, acc_ref,
                *, tq, tk, q_start, blocks):
    step = pl.program_id(2)
    rows = FOX_GROUP * tq
    lane = lax.broadcasted_iota(jnp.int32, (tq, FOX_HD), 1)
    for x in range(blocks):
        for g in range(FOX_GROUP):
            head = pl.program_id(1) * FOX_GROUP + g
            pick = jnp.where(lane < FOX_F_PIECES * FOX_HEADS,
                             jnp.where((lane & (FOX_HEADS - 1)) == head, -1.0, 0.0), 0.0)
            qa_ref[x, g * tq:(g + 1) * tq, :FOX_HD] = q_ref[0, x * tq:(x + 1) * tq, g * FOX_HD:(g + 1) * FOX_HD]
            qa_ref[x, g * tq:(g + 1) * tq, FOX_HD:] = pick.astype(BF16)
    n_full = (q_start + step * blocks * tq) // tk

    m_ref[...] = jnp.full_like(m_ref, MASK_VALUE)
    acc_ref[...] = jnp.zeros_like(acc_ref)

    cw = min(rows, FOX_COL_CHUNK)
    chunks = [slice(c * cw, (c + 1) * cw) for c in range(rows // cw)]
    buffers = ((z0_ref, mb0_ref), (z1_ref, mb1_ref))

    def score(x, kb, masked, buf, keys=tk):
        z_ref, mb_ref = buffers[buf]
        start = pl.multiple_of(kb * tk, tk)
        ka = jnp.concatenate([k_ref[0, 0, pl.ds(start, keys), :], f_ref[0, pl.ds(start, keys), :]], axis=1)
        for c, cs in enumerate(chunks):
            z = lax.dot_general(ka, qa_ref[x, cs, :], (((1,), (1,)), ((), ())), preferred_element_type=F32)
            if masked:
                kpos = kb * tk + lax.broadcasted_iota(jnp.int32, (keys, cw), 0)
                col = c * cw + lax.broadcasted_iota(jnp.int32, (keys, cw), 1)
                qpos = q_start + (step * blocks + x) * tq + (col & (tq - 1))
                z = jnp.where(kpos <= qpos, z, MASK_VALUE)
            z_ref[:keys, cs] = z
            mb_ref[:, cs] = jnp.max(z, axis=0, keepdims=True)

    def accumulate(x, kb, buf, keys=tk):
        z_ref, mb_ref = buffers[buf]
        vt = v_ref[0, 0, kb, :, :keys]
        for cs in chunks:
            m_old = m_ref[x, :, cs]
            m_new = jnp.maximum(m_old, mb_ref[:, cs])
            alpha = jnp.exp2(m_old - m_new)
            p = jnp.exp2(z_ref[:keys, cs] - m_new)
            acc_ref[x, :, cs] = alpha * acc_ref[x, :, cs] + jnp.dot(vt, p.astype(BF16), preferred_element_type=F32)
            m_ref[x, :, cs] = m_new

    if blocks == 4:
        def run(stages, pending):
            for x, kb, masked, keys in stages:
                buf = 1 - pending[2]
                score(x, kb, masked, buf, keys)
                accumulate(*pending)
                pending = (x, kb, buf, keys)
            return pending

        score(0, n_full, True, 0, tq)
        pending = run([(1, n_full, True, tk), (2, n_full + 1, True, tq), (3, n_full + 1, True, tk),
                       (2, n_full, False, tk), (3, n_full, False, tk)], (0, n_full, 0, tq))
        last_block, _, last_buf, _ = pending

        def trip(i, kb_pending):
            stages = [(x, FOX_TILES_PER_TRIP * i + t, False, tk)
                      for t in range(FOX_TILES_PER_TRIP) for x in range(blocks)]
            return run(stages, (last_block, kb_pending, last_buf, tk))[1]

        kb_pending = lax.fori_loop(0, n_full // FOX_TILES_PER_TRIP, trip, n_full)
        accumulate(last_block, kb_pending, last_buf)
    else:
        score(0, n_full, True, 0)

        def pair(i, pending):
            score(0, 2 * i, False, 1)
            accumulate(0, pending, 0)
            score(0, 2 * i + 1, False, 0)
            accumulate(0, 2 * i, 1)
            return 2 * i + 1

        pending = lax.fori_loop(0, n_full // 2, pair, n_full)

        @pl.when(n_full % 2 == 1)
        def _():
            score(0, n_full - 1, False, 1)
            accumulate(0, pending, 0)
            accumulate(0, n_full - 1, 1)

        @pl.when(n_full % 2 == 0)
        def _():
            accumulate(0, pending, 0)

    for x in range(blocks):
        acc = acc_ref[x]
        out = (acc[:FOX_HD] / acc[FOX_HD:FOX_HD + 1]).T
        for g in range(FOX_GROUP):
            o_ref[0, x * tq:(x + 1) * tq, g * FOX_HD:(g + 1) * FOX_HD] = out[g * tq:(g + 1) * tq, :].astype(BF16)


def _fox_attention(q, k_heads, f_pieces, v_t, *, tq, tk, q_start):
    b, lq, _ = q.shape
    lk = k_heads.shape[2]
    nk = lk // tk
    rows = FOX_GROUP * tq
    span = FOX_TILES_PER_TRIP * tk
    blocks = FOX_BLOCKS_PER_STEP if (FOX_BLOCKS_PER_STEP * tq == span and lq % span == 0 and q_start % span == 0) else 1
    assert tq & (tq - 1) == 0 and tk % tq == 0 and q_start % tk == 0
    return pl.pallas_call(
        functools.partial(_fox_kernel, tq=tq, tk=tk, q_start=q_start, blocks=blocks),
        grid=(b, FOX_KV_HEADS, lq // (blocks * tq)),
        in_specs=[
            pl.BlockSpec((1, blocks * tq, FOX_GROUP * FOX_HD), lambda i, h, t: (i, t, h)),
            pl.BlockSpec((1, 1, lk, FOX_HD), lambda i, h, t: (h, i, 0, 0)),
            pl.BlockSpec((1, lk, LANES), lambda i, h, t: (i, 0, 0)),
            pl.BlockSpec((1, 1, nk, FOX_V_ROWS, tk), lambda i, h, t: (h, i, 0, 0, 0)),
        ],
        out_specs=pl.BlockSpec((1, blocks * tq, FOX_GROUP * FOX_HD), lambda i, h, t: (i, t, h)),
        out_shape=jax.ShapeDtypeStruct(q.shape, BF16),
        scratch_shapes=[
            pltpu.VMEM((blocks, rows, 2 * FOX_HD), BF16),
            pltpu.VMEM((tk, rows), F32),
            pltpu.VMEM((tk, rows), F32),
            pltpu.VMEM((1, rows), F32),
            pltpu.VMEM((1, rows), F32),
            pltpu.VMEM((blocks, 1, rows), F32),
            pltpu.VMEM((blocks, FOX_V_ROWS, rows), F32),
        ],
        compiler_params=_params("parallel", "parallel", "arbitrary"),
        name="fox_attention",
    )(q, k_heads, f_pieces, v_t)


def _rope_tables(pos):
    half = RET_DK // 2
    inv = ROPE_BASE ** (-jnp.arange(half, dtype=F32) / half)
    ang = pos.astype(F32)[:, None] * inv[None, :]
    return jnp.cos(ang), jnp.sin(ang)


def _with_cache(new_heads, cache, lk_pad):
    past = cache.transpose(2, 0, 1, 3).astype(BF16)
    both = jnp.concatenate([past, new_heads], axis=2)
    return jnp.pad(both, ((0, 0), (0, 0), (0, lk_pad - both.shape[2]), (0, 0)))


def _transpose_values(v_heads, tk):
    kvh, b, lk, _ = v_heads.shape
    v_t = v_heads.reshape(kvh, b, lk // tk, tk, FOX_HD).transpose(0, 1, 2, 4, 3)
    ones = jnp.ones((kvh, b, lk // tk, FOX_V_ROWS - FOX_HD, tk), BF16)
    return jnp.concatenate([v_t, ones], axis=3)


def _stream(x, *, pos, s0, cache, weights, later_weights, tm, ffn_tm, chunk, tq):
    norm_mix, norm_ffn, norm_kv, norm_final, w_ret_in, w_ret_o, w_kv, b_kv, w_fox_q, w_fox_o = weights
    b, length, d = x.shape
    m = b * length
    h = x.reshape(m, d)

    cos, sin = _rope_tables(pos)
    if cos.shape[0] < tm:
        cos = jnp.tile(cos, (tm // cos.shape[0], 1))
        sin = jnp.tile(sin, (tm // sin.shape[0], 1))
    cast = () if later_weights[0].dtype == BF16 else later_weights
    if s0 is None:
        o, s_fin, narrowed = _ret_fused(h, norm_mix[0:1], cos, sin, w_ret_in, cast,
                                        batch=b, seq=length, tm=tm, chunk=chunk)
    else:
        proj, narrowed = _ret_in(h, norm_mix[0:1], cos, sin, w_ret_in, cast, tm=tm)
        o, s_fin = _retention(proj, s0, batch=b, seq=length, chunk=chunk)
    later_weights = narrowed if cast else later_weights
    w_ffn_gate, w_ffn_up, w_ffn_down = later_weights
    h = _mm_res(o, w_ret_o, h, tm=tm, tn=COL_TILE)
    h = _ffn(h, norm_ffn[0:1], w_ffn_gate, w_ffn_up, w_ffn_down, layer=0, tm=ffn_tm)

    tk = FOX_K_TILE
    if cache is None:
        q_start = 0
        k32, v32, k_heads, v_t, f_pieces, lf = _kv_proj(
            h, norm_kv[None, :], w_kv, b_kv, tm=tm, key_tile=tk, seq=length)
        k_heads = k_heads.reshape(FOX_KV_HEADS, b, length, FOX_HD)
        v_t = v_t.reshape(FOX_KV_HEADS, b, length // tk, FOX_V_ROWS, tk)
        f_pieces = f_pieces.reshape(b, length, LANES)
    else:
        cache_k, cache_v, cache_logf = cache
        q_start = cache_k.shape[1]
        lk_pad = -(-(q_start + length) // tk) * tk
        k32, v32, k_heads, v_heads, lf_pad, lf = _kv_proj(h, norm_kv[None, :], w_kv, b_kv, tm=tm, key_tile=None)
        k_heads = _with_cache(k_heads.reshape(FOX_KV_HEADS, b, length, FOX_HD), cache_k, lk_pad)
        v_t = _transpose_values(_with_cache(v_heads.reshape(FOX_KV_HEADS, b, length, FOX_HD), cache_v, lk_pad), tk)
        lf_all = jnp.concatenate([
            jnp.pad(cache_logf.astype(F32), ((0, 0), (0, 0), (0, LANES - FOX_HEADS))),
            lf_pad.reshape(b, length, LANES)], axis=1)
        lf_all = jnp.pad(lf_all, ((0, 0), (0, lk_pad - lf_all.shape[1]), (0, 0)))
        f_pieces = _cumsum_pieces(lf_all, tl=CUMSUM_TILE)
    lf = lf.reshape(b, length, FOX_HEADS)

    q = _q_proj(h, norm_mix[1:2], w_fox_q, tm=tm).reshape(b, length, d)
    att = _fox_attention(q, k_heads, f_pieces, v_t, tq=tq, tk=tk, q_start=q_start)
    h = _mm_res(att.reshape(m, d), w_fox_o, h, tm=min(tm, ROW_TILE // 2), tn=d)
    y = _ffn(h, norm_ffn[1:2], w_ffn_gate, w_ffn_up, w_ffn_down, norm_final[None, :], layer=1, tm=ffn_tm)

    outputs = (y.reshape(b, length, d), s_fin[None],
               k32.reshape(b, length, FOX_KV_HEADS, FOX_HD), v32.reshape(b, length, FOX_KV_HEADS, FOX_HD), lf)
    return outputs, later_weights


def kernel(x_prompt, x_sample, state_ret, cache_k, cache_v, cache_logf, norm_mix, norm_ffn, norm_kv, norm_final, w_ret_in, w_ret_o, w_kv_k, w_kv_v, w_kv_f, b_kv_f, w_fox_q, w_fox_o, w_ffn_gate, w_ffn_up, w_ffn_down):
    lp = x_prompt.shape[1]
    bs, ls, _ = x_sample.shape
    past = cache_k.shape[1]

    w_kv = jnp.concatenate(
        [jnp.pad(w_kv_f, ((0, 0), (0, LANES - FOX_HEADS))), w_kv_v, w_kv_k], axis=1).astype(BF16)
    b_kv = jnp.pad(b_kv_f.astype(F32), (0, LANES - FOX_HEADS))[None, :]
    weights = (norm_mix.astype(F32), norm_ffn.astype(F32), norm_kv.astype(F32), norm_final.astype(F32),
               w_ret_in[0].astype(BF16), w_ret_o[0].astype(BF16), w_kv, b_kv,
               w_fox_q[0].astype(BF16), w_fox_o[0].astype(BF16))
    later_weights = tuple(w.astype(F32) for w in (w_ffn_gate, w_ffn_up, w_ffn_down))

    (y_p, s_p, k_p, v_p, lf_p), later_weights = _stream(
        x_prompt, pos=jnp.arange(lp), s0=None, cache=None, weights=weights, later_weights=later_weights,
        tm=ROW_TILE, ffn_tm=FFN_ROW_TILE, chunk=RET_CHUNK, tq=FOX_Q_TILE)
    (y_s, s_s, k_s, v_s, lf_s), _ = _stream(
        x_sample, pos=past + jnp.arange(ls), s0=state_ret[0], cache=(cache_k, cache_v, cache_logf),
        weights=weights, later_weights=later_weights, tm=bs * ls, ffn_tm=bs * ls, chunk=ls, tq=ls)
    return (y_p, y_s, s_p, k_p, v_p, lf_p, s_s, k_s, v_s, lf_s)
```

```python
import functools

import jax
import jax.numpy as jnp
from jax import lax
from jax.experimental import pallas as pl
from jax.experimental.pallas import tpu as pltpu

F32 = jnp.float32
BF16 = jnp.bfloat16

D_MODEL = 2048
RET_HEADS = 8
RET_DK = D_MODEL // RET_HEADS
RET_DV = 2 * D_MODEL // RET_HEADS
RET_QK = RET_HEADS * RET_DK
RET_V = RET_HEADS * RET_DV
RET_HEAD_COLS = 2 * RET_DK + 2 * RET_DV
ROPE_BASE = 10000.0
FOX_HEADS = 16
FOX_HD = D_MODEL // FOX_HEADS
FOX_KV_HEADS = 4
FOX_GROUP = FOX_HEADS // FOX_KV_HEADS
FOX_KV = FOX_KV_HEADS * FOX_HD
FOX_F_PIECES = 3
FOX_V_ROWS = FOX_HD + 16
LOG2_E = 1.4426950408889634
NORM_EPS = 1e-6
GN_EPS = 1e-5

LANES = 128
BF16_SUBLANES = 16
VMEM_LIMIT_BYTES = 60 * 1024 * 1024

ROW_TILE = 1024
COL_TILE = 1024
ROW_GROUPS = 4
FFN_ROW_TILE = 1024
FFN_COL_TILE = 512
FFN_COL_TILE_SMALL_ROWS = 1408
RET_CHUNK = 256
RET_HEADS_PER_STEP = 8
FOX_Q_TILE = 256
FOX_K_TILE = 512
FOX_BLOCKS_PER_STEP = 4
FOX_TILES_PER_TRIP = 2
FOX_COL_CHUNK = 512
CUMSUM_TILE = 512
MASK_VALUE = -1e30


def _params(*semantics):
    return pltpu.CompilerParams(dimension_semantics=semantics, vmem_limit_bytes=VMEM_LIMIT_BYTES)


def _rmsnorm(x, gain):
    y = x * lax.rsqrt(jnp.mean(x * x, axis=-1, keepdims=True) + NORM_EPS)
    return y * gain


def _silu(x):
    return x * (1.0 / (1.0 + jnp.exp(-x)))


def _ret_in_kernel(x_ref, gain_ref, cos_ref, sin_ref, wq_ref, wk_ref, wv_ref, wg_ref, *rest, n_cast):
    cast_in = rest[:n_cast]
    o_ref = rest[n_cast]
    cast_out = rest[n_cast + 1:2 * n_cast + 1]
    xn_ref = rest[2 * n_cast + 1]

    @pl.when(pl.program_id(1) == 0)
    def _():
        xn_ref[...] = _rmsnorm(x_ref[...], gain_ref[...]).astype(BF16)

    for src, dst in zip(cast_in, cast_out):
        dst[...] = src[...].astype(BF16)

    w = jnp.concatenate([wg_ref[...], wq_ref[...], wk_ref[...], wv_ref[...]], axis=1)
    y = jnp.dot(xn_ref[...], w, preferred_element_type=F32)
    cos = cos_ref[...]
    sin = sin_ref[...]
    half = RET_DK // 2
    v_first = 2 * RET_DK
    g_first = v_first + RET_DV
    o_ref[:, g_first:] = _silu(y[:, :RET_DV]).astype(BF16)
    for src, dst, scale in ((RET_DV, 0, 1.0), (RET_DV + RET_DK, RET_DK, RET_DK ** -0.5)):
        x1 = y[:, src:src + half]
        x2 = y[:, src + half:src + RET_DK]
        o_ref[:, dst:dst + half] = ((x1 * cos - x2 * sin) * scale).astype(BF16)
        o_ref[:, dst + half:dst + RET_DK] = ((x1 * sin + x2 * cos) * scale).astype(BF16)
    o_ref[:, v_first:g_first] = y[:, RET_DV + 2 * RET_DK:].astype(BF16)


def _slab_spec(shape, steps):
    layers, rows, cols = shape
    slab = next(r for r in range(BF16_SUBLANES, rows + 1, BF16_SUBLANES)
                if rows % r == 0 and layers * rows // r <= steps)
    per_layer = rows // slab
    last = layers * per_layer - 1

    def index(i, j):
        b = jnp.minimum(i * RET_HEADS + j, last)
        return (b // per_layer, b % per_layer, 0)

    return pl.BlockSpec((None, slab, cols), index)


def _ret_in(x, gain, cos, sin, w, cast=(), *, tm):
    m, d = x.shape
    pos_tiles = cos.shape[0] // tm
    k_first = RET_QK // RET_DK
    v_first = 2 * RET_QK // RET_DV
    g_first = (2 * RET_QK + RET_V) // RET_DV
    grid = (m // tm, RET_HEADS)
    cast_specs = [_slab_spec(c.shape, grid[0] * grid[1]) for c in cast]
    outs = pl.pallas_call(
        functools.partial(_ret_in_kernel, n_cast=len(cast)),
        grid=grid,
        in_specs=[
            pl.BlockSpec((tm, d), lambda i, j: (i, 0)),
            pl.BlockSpec((1, d), lambda i, j: (0, 0)),
            pl.BlockSpec((tm, RET_DK // 2), lambda i, j: (i % pos_tiles, 0)),
            pl.BlockSpec((tm, RET_DK // 2), lambda i, j: (i % pos_tiles, 0)),
            pl.BlockSpec((d, RET_DK), lambda i, j: (0, j)),
            pl.BlockSpec((d, RET_DK), lambda i, j: (0, k_first + j)),
            pl.BlockSpec((d, RET_DV), lambda i, j: (0, v_first + j)),
            pl.BlockSpec((d, RET_DV), lambda i, j: (0, g_first + j)),
        ] + cast_specs,
        out_specs=[pl.BlockSpec((tm, RET_HEAD_COLS), lambda i, j: (i, j))] + cast_specs,
        out_shape=[jax.ShapeDtypeStruct((m, RET_HEADS * RET_HEAD_COLS), BF16)]
        + [jax.ShapeDtypeStruct(c.shape, BF16) for c in cast],
        scratch_shapes=[pltpu.VMEM((tm, d), BF16)],
        compiler_params=_params("arbitrary", "arbitrary"),
        name="ret_in_proj",
    )(x, gain, cos, sin, w, w, w, w, *cast)
    return outs[0], tuple(outs[1:])


def _retention_kernel(*refs, chunk, has_init):
    if has_init:
        p_ref, s0_ref, o_ref, s_ref, dm_ref, rd_ref, kw_ref, raw_ref = refs
    else:
        p_ref, o_ref, s_ref, dm_ref, rd_ref, kw_ref, raw_ref = refs
        s0_ref = None
    c = pl.program_id(2)
    heads = RET_HEADS_PER_STEP

    def log_gamma(hh):
        head = pl.program_id(1) * heads + hh
        return jnp.log(1.0 - jnp.exp2(-5.0 - jnp.full((1, 1), head, jnp.int32).astype(F32)))

    @pl.when(c == 0)
    def _():
        li = lax.broadcasted_iota(jnp.int32, (chunk, chunk), 0)
        mi = lax.broadcasted_iota(jnp.int32, (chunk, chunk), 1)
        diff = (li - mi).astype(F32)
        idx = lax.broadcasted_iota(jnp.int32, (chunk, 1), 0).astype(F32)
        for hh in range(heads):
            lg = log_gamma(hh)
            dm_ref[hh] = jnp.where(diff >= 0, jnp.exp(lg * jnp.maximum(diff, 0.0)), 0.0)
            rd_ref[hh] = jnp.exp(lg * (idx + 1.0))
            kw_ref[hh] = jnp.exp(lg * (chunk - 1.0 - idx))
        if has_init:
            s_ref[...] = s0_ref[...]
        else:
            s_ref[...] = jnp.zeros_like(s_ref)

    for hh in range(heads):
        base = hh * RET_HEAD_COLS
        q = p_ref[:, base:base + RET_DK]
        k = p_ref[:, base + RET_DK:base + 2 * RET_DK]
        v = p_ref[:, base + 2 * RET_DK:base + 2 * RET_DK + RET_DV]
        s_old = s_ref[0, hh]
        scores = lax.dot_general(q, k, (((1,), (1,)), ((), ())), preferred_element_type=F32) * dm_ref[hh]
        inner = jnp.dot(scores.astype(BF16), v, preferred_element_type=F32)
        cross = jnp.dot(q, s_old.astype(BF16), preferred_element_type=F32) * rd_ref[hh]
        raw_ref[hh] = inner + cross
        kd = (k.astype(F32) * kw_ref[hh]).astype(BF16)
        s_ref[0, hh] = jnp.exp(log_gamma(hh) * float(chunk)) * s_old + lax.dot_general(
            kd, v, (((0,), (0,)), ((), ())), preferred_element_type=F32)

    for hh in range(heads):
        o = raw_ref[hh]
        mu = jnp.mean(o, axis=-1, keepdims=True)
        cen = o - mu
        var = jnp.mean(cen * cen, axis=-1, keepdims=True)
        on = cen * lax.rsqrt(var + GN_EPS)
        gate = p_ref[:, (hh + 1) * RET_HEAD_COLS - RET_DV:(hh + 1) * RET_HEAD_COLS].astype(F32)
        o_ref[:, hh * RET_DV:(hh + 1) * RET_DV] = (on * gate).astype(BF16)


def _retention(proj, s0, *, batch, seq, chunk):
    nc = seq // chunk
    has_init = s0 is not None
    heads = RET_HEADS_PER_STEP
    v_w = heads * RET_DV
    in_specs = [pl.BlockSpec((chunk, heads * RET_HEAD_COLS), lambda b, h, c: (b * nc + c, h))]
    args = [proj]
    if has_init:
        in_specs.append(pl.BlockSpec((1, heads, RET_DK, RET_DV), lambda b, h, c: (b, h, 0, 0)))
        args.append(s0)
    return pl.pallas_call(
        functools.partial(_retention_kernel, chunk=chunk, has_init=has_init),
        grid=(batch, RET_HEADS // heads, nc),
        in_specs=in_specs,
        out_specs=[
            pl.BlockSpec((chunk, v_w), lambda b, h, c: (b * nc + c, h)),
            pl.BlockSpec((1, heads, RET_DK, RET_DV), lambda b, h, c: (b, h, 0, 0)),
        ],
        out_shape=[
            jax.ShapeDtypeStruct((batch * seq, RET_V), BF16),
            jax.ShapeDtypeStruct((batch, RET_HEADS, RET_DK, RET_DV), F32),
        ],
        scratch_shapes=[
            pltpu.VMEM((heads, chunk, chunk), F32),
            pltpu.VMEM((heads, chunk, 1), F32),
            pltpu.VMEM((heads, chunk, 1), F32),
            pltpu.VMEM((heads, chunk, RET_DV), F32),
        ],
        compiler_params=_params("parallel", "parallel", "arbitrary"),
        name="retention",
    )(*args)


def _ret_fused_kernel(x_ref, gain_ref, cos_ref, sin_ref, wq_ref, wk_ref, wv_ref, wg_ref, *rest,
                      n_cast, chunk, tiles_per_seq):
    cast_in = rest[:n_cast]
    o_ref, state_ref = rest[n_cast:n_cast + 2]
    cast_out = rest[n_cast + 2:2 * n_cast + 2]
    xn_ref, p_ref, dm_ref, rd_ref, kw_ref = rest[2 * n_cast + 2:]
    s_ref = state_ref.at[0]
    i = pl.program_id(0)
    head = pl.program_id(1)
    lg = jnp.log(1.0 - jnp.exp2(-5.0 - jnp.full((1, 1), head, jnp.int32).astype(F32)))

    @pl.when(head == 0)
    def _():
        xn_ref[...] = _rmsnorm(x_ref[...], gain_ref[...]).astype(BF16)

    @pl.when(i == 0)
    def _():
        li = lax.broadcasted_iota(jnp.int32, (chunk, chunk), 0)
        mi = lax.broadcasted_iota(jnp.int32, (chunk, chunk), 1)
        diff = (li - mi).astype(F32)
        idx = lax.broadcasted_iota(jnp.int32, (chunk, 1), 0).astype(F32)
        dm_ref[head] = jnp.where(diff >= 0, jnp.exp(lg * jnp.maximum(diff, 0.0)), 0.0)
        rd_ref[head] = jnp.exp(lg * (idx + 1.0))
        kw_ref[head] = jnp.exp(lg * (chunk - 1.0 - idx))

    @pl.when(i % tiles_per_seq == 0)
    def _():
        s_ref[head] = jnp.zeros((RET_DK, RET_DV), F32)

    for src, dst in zip(cast_in, cast_out):
        dst[...] = src[...].astype(BF16)

    w = jnp.concatenate([wg_ref[...], wq_ref[...], wk_ref[...], wv_ref[...]], axis=1)
    half = RET_DK // 2
    v_first = 2 * RET_DK
    g_first = v_first + RET_DV

    def project(rows):
        y = jnp.dot(xn_ref[rows, :], w, preferred_element_type=F32)
        cos = cos_ref[rows, :]
        sin = sin_ref[rows, :]
        p_ref[rows, g_first:] = _silu(y[:, :RET_DV]).astype(BF16)
        for src, dst, scale in ((RET_DV, 0, 1.0), (RET_DV + RET_DK, RET_DK, RET_DK ** -0.5)):
            x1 = y[:, src:src + half]
            x2 = y[:, src + half:src + RET_DK]
            p_ref[rows, dst:dst + half] = ((x1 * cos - x2 * sin) * scale).astype(BF16)
            p_ref[rows, dst + half:dst + RET_DK] = ((x1 * sin + x2 * cos) * scale).astype(BF16)
        p_ref[rows, v_first:g_first] = y[:, RET_DV + 2 * RET_DK:].astype(BF16)

    def recur(rows):
        q = p_ref[rows, :RET_DK]
        k = p_ref[rows, RET_DK:v_first]
        v = p_ref[rows, v_first:g_first]
        s_old = s_ref[head]
        scores = lax.dot_general(q, k, (((1,), (1,)), ((), ())), preferred_element_type=F32) * dm_ref[head]
        inner = jnp.dot(scores.astype(BF16), v, preferred_element_type=F32)
        cross = jnp.dot(q, s_old.astype(BF16), preferred_element_type=F32) * rd_ref[head]
        kd = (k.astype(F32) * kw_ref[head]).astype(BF16)
        s_ref[head] = jnp.exp(lg * float(chunk)) * s_old + lax.dot_general(
            kd, v, (((0,), (0,)), ((), ())), preferred_element_type=F32)
        o = inner + cross
        mu = jnp.mean(o, axis=-1, keepdims=True)
        cen = o - mu
        var = jnp.mean(cen * cen, axis=-1, keepdims=True)
        on = cen * lax.rsqrt(var + GN_EPS)
        o_ref[rows, :] = (on * p_ref[rows, g_first:].astype(F32)).astype(BF16)

    groups = [slice(g * chunk, (g + 1) * chunk) for g in range(x_ref.shape[0] // chunk)]
    project(groups[0])
    for g in range(1, len(groups)):
        project(groups[g])
        recur(groups[g - 1])
    recur(groups[-1])


def _ret_fused(x, gain, cos, sin, w, cast=(), *, batch, seq, tm, chunk):
    m, d = x.shape
    pos_tiles = cos.shape[0] // tm
    tiles_per_seq = seq // tm
    k_first = RET_QK // RET_DK
    v_first = 2 * RET_QK // RET_DV
    g_first = (2 * RET_QK + RET_V) // RET_DV
    grid = (m // tm, RET_HEADS)
    cast_specs = [_slab_spec(c.shape, grid[0] * grid[1]) for c in cast]
    outs = pl.pallas_call(
        functools.partial(_ret_fused_kernel, n_cast=len(cast), chunk=chunk, tiles_per_seq=tiles_per_seq),
        grid=grid,
        in_specs=[
            pl.BlockSpec((tm, d), lambda i, j: (i, 0)),
            pl.BlockSpec((1, d), lambda i, j: (0, 0)),
            pl.BlockSpec((tm, RET_DK // 2), lambda i, j: (i % pos_tiles, 0)),
            pl.BlockSpec((tm, RET_DK // 2), lambda i, j: (i % pos_tiles, 0)),
            pl.BlockSpec((d, RET_DK), lambda i, j: (0, j)),
            pl.BlockSpec((d, RET_DK), lambda i, j: (0, k_first + j)),
            pl.BlockSpec((d, RET_DV), lambda i, j: (0, v_first + j)),
            pl.BlockSpec((d, RET_DV), lambda i, j: (0, g_first + j)),
        ] + cast_specs,
        out_specs=[
            pl.BlockSpec((tm, RET_DV), lambda i, j: (i, j)),
            pl.BlockSpec((1, RET_HEADS, RET_DK, RET_DV), lambda i, j: (i // tiles_per_seq, 0, 0, 0)),
        ] + cast_specs,
        out_shape=[
            jax.ShapeDtypeStruct((m, RET_V), BF16),
            jax.ShapeDtypeStruct((batch, RET_HEADS, RET_DK, RET_DV), F32),
        ] + [jax.ShapeDtypeStruct(c.shape, BF16) for c in cast],
        scratch_shapes=[
            pltpu.VMEM((tm, d), BF16),
            pltpu.VMEM((tm, RET_HEAD_COLS), BF16),
            pltpu.VMEM((RET_HEADS, chunk, chunk), F32),
            pltpu.VMEM((RET_HEADS, chunk, 1), F32),
            pltpu.VMEM((RET_HEADS, chunk, 1), F32),
        ],
        compiler_params=_params("arbitrary", "arbitrary"),
        name="ret_fused",
    )(x, gain, cos, sin, w, w, w, w, *cast)
    return outs[0], outs[1], tuple(outs[2:])


def _mm_res_kernel(a_ref, w_ref, r_ref, o_ref):
    o_ref[...] = r_ref[...] + jnp.dot(a_ref[...], w_ref[...], preferred_element_type=F32)


def _mm_res(a, w, res, *, tm, tn):
    m, kdim = a.shape
    n = w.shape[1]
    w_mode = dict(pipeline_mode=pl.Buffered(1)) if tn == n else {}
    return pl.pallas_call(
        _mm_res_kernel,
        grid=(m // tm, n // tn),
        in_specs=[
            pl.BlockSpec((tm, kdim), lambda i, j: (i, 0)),
            pl.BlockSpec((kdim, tn), lambda i, j: (0, j), **w_mode),
            pl.BlockSpec((tm, tn), lambda i, j: (i, j)),
        ],
        out_specs=pl.BlockSpec((tm, tn), lambda i, j: (i, j)),
        out_shape=jax.ShapeDtypeStruct((m, n), F32),
        compiler_params=_params("parallel", "arbitrary"),
        name="matmul_residual",
    )(a, w, res)


def _ffn_kernel(*refs, final_norm):
    if final_norm:
        x_ref, gain_ref, wg_ref, wu_ref, wd_ref, fgain_ref, o_ref, xn_ref = refs
    else:
        x_ref, gain_ref, wg_ref, wu_ref, wd_ref, o_ref, xn_ref = refs
    j = pl.program_id(1)

    @pl.when(j == 0)
    def _():
        x = x_ref[...]
        xn_ref[...] = _rmsnorm(x, gain_ref[...]).astype(BF16)
        o_ref[...] = x

    xn = xn_ref[...]
    gate = jnp.dot(xn, wg_ref[...], preferred_element_type=F32)
    up = jnp.dot(xn, wu_ref[...], preferred_element_type=F32)
    hidden = (_silu(gate) * up).astype(BF16)
    o_ref[...] += jnp.dot(hidden, wd_ref[...], preferred_element_type=F32)

    if final_norm:
        @pl.when(j == pl.num_programs(1) - 1)
        def _():
            o_ref[...] = _rmsnorm(o_ref[...], fgain_ref[...])


def _ffn(x, gain, wg, wu, wd, final_gain=None, *, layer, tm):
    m, d = x.shape
    f = wg.shape[2]
    tf = FFN_COL_TILE if tm >= FFN_ROW_TILE else FFN_COL_TILE_SMALL_ROWS
    final_norm = final_gain is not None
    in_specs = [
        pl.BlockSpec((tm, d), lambda i, j: (i, 0)),
        pl.BlockSpec((1, d), lambda i, j: (0, 0)),
        pl.BlockSpec((None, d, tf), lambda i, j: (layer, 0, j)),
        pl.BlockSpec((None, d, tf), lambda i, j: (layer, 0, j)),
        pl.BlockSpec((None, tf, d), lambda i, j: (layer, j, 0)),
    ]
    args = [x, gain, wg, wu, wd]
    if final_norm:
        in_specs.append(pl.BlockSpec((1, d), lambda i, j: (0, 0)))
        args.append(final_gain)
    return pl.pallas_call(
        functools.partial(_ffn_kernel, final_norm=final_norm),
        grid=(m // tm, f // tf),
        in_specs=in_specs,
        out_specs=pl.BlockSpec((tm, d), lambda i, j: (i, 0)),
        out_shape=jax.ShapeDtypeStruct((m, d), F32),
        scratch_shapes=[pltpu.VMEM((tm, d), BF16)],
        compiler_params=_params("parallel", "arbitrary"),
        name="ffn",
    )(*args)


def _kv_kernel(x_ref, gain_ref, w_ref, bias_ref, k32_ref, v32_ref, kh_ref, vh_ref, lf_ref, lf16_ref, *scratch,
               key_tile, tiles_per_seq):
    if tiles_per_seq is not None:
        carry_ref, = scratch

        @pl.when(pl.program_id(0) % tiles_per_seq == 0)
        def _():
            carry_ref[...] = jnp.zeros_like(carry_ref)

    rows = x_ref.shape[0] // ROW_GROUPS
    for r in range(ROW_GROUPS):
        rs = slice(r * rows, (r + 1) * rows)
        xn = _rmsnorm(x_ref[rs, :], gain_ref[...]).astype(BF16)
        y = jnp.dot(xn, w_ref[...], preferred_element_type=F32)
        z = y[:, :LANES] + bias_ref[...]
        lf = jnp.minimum(z, 0.0) - jnp.log1p(jnp.exp(-jnp.abs(z)))
        lf16_ref[rs, :] = lf[:, :FOX_HEADS]
        if tiles_per_seq is None:
            lf_ref[rs, :] = lf
        else:
            lf_ref[rs, :], carry_ref[...] = _prefix_pieces(lf, carry_ref[...])
        v = y[:, LANES:LANES + FOX_KV]
        k = y[:, LANES + FOX_KV:]
        for h in range(FOX_KV_HEADS):
            hs = slice(h * FOX_HD, (h + 1) * FOX_HD)
            out_rows = pl.ds(r * rows * FOX_KV_HEADS + h, rows, stride=FOX_KV_HEADS)
            v32_ref[out_rows, :] = v[:, hs]
            if key_tile is None:
                vh_ref[h, rs, :] = v[:, hs].astype(BF16)
            else:
                v_t = v[:, hs].T.astype(BF16)
                width = min(rows, key_tile)
                for t in range(rows // width):
                    first = r * rows + t * width
                    tile, lanes = first // key_tile, slice(first % key_tile, first % key_tile + width)
                    vh_ref[h, tile, :FOX_HD, lanes] = v_t[:, t * width:(t + 1) * width]
                    vh_ref[h, tile, FOX_HD:, lanes] = jnp.ones((FOX_V_ROWS - FOX_HD, width), BF16)
        for h in range(FOX_KV_HEADS):
            hs = slice(h * FOX_HD, (h + 1) * FOX_HD)
            out_rows = pl.ds(r * rows * FOX_KV_HEADS + h, rows, stride=FOX_KV_HEADS)
            k32_ref[out_rows, :] = k[:, hs]
            kh_ref[h, rs, :] = k[:, hs].astype(BF16)


def _kv_proj(x, gain, w, bias, *, tm, key_tile, seq=None):
    m, d = x.shape
    n = w.shape[1]
    tiles_per_seq = None if seq is None else seq // tm
    if key_tile is None:
        v_spec = pl.BlockSpec((FOX_KV_HEADS, tm, FOX_HD), lambda i: (0, i, 0))
        v_shape = jax.ShapeDtypeStruct((FOX_KV_HEADS, m, FOX_HD), BF16)
    else:
        v_spec = pl.BlockSpec((FOX_KV_HEADS, tm // key_tile, FOX_V_ROWS, key_tile), lambda i: (0, i, 0, 0))
        v_shape = jax.ShapeDtypeStruct((FOX_KV_HEADS, m // key_tile, FOX_V_ROWS, key_tile), BF16)
    return pl.pallas_call(
        functools.partial(_kv_kernel, key_tile=key_tile, tiles_per_seq=tiles_per_seq),
        grid=(m // tm,),
        in_specs=[
            pl.BlockSpec((tm, d), lambda i: (i, 0)),
            pl.BlockSpec((1, d), lambda i: (0, 0)),
            pl.BlockSpec((d, n), lambda i: (0, 0)),
            pl.BlockSpec((1, LANES), lambda i: (0, 0)),
        ],
        out_specs=[
            pl.BlockSpec((tm * FOX_KV_HEADS, FOX_HD), lambda i: (i, 0)),
            pl.BlockSpec((tm * FOX_KV_HEADS, FOX_HD), lambda i: (i, 0)),
            pl.BlockSpec((FOX_KV_HEADS, tm, FOX_HD), lambda i: (0, i, 0)),
            v_spec,
            pl.BlockSpec((tm, LANES), lambda i: (i, 0)),
            pl.BlockSpec((tm, FOX_HEADS), lambda i: (i, 0)),
        ],
        out_shape=[
            jax.ShapeDtypeStruct((m * FOX_KV_HEADS, FOX_HD), F32),
            jax.ShapeDtypeStruct((m * FOX_KV_HEADS, FOX_HD), F32),
            jax.ShapeDtypeStruct((FOX_KV_HEADS, m, FOX_HD), BF16),
            v_shape,
            jax.ShapeDtypeStruct((m, LANES), F32 if seq is None else BF16),
            jax.ShapeDtypeStruct((m, FOX_HEADS), F32),
        ],
        scratch_shapes=[] if seq is None else [pltpu.VMEM((1, LANES), F32)],
        compiler_params=_params("arbitrary"),
        name="kv_proj",
    )(x, gain, w, bias)


def _q_kernel(x_ref, gain_ref, w_ref, o_ref):
    rows = x_ref.shape[0] // ROW_GROUPS
    for r in range(ROW_GROUPS):
        rs = slice(r * rows, (r + 1) * rows)
        xn = _rmsnorm(x_ref[rs, :], gain_ref[...]).astype(BF16)
        y = jnp.dot(xn, w_ref[...], preferred_element_type=F32)
        o_ref[rs, :] = (y * (FOX_HD ** -0.5 * LOG2_E)).astype(BF16)


def _q_proj(x, gain, w, *, tm):
    m, d = x.shape
    n = w.shape[1]
    return pl.pallas_call(
        _q_kernel,
        grid=(m // tm,),
        in_specs=[
            pl.BlockSpec((tm, d), lambda i: (i, 0)),
            pl.BlockSpec((1, d), lambda i: (0, 0)),
            pl.BlockSpec((d, n), lambda i: (0, 0)),
        ],
        out_specs=pl.BlockSpec((tm, n), lambda i: (i, 0)),
        out_shape=jax.ShapeDtypeStruct((m, n), BF16),
        compiler_params=_params("parallel"),
        name="q_proj",
    )(x, gain, w)


def _prefix_pieces(x, carry):
    tl = x.shape[0]
    r = lax.broadcasted_iota(jnp.int32, (tl, tl), 0)
    c = lax.broadcasted_iota(jnp.int32, (tl, tl), 1)
    tri = jnp.where(c <= r, 1.0, 0.0).astype(BF16)
    x_hi = x.astype(BF16)
    x_rest = x - x_hi.astype(F32)
    x_mid = x_rest.astype(BF16)
    x_lo = (x_rest - x_mid.astype(F32)).astype(BF16)
    y = (jnp.dot(tri, x_lo, preferred_element_type=F32) + jnp.dot(tri, x_mid, preferred_element_type=F32)
         + jnp.dot(tri, x_hi, preferred_element_type=F32) + carry)
    y2 = y * LOG2_E
    hi = y2.astype(BF16).astype(F32)
    rest = y2 - hi
    mid = rest.astype(BF16).astype(F32)
    lo = rest - mid
    lane = lax.broadcasted_iota(jnp.int32, y.shape, 1)
    packed = jnp.where(
        lane < FOX_HEADS, hi,
        jnp.where(lane < 2 * FOX_HEADS, pltpu.roll(mid, FOX_HEADS, 1),
                  jnp.where(lane < 3 * FOX_HEADS, pltpu.roll(lo, 2 * FOX_HEADS, 1), 0.0)))
    return packed.astype(BF16), y[tl - 1:tl, :]


def _cumsum_kernel(x_ref, o_ref, carry_ref):
    @pl.when(pl.program_id(1) == 0)
    def _():
        carry_ref[...] = jnp.zeros_like(carry_ref)

    o_ref[0], carry_ref[...] = _prefix_pieces(x_ref[0], carry_ref[...])


def _cumsum_pieces(x, *, tl):
    b, length, lanes = x.shape
    return pl.pallas_call(
        _cumsum_kernel,
        grid=(b, length // tl),
        in_specs=[pl.BlockSpec((1, tl, lanes), lambda i, t: (i, t, 0))],
        out_specs=pl.BlockSpec((1, tl, lanes), lambda i, t: (i, t, 0)),
        out_shape=jax.ShapeDtypeStruct((b, length, lanes), BF16),
        scratch_shapes=[pltpu.VMEM((1, lanes), F32)],
        compiler_params=_params("parallel", "arbitrary"),
        name="logf_cumsum",
    )(x)


def _fox_kernel(q_ref, k_ref, f_ref, v_ref, o_ref, qa_ref, z0_ref, z1_ref, mb0_ref, mb1_ref, m_ref, acc_ref,
                *, tq, tk, q_start, blocks):
    step = pl.program_id(2)
    rows = FOX_GROUP * tq
    lane = lax.broadcasted_iota(jnp.int32, (tq, FOX_HD), 1)
    for x in range(blocks):
        for g in range(FOX_GROUP):
            head = pl.program_id(1) * FOX_GROUP + g
            pick = jnp.where(lane < FOX_F_PIECES * FOX_HEADS,
                             jnp.where((lane & (FOX_HEADS - 1)) == head, -1.0, 0.0), 0.0)
            qa_ref[x, g * tq:(g + 1) * tq, :FOX_HD] = q_ref[0, x * tq:(x + 1) * tq, g * FOX_HD:(g + 1) * FOX_HD]
            qa_ref[x, g * tq:(g + 1) * tq, FOX_HD:] = pick.astype(BF16)
    n_full = (q_start + step * blocks * tq) // tk

    m_ref[...] = jnp.full_like(m_ref, MASK_VALUE)
    acc_ref[...] = jnp.zeros_like(acc_ref)

    cw = min(rows, FOX_COL_CHUNK)
    chunks = [slice(c * cw, (c + 1) * cw) for c in range(rows // cw)]
    buffers = ((z0_ref, mb0_ref), (z1_ref, mb1_ref))

    def score(x, kb, masked, buf, keys=tk):
        z_ref, mb_ref = buffers[buf]
        start = pl.multiple_of(kb * tk, tk)
        ka = jnp.concatenate([k_ref[0, 0, pl.ds(start, keys), :], f_ref[0, pl.ds(start, keys), :]], axis=1)
        for c, cs in enumerate(chunks):
            z = lax.dot_general(ka, qa_ref[x, cs, :], (((1,), (1,)), ((), ())), preferred_element_type=F32)
            if masked:
                kpos = kb * tk + lax.broadcasted_iota(jnp.int32, (keys, cw), 0)
                col = c * cw + lax.broadcasted_iota(jnp.int32, (keys, cw), 1)
                qpos = q_start + (step * blocks + x) * tq + (col & (tq - 1))
                z = jnp.where(kpos <= qpos, z, MASK_VALUE)
            z_ref[:keys, cs] = z
            mb_ref[:, cs] = jnp.max(z, axis=0, keepdims=True)

    def accumulate(x, kb, buf, keys=tk):
        z_ref, mb_ref = buffers[buf]
        vt = v_ref[0, 0, kb, :, :keys]
        for cs in chunks:
            m_old = m_ref[x, :, cs]
            m_new = jnp.maximum(m_old, mb_ref[:, cs])
            alpha = jnp.exp2(m_old - m_new)
            p = jnp.exp2(z_ref[:keys, cs] - m_new)
            acc_ref[x, :, cs] = alpha * acc_ref[x, :, cs] + jnp.dot(vt, p.astype(BF16), preferred_element_type=F32)
            m_ref[x, :, cs] = m_new

    if blocks == 4:
        def run(stages, pending):
            for x, kb, masked, keys in stages:
                buf = 1 - pending[2]
                score(x, kb, masked, buf, keys)
                accumulate(*pending)
                pending = (x, kb, buf, keys)
            return pending

        score(0, n_full, True, 0, tq)
        pending = run([(1, n_full, True, tk), (2, n_full + 1, True, tq), (3, n_full + 1, True, tk),
                       (2, n_full, False, tk), (3, n_full, False, tk)], (0, n_full, 0, tq))
        last_block, _, last_buf, _ = pending

        def trip(i, kb_pending):
            stages = [(x, FOX_TILES_PER_TRIP * i + t, False, tk)
                      for t in range(FOX_TILES_PER_TRIP) for x in range(blocks)]
            return run(stages, (last_block, kb_pending, last_buf, tk))[1]

        kb_pending = lax.fori_loop(0, n_full // FOX_TILES_PER_TRIP, trip, n_full)
        accumulate(last_block, kb_pending, last_buf)
    else:
        score(0, n_full, True, 0)

        def pair(i, pending):
            score(0, 2 * i, False, 1)
            accumulate(0, pending, 0)
            score(0, 2 * i + 1, False, 0)
            accumulate(0, 2 * i, 1)
            return 2 * i + 1

        pending = lax.fori_loop(0, n_full // 2, pair, n_full)

        @pl.when(n_full % 2 == 1)
        def _():
            score(0, n_full - 1, False, 1)
            accumulate(0, pending, 0)
            accumulate(0, n_full - 1, 1)

        @pl.when(n_full % 2 == 0)
        def _():
            accumulate(0, pending, 0)

    for x in range(blocks):
        acc = acc_ref[x]
        out = (acc[:FOX_HD] / acc[FOX_HD:FOX_HD + 1]).T
        for g in range(FOX_GROUP):
            o_ref[0, x * tq:(x + 1) * tq, g * FOX_HD:(g + 1) * FOX_HD] = out[g * tq:(g + 1) * tq, :].astype(BF16)


def _fox_attention(q, k_heads, f_pieces, v_t, *, tq, tk, q_start):
    b, lq, _ = q.shape
    lk = k_heads.shape[2]
    nk = lk // tk
    rows = FOX_GROUP * tq
    span = FOX_TILES_PER_TRIP * tk
    blocks = FOX_BLOCKS_PER_STEP if (FOX_BLOCKS_PER_STEP * tq == span and lq % span == 0 and q_start % span == 0) else 1
    assert tq & (tq - 1) == 0 and tk % tq == 0 and q_start % tk == 0
    return pl.pallas_call(
        functools.partial(_fox_kernel, tq=tq, tk=tk, q_start=q_start, blocks=blocks),
        grid=(b, FOX_KV_HEADS, lq // (blocks * tq)),
        in_specs=[
            pl.BlockSpec((1, blocks * tq, FOX_GROUP * FOX_HD), lambda i, h, t: (i, t, h)),
            pl.BlockSpec((1, 1, lk, FOX_HD), lambda i, h, t: (h, i, 0, 0)),
            pl.BlockSpec((1, lk, LANES), lambda i, h, t: (i, 0, 0)),
            pl.BlockSpec((1, 1, nk, FOX_V_ROWS, tk), lambda i, h, t: (h, i, 0, 0, 0)),
        ],
        out_specs=pl.BlockSpec((1, blocks * tq, FOX_GROUP * FOX_HD), lambda i, h, t: (i, t, h)),
        out_shape=jax.ShapeDtypeStruct(q.shape, BF16),
        scratch_shapes=[
            pltpu.VMEM((blocks, rows, 2 * FOX_HD), BF16),
            pltpu.VMEM((tk, rows), F32),
            pltpu.VMEM((tk, rows), F32),
            pltpu.VMEM((1, rows), F32),
            pltpu.VMEM((1, rows), F32),
            pltpu.VMEM((blocks, 1, rows), F32),
            pltpu.VMEM((blocks, FOX_V_ROWS, rows), F32),
        ],
        compiler_params=_params("parallel", "parallel", "arbitrary"),
        name="fox_attention",
    )(q, k_heads, f_pieces, v_t)


def _rope_tables(pos):
    half = RET_DK // 2
    inv = ROPE_BASE ** (-jnp.arange(half, dtype=F32) / half)
    ang = pos.astype(F32)[:, None] * inv[None, :]
    return jnp.cos(ang), jnp.sin(ang)


def _with_cache(new_heads, cache, lk_pad):
    past = cache.transpose(2, 0, 1, 3).astype(BF16)
    both = jnp.concatenate([past, new_heads], axis=2)
    return jnp.pad(both, ((0, 0), (0, 0), (0, lk_pad - both.shape[2]), (0, 0)))


def _transpose_values(v_heads, tk):
    kvh, b, lk, _ = v_heads.shape
    v_t = v_heads.reshape(kvh, b, lk // tk, tk, FOX_HD).transpose(0, 1, 2, 4, 3)
    ones = jnp.ones((kvh, b, lk // tk, FOX_V_ROWS - FOX_HD, tk), BF16)
    return jnp.concatenate([v_t, ones], axis=3)


def _stream(x, *, pos, s0, cache, weights, later_weights, tm, ffn_tm, chunk, tq):
    norm_mix, norm_ffn, norm_kv, norm_final, w_ret_in, w_ret_o, w_kv, b_kv, w_fox_q, w_fox_o = weights
    b, length, d = x.shape
    m = b * length
    h = x.reshape(m, d)

    cos, sin = _rope_tables(pos)
    if cos.shape[0] < tm:
        cos = jnp.tile(cos, (tm // cos.shape[0], 1))
        sin = jnp.tile(sin, (tm // sin.shape[0], 1))
    cast = () if later_weights[0].dtype == BF16 else later_weights
    if s0 is None:
        o, s_fin, narrowed = _ret_fused(h, norm_mix[0:1], cos, sin, w_ret_in, cast,
                                        batch=b, seq=length, tm=tm, chunk=chunk)
    else:
        proj, narrowed = _ret_in(h, norm_mix[0:1], cos, sin, w_ret_in, cast, tm=tm)
        o, s_fin = _retention(proj, s0, batch=b, seq=length, chunk=chunk)
    later_weights = narrowed if cast else later_weights
    w_ffn_gate, w_ffn_up, w_ffn_down = later_weights
    h = _mm_res(o, w_ret_o, h, tm=tm, tn=COL_TILE)
    h = _ffn(h, norm_ffn[0:1], w_ffn_gate, w_ffn_up, w_ffn_down, layer=0, tm=ffn_tm)

    tk = FOX_K_TILE
    if cache is None:
        q_start = 0
        k32, v32, k_heads, v_t, f_pieces, lf = _kv_proj(
            h, norm_kv[None, :], w_kv, b_kv, tm=tm, key_tile=tk, seq=length)
        k_heads = k_heads.reshape(FOX_KV_HEADS, b, length, FOX_HD)
        v_t = v_t.reshape(FOX_KV_HEADS, b, length // tk, FOX_V_ROWS, tk)
        f_pieces = f_pieces.reshape(b, length, LANES)
    else:
        cache_k, cache_v, cache_logf = cache
        q_start = cache_k.shape[1]
        lk_pad = -(-(q_start + length) // tk) * tk
        k32, v32, k_heads, v_heads, lf_pad, lf = _kv_proj(h, norm_kv[None, :], w_kv, b_kv, tm=tm, key_tile=None)
        k_heads = _with_cache(k_heads.reshape(FOX_KV_HEADS, b, length, FOX_HD), cache_k, lk_pad)
        v_t = _transpose_values(_with_cache(v_heads.reshape(FOX_KV_HEADS, b, length, FOX_HD), cache_v, lk_pad), tk)
        lf_all = jnp.concatenate([
            jnp.pad(cache_logf.astype(F32), ((0, 0), (0, 0), (0, LANES - FOX_HEADS))),
            lf_pad.reshape(b, length, LANES)], axis=1)
        lf_all = jnp.pad(lf_all, ((0, 0), (0, lk_pad - lf_all.shape[1]), (0, 0)))
        f_pieces = _cumsum_pieces(lf_all, tl=CUMSUM_TILE)
    lf = lf.reshape(b, length, FOX_HEADS)

    q = _q_proj(h, norm_mix[1:2], w_fox_q, tm=tm).reshape(b, length, d)
    att = _fox_attention(q, k_heads, f_pieces, v_t, tq=tq, tk=tk, q_start=q_start)
    h = _mm_res(att.reshape(m, d), w_fox_o, h, tm=tm, tn=d)
    y = _ffn(h, norm_ffn[1:2], w_ffn_gate, w_ffn_up, w_ffn_down, norm_final[None, :], layer=1, tm=ffn_tm)

    outputs = (y.reshape(b, length, d), s_fin[None],
               k32.reshape(b, length, FOX_KV_HEADS, FOX_HD), v32.reshape(b, length, FOX_KV_HEADS, FOX_HD), lf)
    return outputs, later_weights


def kernel(x_prompt, x_sample, state_ret, cache_k, cache_v, cache_logf, norm_mix, norm_ffn, norm_kv, norm_final, w_ret_in, w_ret_o, w_kv_k, w_kv_v, w_kv_f, b_kv_f, w_fox_q, w_fox_o, w_ffn_gate, w_ffn_up, w_ffn_down):
    lp = x_prompt.shape[1]
    bs, ls, _ = x_sample.shape
    past = cache_k.shape[1]

    w_kv = jnp.concatenate(
        [jnp.pad(w_kv_f, ((0, 0), (0, LANES - FOX_HEADS))), w_kv_v, w_kv_k], axis=1).astype(BF16)
    b_kv = jnp.pad(b_kv_f.astype(F32), (0, LANES - FOX_HEADS))[None, :]
    weights = (norm_mix.astype(F32), norm_ffn.astype(F32), norm_kv.astype(F32), norm_final.astype(F32),
               w_ret_in[0].astype(BF16), w_ret_o[0].astype(BF16), w_kv, b_kv,
               w_fox_q[0].astype(BF16), w_fox_o[0].astype(BF16))
    later_weights = tuple(w.astype(F32) for w in (w_ffn_gate, w_ffn_up, w_ffn_down))

    (y_p, s_p, k_p, v_p, lf_p), later_weights = _stream(
        x_prompt, pos=jnp.arange(lp), s0=None, cache=None, weights=weights, later_weights=later_weights,
        tm=ROW_TILE, ffn_tm=FFN_ROW_TILE, chunk=RET_CHUNK, tq=FOX_Q_TILE)
    (y_s, s_s, k_s, v_s, lf_s), _ = _stream(
        x_sample, pos=past + jnp.arange(ls), s0=state_ret[0], cache=(cache_k, cache_v, cache_logf),
        weights=weights, later_weights=later_weights, tm=bs * ls, ffn_tm=bs * ls, chunk=ls, tq=ls)
    return (y_p, y_s, s_p, k_p, v_p, lf_p, s_s, k_s, v_s, lf_s)
```
